```python
import math, functools
import jax, jax.numpy as jnp
from jax import lax
import numpy as np

D_MODEL = 2048
BATCH = 2
SEQ = 4096
DEPTH = 1
DEC_BATCH = 128
DEC_SEQ = 4
PAST_LEN = 16384
PAGE_SIZE = 128

MLA_HEADS = 16
Q_LORA = 512
KV_LORA = 512
QK_NOPE = 128
QK_ROPE = 64
V_HEAD = 128
MLA_WIDTH = MLA_HEADS * V_HEAD
MLA_SCALE = (QK_NOPE + QK_ROPE) ** -0.5
ROPE_THETA = 10000.0
Q_BLOCK = 128
HG_HEADS = 16
HG_DK = D_MODEL // HG_HEADS
HG_DV = D_MODEL // HG_HEADS
HG_WIDTH = HG_HEADS * HG_DK
HG_CHUNK = 64
MEM_TOKENS = 256
X_HEADS = 4
X_HEAD_DIM = 128
X_WIDTH = X_HEADS * X_HEAD_DIM
N_GROUPS = 8
EXPERTS_PER_GROUP = 8
N_EXPERTS = N_GROUPS * EXPERTS_PER_GROUP
TOP_K = 2
D_EXPERT = 512
MOE_BLOCK = 128
EPS = 1e-6
NEG = -1e30
IN_SPLITS = (Q_LORA, KV_LORA, QK_ROPE, HG_WIDTH, HG_WIDTH, HG_WIDTH, HG_WIDTH, D_MODEL, D_MODEL)
N_IN = sum(IN_SPLITS)

kernel_name = "mla_hgrn2_gated_hier_moe_step"


def rmsnorm(x, g):
    xf = x.astype(jnp.float32)
    y = xf * lax.rsqrt(jnp.mean(xf * xf, axis=-1, keepdims=True) + EPS)
    return (y * g.astype(jnp.float32)).astype(x.dtype)


def rope_cos_sin(pos):
    half = QK_ROPE // 2
    inv = ROPE_THETA ** (-jnp.arange(half, dtype=jnp.float32) / half)
    ang = pos.astype(jnp.float32)[:, None] * inv[None, :]
    return jnp.cos(ang), jnp.sin(ang)


def apply_rope(x, cos, sin):
    xf = x.astype(jnp.float32)
    x1, x2 = jnp.split(xf, 2, axis=-1)
    return jnp.concatenate([x1 * cos - x2 * sin, x2 * cos + x1 * sin], axis=-1).astype(x.dtype)


def split_columns(z):
    idx = [int(i) for i in np.cumsum(IN_SPLITS)[:-1]]
    return jnp.split(z, idx, axis=-1)


def mla_project(c_q, c_kv, k_rope_raw, pos, lw):
    B, T = c_q.shape[:2]
    q = (rmsnorm(c_q, lw['q_norm']) @ lw['w_uq']).reshape(B, T, MLA_HEADS, QK_NOPE + QK_ROPE)
    cos, sin = rope_cos_sin(pos)
    q_nope = q[..., :QK_NOPE]
    q_rope = apply_rope(q[..., QK_NOPE:], cos[:, None, :], sin[:, None, :])
    lat = rmsnorm(c_kv, lw['kv_norm'])
    k_rope = apply_rope(k_rope_raw, cos, sin)
    return q_nope, q_rope, lat, k_rope


def mla_prompt_attention(q_nope, q_rope, lat, k_rope, lw):
    B, T = lat.shape[:2]
    k_nope = jnp.einsum('btc,chn->bthn', lat, lw['w_uk'].reshape(KV_LORA, MLA_HEADS, QK_NOPE))
    v = jnp.einsum('btc,chv->bthv', lat, lw['w_uv'].reshape(KV_LORA, MLA_HEADS, V_HEAD))
    nb = T // Q_BLOCK
    qn = q_nope.reshape(B, nb, Q_BLOCK, MLA_HEADS, QK_NOPE).transpose(1, 0, 2, 3, 4)
    qr = q_rope.reshape(B, nb, Q_BLOCK, MLA_HEADS, QK_ROPE).transpose(1, 0, 2, 3, 4)
    key_pos = jnp.arange(T)

    def block(args):
        qn_b, qr_b, start = args
        s = jnp.einsum('bqhn,bkhn->bhqk', qn_b, k_nope) + jnp.einsum('bqhr,bkr->bhqk', qr_b, k_rope)
        s = s.astype(jnp.float32) * MLA_SCALE
        q_pos = start + jnp.arange(Q_BLOCK)
        s = jnp.where(key_pos[None, :] <= q_pos[:, None], s, NEG)
        p = jax.nn.softmax(s, axis=-1)
        return jnp.einsum('bhqk,bkhv->bqhv', p.astype(v.dtype), v)

    o = lax.map(block, (qn, qr, jnp.arange(nb) * Q_BLOCK))
    return o.transpose(1, 0, 2, 3, 4).reshape(B, T, MLA_WIDTH)


def online_update(carry, s, c):
    m, l, acc = carry
    m_new = jnp.maximum(m, s.max(-1))
    corr = jnp.exp(m - m_new)
    p = jnp.exp(s - m_new[..., None])
    acc = acc * corr[..., None] + jnp.einsum('bhqk,bkc->bhqc', p, c.astype(jnp.float32))
    return (m_new, l * corr + p.sum(-1), acc)


def mla_sample_attention(q_nope, q_rope, lat, k_rope, cache_lat, cache_kr, page_table, lw):
    Bd, T = lat.shape[:2]
    q_lat = jnp.einsum('bqhn,chn->bqhc', q_nope, lw['w_uk'].reshape(KV_LORA, MLA_HEADS, QK_NOPE))

    def scores(c, kr):
        s = jnp.einsum('bqhc,bkc->bhqk', q_lat, c) + jnp.einsum('bqhr,bkr->bhqk', q_rope, kr)
        return s.astype(jnp.float32) * MLA_SCALE

    def page_step(carry, pid):
        c = cache_lat[pid]
        kr = cache_kr[pid]
        return online_update(carry, scores(c, kr), c), None

    init = (jnp.full((Bd, MLA_HEADS, T), NEG, jnp.float32),
            jnp.zeros((Bd, MLA_HEADS, T), jnp.float32),
            jnp.zeros((Bd, MLA_HEADS, T, KV_LORA), jnp.float32))
    carry, _ = lax.scan(page_step, init, page_table.T)
    causal = jnp.tril(jnp.ones((T, T), bool))
    s_new = jnp.where(causal, scores(lat, k_rope), NEG)
    m, l, acc = online_update(carry, s_new, lat)
    o_lat = (acc / l[..., None]).astype(lat.dtype)
    o = jnp.einsum('bhqc,chv->bqhv', o_lat, lw['w_uv'].reshape(KV_LORA, MLA_HEADS, V_HEAD))
    return o.reshape(Bd, T, MLA_WIDTH)


def gla_chunked(q, k, v, logf, S0, chunk):
    B, T, H, DK = q.shape
    DV = v.shape[-1]
    n = T // chunk

    def to_chunks(a):
        return a.reshape(B, n, chunk, H, a.shape[-1]).transpose(1, 0, 2, 3, 4)

    causal = jnp.tril(jnp.ones((chunk, chunk), bool))

    def step(S, inp):
        qc, kc, vc, gc = inp
        b = jnp.cumsum(gc, axis=1)
        b_last = b[:, -1]
        q_dec = qc * jnp.exp(b)
        k_inv = kc * jnp.exp(-b)
        att = jnp.where(causal, jnp.einsum('bthk,bshk->bhts', q_dec, k_inv), 0.0)
        o = jnp.einsum('bhts,bshv->bthv', att, vc) + jnp.einsum('bthk,bhkv->bthv', q_dec, S)
        k_end = kc * jnp.exp(b_last[:, None] - b)
        S = jnp.exp(b_last)[..., None] * S + jnp.einsum('bshk,bshv->bhkv', k_end, vc)
        return S, o

    S, o = lax.scan(step, S0, (to_chunks(q), to_chunks(k), to_chunks(v), to_chunks(logf)))
    return o.transpose(1, 0, 2, 3, 4).reshape(B, T, H, DV), S


def hgrn2_branch(hq, hf, hi, hg, lb, S0, lw):
    B, T = hq.shape[:2]
    f32 = jnp.float32
    q = jax.nn.silu(hq.astype(f32)).reshape(B, T, HG_HEADS, HG_DK) * HG_DK ** -0.5
    logf = jnp.log(lb + (1.0 - lb) * jax.nn.sigmoid(hf.astype(f32))).reshape(B, T, HG_HEADS, HG_DK)
    k = 1.0 - jnp.exp(logf)
    v = hi.astype(f32).reshape(B, T, HG_HEADS, HG_DV)
    o, S = gla_chunked(q, k, v, logf, S0.astype(f32), math.gcd(HG_CHUNK, T))
    o = rmsnorm(o.astype(hq.dtype), lw['hg_norm']) * jax.nn.silu(hg.reshape(B, T, HG_HEADS, HG_DV))
    return o.reshape(B, T, HG_WIDTH), S.astype(S0.dtype)


def token_mix(h, pos, S0, attend, lb, lw):
    c_q, c_kv, kr_raw, hq, hf, hi, hg, ga, gb = split_columns(h @ lw['w_in'])
    q_nope, q_rope, lat, k_rope = mla_project(c_q, c_kv, kr_raw, pos, lw)
    a = attend(q_nope, q_rope, lat, k_rope)
    b, S = hgrn2_branch(hq, hf, hi, hg, lb, S0, lw)
    merged = jax.nn.sigmoid(ga) * (a @ lw['w_out_mla']) + jax.nn.sigmoid(gb) * (b @ lw['w_out_hgrn'])
    return merged @ lw['w_out'], lat, k_rope, S


def memory_kv(mem, norm_mem, lw):
    B, M = mem.shape[:2]
    kv = rmsnorm(mem, norm_mem) @ lw['w_ckv']
    mk, mv = jnp.split(kv, 2, axis=-1)
    return mk.reshape(B, M, X_HEADS, X_HEAD_DIM), mv.reshape(B, M, X_HEADS, X_HEAD_DIM)


def cross_attention(h, mk, mv, lw):
    B, T = h.shape[:2]
    q = (h @ lw['w_cq']).reshape(B, T, X_HEADS, X_HEAD_DIM)
    s = jnp.einsum('bqhd,bmhd->bhqm', q, mk).astype(jnp.float32) * X_HEAD_DIM ** -0.5
    p = jax.nn.softmax(s, axis=-1)
    o = jnp.einsum('bhqm,bmhd->bqhd', p.astype(mv.dtype), mv).reshape(B, T, X_WIDTH)
    return o @ lw['w_co']


def grouped_expert_ffn(x, expert_idx, gate, w_gu, w_dn):
    N, D = x.shape
    A = N * TOP_K
    flat_e = expert_idx.reshape(A)
    order = jnp.argsort(flat_e)
    e_sorted = flat_e[order]
    counts = jnp.bincount(flat_e, length=N_EXPERTS)
    padded = (counts + MOE_BLOCK - 1) // MOE_BLOCK * MOE_BLOCK
    pad_end = jnp.cumsum(padded)
    pad_start = pad_end - padded
    raw_start = jnp.cumsum(counts) - counts
    dest = pad_start[e_sorted] + (jnp.arange(A) - raw_start[e_sorted])
    n_blocks = (A + N_EXPERTS * (MOE_BLOCK - 1) + MOE_BLOCK - 1) // MOE_BLOCK
    n_rows = n_blocks * MOE_BLOCK
    row_token = jnp.full((n_rows,), N, jnp.int32).at[dest].set((order // TOP_K).astype(jnp.int32))
    row_gate = jnp.zeros((n_rows,), jnp.float32).at[dest].set(gate.reshape(A)[order])
    block_expert = jnp.minimum(jnp.searchsorted(pad_end, jnp.arange(n_blocks) * MOE_BLOCK, side='right'), N_EXPERTS - 1)
    x_pad = jnp.concatenate([x, jnp.zeros((1, D), x.dtype)], axis=0)
    xb = x_pad[row_token].reshape(n_blocks, MOE_BLOCK, D)

    def expert_block(args):
        xblk, e = args
        g, u = jnp.split(xblk @ w_gu[e], 2, axis=-1)
        return (jax.nn.silu(g) * u) @ w_dn[e]

    yb = lax.map(expert_block, (xb, block_expert)).reshape(n_rows, D)
    y = jax.ops.segment_sum(yb.astype(jnp.float32) * row_gate[:, None], row_token, num_segments=N + 1)[:N]
    return y.astype(x.dtype)


def hier_moe(h, lw):
    B, T, D = h.shape
    N = B * T
    x = h.reshape(N, D)
    p_group = jax.nn.softmax((x @ lw['w_rg'] + lw['b_rg']).astype(jnp.float32), axis=-1)
    g_w, g_idx = lax.top_k(p_group, 1)
    e_logits = (x @ lw['w_re'] + lw['b_re']).astype(jnp.float32).reshape(N, N_GROUPS, EXPERTS_PER_GROUP)
    in_group = jnp.take_along_axis(e_logits, g_idx[:, :, None], axis=1)[:, 0]
    top_logit, top_local = lax.top_k(in_group, TOP_K)
    gate = g_w * jax.nn.softmax(top_logit, axis=-1)
    expert_idx = g_idx * EXPERTS_PER_GROUP + top_local
    return grouped_expert_ffn(x, expert_idx, gate, lw['w_gu'], lw['w_dn']).reshape(B, T, D)


def setup_inputs(seed: int = 0) -> dict:
    key = jax.random.key(seed)
    ks = iter(jax.random.split(key, 48))
    f32 = jnp.float32

    def nrm(shape, scale=1.0):
        return jax.random.normal(next(ks), shape, f32) * scale

    def gain(dim):
        return 1.0 + 0.1 * nrm((DEPTH, dim))

    n_pages = PAST_LEN // PAGE_SIZE
    n_phys = (DEC_BATCH * n_pages * 5) // 4
    page_table = jax.random.permutation(next(ks), n_phys)[:DEC_BATCH * n_pages].reshape(DEC_BATCH, n_pages).astype(jnp.int32)
    return {
        'x_prompt': nrm((BATCH, SEQ, D_MODEL)),
        'mem_prompt': nrm((BATCH, MEM_TOKENS, D_MODEL)),
        'x_sample': nrm((DEC_BATCH, DEC_SEQ, D_MODEL)),
        'cache_kv_latent': nrm((DEPTH, n_phys, PAGE_SIZE, KV_LORA)),
        'cache_k_rope': nrm((DEPTH, n_phys, PAGE_SIZE, QK_ROPE)),
        'page_table': page_table,
        'state_hgrn': nrm((DEPTH, DEC_BATCH, HG_HEADS, HG_DK, HG_DV), 0.5),
        'cache_mem_k': nrm((DEPTH, DEC_BATCH, MEM_TOKENS, X_HEADS, X_HEAD_DIM)),
        'cache_mem_v': nrm((DEPTH, DEC_BATCH, MEM_TOKENS, X_HEADS, X_HEAD_DIM)),
        'norm_mix': gain(D_MODEL),
        'w_in': nrm((DEPTH, D_MODEL, N_IN), D_MODEL ** -0.5),
        'q_norm': gain(Q_LORA),
        'w_uq': nrm((DEPTH, Q_LORA, MLA_HEADS * (QK_NOPE + QK_ROPE)), Q_LORA ** -0.5),
        'kv_norm': gain(KV_LORA),
        'w_uk': nrm((DEPTH, KV_LORA, MLA_HEADS * QK_NOPE), KV_LORA ** -0.5),
        'w_uv': nrm((DEPTH, KV_LORA, MLA_HEADS * V_HEAD), KV_LORA ** -0.5),
        'lb_logits': nrm((DEPTH + 1, HG_WIDTH), 0.1),
        'hg_norm': gain(HG_DV),
        'w_out_mla': nrm((DEPTH, MLA_WIDTH, D_MODEL), MLA_WIDTH ** -0.5),
        'w_out_hgrn': nrm((DEPTH, HG_WIDTH, D_MODEL), HG_WIDTH ** -0.5),
        'w_out': nrm((DEPTH, D_MODEL, D_MODEL), D_MODEL ** -0.5),
        'norm_cross': gain(D_MODEL),
        'norm_mem': gain(D_MODEL),
        'w_cq': nrm((DEPTH, D_MODEL, X_WIDTH), D_MODEL ** -0.5),
        'w_ckv': nrm((DEPTH, D_MODEL, 2 * X_WIDTH), D_MODEL ** -0.5),
        'w_co': nrm((DEPTH, X_WIDTH, D_MODEL), X_WIDTH ** -0.5),
        'norm_ffn': gain(D_MODEL),
        'w_router_group': nrm((DEPTH, D_MODEL, N_GROUPS), D_MODEL ** -0.5),
        'b_router_group': nrm((DEPTH, N_GROUPS), 0.01),
        'w_router_expert': nrm((DEPTH, D_MODEL, N_EXPERTS), D_MODEL ** -0.5),
        'b_router_expert': nrm((DEPTH, N_EXPERTS), 0.01),
        'w_expert_gate_up': nrm((DEPTH, N_EXPERTS, D_MODEL, 2 * D_EXPERT), D_MODEL ** -0.5),
        'w_expert_down': nrm((DEPTH, N_EXPERTS, D_EXPERT, D_MODEL), D_EXPERT ** -0.5),
        'norm_final': 1.0 + 0.1 * nrm((D_MODEL,)),
    }


def reference(x_prompt, mem_prompt, x_sample, cache_kv_latent, cache_k_rope, page_table, state_hgrn,
              cache_mem_k, cache_mem_v, norm_mix, w_in, q_norm, w_uq, kv_norm, w_uk, w_uv, lb_logits,
              hg_norm, w_out_mla, w_out_hgrn, w_out, norm_cross, norm_mem, w_cq, w_ckv, w_co, norm_ffn,
              w_router_group, b_router_group, w_router_expert, b_router_expert, w_expert_gate_up,
              w_expert_down, norm_final):
    pos_prompt = jnp.arange(SEQ)
    pos_sample = PAST_LEN + jnp.arange(DEC_SEQ)
    lower_bounds = jnp.cumsum(jax.nn.softmax(lb_logits.astype(jnp.float32), axis=0), axis=0)
    xp, xs = x_prompt, x_sample
    lat_p_l, kr_p_l, S_p_l, mk_p_l, mv_p_l, lat_s_l, kr_s_l, S_s_l = [], [], [], [], [], [], [], []
    for l in range(DEPTH):
        lw = {'w_in': w_in[l], 'q_norm': q_norm[l], 'w_uq': w_uq[l], 'kv_norm': kv_norm[l],
              'w_uk': w_uk[l], 'w_uv': w_uv[l], 'hg_norm': hg_norm[l], 'w_out_mla': w_out_mla[l],
              'w_out_hgrn': w_out_hgrn[l], 'w_out': w_out[l], 'w_cq': w_cq[l], 'w_ckv': w_ckv[l],
              'w_co': w_co[l], 'w_rg': w_router_group[l], 'b_rg': b_router_group[l],
              'w_re': w_router_expert[l], 'b_re': b_router_expert[l], 'w_gu': w_expert_gate_up[l],
              'w_dn': w_expert_down[l]}
        lb = lower_bounds[l]
        attend_p = functools.partial(mla_prompt_attention, lw=lw)
        attend_s = functools.partial(mla_sample_attention, cache_lat=cache_kv_latent[l],
                                     cache_kr=cache_k_rope[l], page_table=page_table, lw=lw)
        S0_p = jnp.zeros((BATCH, HG_HEADS, HG_DK, HG_DV), jnp.float32)
        mix_p, lat_p, kr_p, S_p = token_mix(rmsnorm(xp, norm_mix[l]), pos_prompt, S0_p, attend_p, lb, lw)
        mix_s, lat_s, kr_s, S_s = token_mix(rmsnorm(xs, norm_mix[l]), pos_sample, state_hgrn[l], attend_s, lb, lw)
        xp = xp + mix_p
        xs = xs + mix_s
        mk_p, mv_p = memory_kv(mem_prompt, norm_mem[l], lw)
        xp = xp + cross_attention(rmsnorm(xp, norm_cross[l]), mk_p, mv_p, lw)
        xs = xs + cross_attention(rmsnorm(xs, norm_cross[l]), cache_mem_k[l], cache_mem_v[l], lw)
        xp = xp + hier_moe(rmsnorm(xp, norm_ffn[l]), lw)
        xs = xs + hier_moe(rmsnorm(xs, norm_ffn[l]), lw)
        lat_p_l.append(lat_p); kr_p_l.append(kr_p); S_p_l.append(S_p)
        mk_p_l.append(mk_p); mv_p_l.append(mv_p)
        lat_s_l.append(lat_s); kr_s_l.append(kr_s); S_s_l.append(S_s)
    y_prompt = rmsnorm(xp, norm_final)
    y_sample = rmsnorm(xs, norm_final)
    n_prompt_pages = BATCH * (SEQ // PAGE_SIZE)
    kv_latent_prompt = jnp.stack(lat_p_l).reshape(DEPTH, n_prompt_pages, PAGE_SIZE, KV_LORA)
    k_rope_prompt = jnp.stack(kr_p_l).reshape(DEPTH, n_prompt_pages, PAGE_SIZE, QK_ROPE)
    state_hgrn_prompt = jnp.stack(S_p_l)
    mem_k_prompt = jnp.stack(mk_p_l)
    mem_v_prompt = jnp.stack(mv_p_l)
    kv_latent_sample = jnp.stack(lat_s_l)
    k_rope_sample = jnp.stack(kr_s_l)
    state_hgrn_sample = jnp.stack(S_s_l)
    return (y_prompt, y_sample, kv_latent_prompt, k_rope_prompt, state_hgrn_prompt, mem_k_prompt, mem_v_prompt, kv_latent_sample, k_rope_sample, state_hgrn_sample)
```

```python
import functools
import math

import jax
import jax.numpy as jnp
from jax import lax
from jax.experimental import pallas as pl
from jax.experimental.pallas import tpu as pltpu

F32 = jnp.float32
BF16 = jnp.bfloat16

EPS = 1e-6
NEG = -1e30
ROPE_THETA = 10000.0
TOP_K = 2
HG_CHUNK = 64
LANES = 128
SUBLANES = 8
VMEM_LIMIT = 56 * 1024 * 1024
PAGE_GROUP = 8


def _params(n_axes):
    return pltpu.CompilerParams(dimension_semantics=("arbitrary",) * n_axes, vmem_limit_bytes=VMEM_LIMIT)


def _tile(n, cap, unit=LANES):
    if n <= cap:
        return n
    t = cap - cap % unit
    while t >= unit:
        if n % t == 0:
            return t
        t -= unit
    raise ValueError(f"no tile for {n} under {cap}")


def _rms(x, g):
    return x * lax.rsqrt(jnp.mean(x * x, axis=-1, keepdims=True) + EPS) * g


def _dot(a, b):
    return jnp.dot(a, b, preferred_element_type=F32)


def _dot_nt(a, b):
    return lax.dot_general(a, b, (((1,), (1,)), ((), ())), preferred_element_type=F32)


def _dot_tn(a, b):
    return lax.dot_general(a, b, (((0,), (0,)), ((), ())), preferred_element_type=F32)


def _mm_kernel(*refs, has_gain, has_res):
    it = iter(refs)
    x_ref, w_ref = next(it), next(it)
    g_ref = next(it) if has_gain else None
    r_ref = next(it) if has_res else None
    o_ref = next(it)
    if has_gain:
        xs_ref = next(it)

        @pl.when(pl.program_id(1) == 0)
        def _():
            xs_ref[...] = _rms(x_ref[...].astype(F32), g_ref[...])

        x = xs_ref[...]
    else:
        x = x_ref[...]
    acc = _dot(x, w_ref[...])
    if has_res:
        acc = r_ref[...] + acc
    o_ref[...] = acc.astype(o_ref.dtype)


def _matmul(x, w, *, gain=None, res=None, x_col=0, tm=512, tn_cap=512, out_dtype=F32, name="mm"):
    m = x.shape[0]
    k, n = w.shape
    assert x_col % k == 0 and m % tm == 0
    tn = _tile(n, tn_cap)
    xb = x_col // k
    in_specs = [pl.BlockSpec((tm, k), lambda i, j: (i, xb)), pl.BlockSpec((k, tn), lambda i, j: (0, j))]
    args = [x, w]
    scratch = []
    if gain is not None:
        in_specs.append(pl.BlockSpec((1, k), lambda i, j: (0, 0)))
        args.append(gain.reshape(1, k).astype(F32))
        scratch.append(pltpu.VMEM((tm, k), F32))
    if res is not None:
        in_specs.append(pl.BlockSpec((tm, tn), lambda i, j: (i, j)))
        args.append(res)
    return pl.pallas_call(
        functools.partial(_mm_kernel, has_gain=gain is not None, has_res=res is not None),
        grid=(m // tm, n // tn),
        in_specs=in_specs,
        out_specs=pl.BlockSpec((tm, tn), lambda i, j: (i, j)),
        out_shape=jax.ShapeDtypeStruct((m, n), out_dtype),
        scratch_shapes=scratch,
        compiler_params=_params(2),
        name=name,
    )(*args)


def _rope_tile(x, cos_t, sin_t, half):
    lane = lax.broadcasted_iota(jnp.int32, x.shape, 1)
    swapped = jnp.where(lane < half, pltpu.roll(x, LANES - half, 1), pltpu.roll(x, half, 1))
    return x * cos_t + swapped * sin_t


def _rope_tables(pos, half):
    inv = ROPE_THETA ** (-jnp.arange(half, dtype=F32) / half)
    ang = pos.astype(F32)[:, None] * inv[None, :]
    cos, sin = jnp.cos(ang), jnp.sin(ang)
    zeros = jnp.zeros((pos.shape[0], LANES - 2 * half), F32)
    return jnp.concatenate([cos, cos, zeros], axis=1), jnp.concatenate([-sin, sin, zeros], axis=1)


def _latkr_kernel(ckv_ref, kr_ref, g_ref, cos_ref, sin_ref, lat_ref, kro_ref, *, half):
    lat_ref[...] = _rms(ckv_ref[...], g_ref[...])
    kro_ref[...] = _rope_tile(kr_ref[...], cos_ref[...], sin_ref[...], half)


def _lat_kr(z, kv_norm, cos_t, sin_t, *, off_ckv, off_kr, kv_lora, half, tm=512):
    m = z.shape[0]
    return pl.pallas_call(
        functools.partial(_latkr_kernel, half=half),
        grid=(m // tm,),
        in_specs=[
            pl.BlockSpec((tm, kv_lora), lambda i: (i, off_ckv // kv_lora)),
            pl.BlockSpec((tm, LANES), lambda i: (i, off_kr // LANES)),
            pl.BlockSpec((1, kv_lora), lambda i: (0, 0)),
            pl.BlockSpec((tm, LANES), lambda i: (i, 0)),
            pl.BlockSpec((tm, LANES), lambda i: (i, 0)),
        ],
        out_specs=[pl.BlockSpec((tm, kv_lora), lambda i: (i, 0)), pl.BlockSpec((tm, LANES), lambda i: (i, 0))],
        out_shape=[jax.ShapeDtypeStruct((m, kv_lora), F32), jax.ShapeDtypeStruct((m, LANES), F32)],
        compiler_params=_params(1),
        name="lat_kr",
    )(z, z, kv_norm.reshape(1, kv_lora), cos_t, sin_t)


def _qproj_kernel(cq_ref, w_ref, g_ref, cos_ref, sin_ref, o_ref, xs_ref, *, nope, half):
    @pl.when(pl.program_id(1) == 0)
    def _():
        xs_ref[...] = _rms(cq_ref[...], g_ref[...])

    q = _dot(xs_ref[...], w_ref[0])
    o_ref[0, :, :nope] = q[:, :nope].astype(o_ref.dtype)
    o_ref[0, :, nope:] = _rope_tile(q[:, nope:], cos_ref[...], sin_ref[...], half).astype(o_ref.dtype)


def _q_proj(z, w_uq_p, q_norm, cos_t, sin_t, *, off_cq, q_lora, nope, half, tm=512):
    m = z.shape[0]
    heads, _, width = w_uq_p.shape
    return pl.pallas_call(
        functools.partial(_qproj_kernel, nope=nope, half=half),
        grid=(m // tm, heads),
        in_specs=[
            pl.BlockSpec((tm, q_lora), lambda i, h: (i, off_cq // q_lora)),
            pl.BlockSpec((1, q_lora, width), lambda i, h: (h, 0, 0)),
            pl.BlockSpec((1, q_lora), lambda i, h: (0, 0)),
            pl.BlockSpec((tm, LANES), lambda i, h: (i, 0)),
            pl.BlockSpec((tm, LANES), lambda i, h: (i, 0)),
        ],
        out_specs=pl.BlockSpec((1, tm, width), lambda i, h: (h, i, 0)),
        out_shape=jax.ShapeDtypeStruct((heads, m, width), BF16),
        scratch_shapes=[pltpu.VMEM((tm, q_lora), F32)],
        compiler_params=_params(2),
        name="q_proj",
    )(z, w_uq_p, q_norm.reshape(1, q_lora), cos_t, sin_t)


def _kproj_kernel(lat_ref, w_ref, kr_ref, o_ref, *, nope):
    o_ref[0, :, :nope] = _dot(lat_ref[...], w_ref[...]).astype(o_ref.dtype)
    o_ref[0, :, nope:] = kr_ref[...].astype(o_ref.dtype)


def _k_proj(lat, w_uk, kr_pad, *, rows, heads, nope, tm=512):
    kv_lora = lat.shape[1]
    return pl.pallas_call(
        functools.partial(_kproj_kernel, nope=nope),
        grid=(rows // tm, heads),
        in_specs=[
            pl.BlockSpec((tm, kv_lora), lambda i, h: (i, 0)),
            pl.BlockSpec((kv_lora, nope), lambda i, h: (0, h)),
            pl.BlockSpec((tm, LANES), lambda i, h: (i, 0)),
        ],
        out_specs=pl.BlockSpec((1, tm, nope + LANES), lambda i, h: (h, i, 0)),
        out_shape=jax.ShapeDtypeStruct((heads, rows, nope + LANES), BF16),
        compiler_params=_params(2),
        name="k_proj",
    )(lat, w_uk, kr_pad)


def _flash_kernel(q_ref, k_ref, v_ref, o_ref, *, tq, scale):
    qi = pl.program_id(2)
    q = q_ref[0]
    dv = v_ref.shape[1]

    def step(kb, carry, masked):
        m, l, acc = carry
        start = pl.multiple_of(kb * tq, tq)
        k = k_ref[0, pl.ds(start, tq), :]
        v = v_ref[pl.ds(start, tq), :]
        s = _dot_nt(q, k) * scale
        if masked:
            row = lax.broadcasted_iota(jnp.int32, s.shape, 0)
            col = lax.broadcasted_iota(jnp.int32, s.shape, 1)
            s = jnp.where(col <= row, s, NEG)
        m_new = jnp.maximum(m, jnp.max(s, axis=-1, keepdims=True))
        alpha = jnp.exp(m - m_new)
        p = jnp.exp(s - m_new)
        l = l * alpha + jnp.sum(p, axis=-1, keepdims=True)
        acc = acc * alpha + _dot(p.astype(v.dtype), v)
        return m_new, l, acc

    init = (jnp.full((tq, 1), NEG, F32), jnp.zeros((tq, 1), F32), jnp.zeros((tq, dv), F32))
    carry = lax.fori_loop(0, qi, functools.partial(step, masked=False), init)
    _, l, acc = step(qi, carry, True)
    o_ref[...] = acc / l


def _flash(q_all, k_all, v_all, *, batch, seq, heads, dv, scale, tq=512):
    nq = seq // tq
    width = q_all.shape[2]
    return pl.pallas_call(
        functools.partial(_flash_kernel, tq=tq, scale=scale),
        grid=(batch, heads, nq),
        in_specs=[
            pl.BlockSpec((1, tq, width), lambda b, h, i: (h, b * nq + i, 0)),
            pl.BlockSpec((1, seq, width), lambda b, h, i: (h, b, 0)),
            pl.BlockSpec((seq, dv), lambda b, h, i: (b, h)),
        ],
        out_specs=pl.BlockSpec((tq, dv), lambda b, h, i: (b * nq + i, h)),
        out_shape=jax.ShapeDtypeStruct((batch * seq, heads * dv), F32),
        compiler_params=_params(3),
        name="flash",
    )(q_all, k_all, v_all)


def _qlat_kernel(q_ref, w_ref, o_ref, *, nope):
    o_ref[0] = _dot_nt(q_ref[0, :, :nope], w_ref[...].astype(q_ref.dtype))


def _q_lat(q_all, w_uk, *, row0, rows, nope):
    heads, _, width = q_all.shape
    kv_lora = w_uk.shape[0]
    assert row0 % rows == 0
    return pl.pallas_call(
        functools.partial(_qlat_kernel, nope=nope),
        grid=(heads,),
        in_specs=[
            pl.BlockSpec((1, rows, width), lambda h: (h, row0 // rows, 0)),
            pl.BlockSpec((kv_lora, nope), lambda h: (0, h)),
        ],
        out_specs=pl.BlockSpec((1, rows, kv_lora), lambda h: (h, 0, 0)),
        out_shape=jax.ShapeDtypeStruct((heads, rows, kv_lora), F32),
        compiler_params=_params(1),
        name="q_lat",
    )(q_all, w_uk)


def _ouv_kernel(x_ref, w_ref, o_ref):
    o_ref[...] = _dot(x_ref[0], w_ref[...])


def _o_uv(o_lat, w_uv, *, dv):
    heads, rows, kv_lora = o_lat.shape
    return pl.pallas_call(
        _ouv_kernel,
        grid=(heads,),
        in_specs=[
            pl.BlockSpec((1, rows, kv_lora), lambda h: (h, 0, 0)),
            pl.BlockSpec((kv_lora, dv), lambda h: (0, h)),
        ],
        out_specs=pl.BlockSpec((rows, dv), lambda h: (0, h)),
        out_shape=jax.ShapeDtypeStruct((rows, heads * dv), F32),
        compiler_params=_params(1),
        name="o_uv",
    )(o_lat, w_uv)


def _paged_kernel(pt_ref, ql_ref, qr_ref, tok_ref, nl_ref, nr_ref, cl_hbm, cr_hbm, o_ref, bl_ref, br_ref, sem_ref,
                  *, n_groups, group, page, t_new, scale):
    b = pl.program_id(0)
    nb = pl.num_programs(0)

    def copies(seq, g, slot):
        out = []
        for i in range(group):
            pid = pt_ref[seq, g * group + i]
            rows = pl.ds(i * page, page)
            out.append(pltpu.make_async_copy(cl_hbm.at[pid], bl_ref.at[slot, rows], sem_ref.at[0, slot, i]))
            out.append(pltpu.make_async_copy(cr_hbm.at[pid], br_ref.at[slot, rows], sem_ref.at[1, slot, i]))
        return out

    def start(seq, g, slot):
        for c in copies(seq, g, slot):
            c.start()

    @pl.when(b == 0)
    def _():
        start(0, 0, 0)

    ql = ql_ref[0]
    qr = qr_ref[0]
    rows = ql.shape[0]
    kv = ql.shape[1]

    def body(g, carry):
        m, l, acc = carry
        slot = (b * n_groups + g) % 2

        @pl.when(g + 1 < n_groups)
        def _():
            start(b, g + 1, 1 - slot)

        @pl.when(jnp.logical_and(g + 1 == n_groups, b + 1 < nb))
        def _():
            start(b + 1, 0, 1 - slot)

        for c in copies(b, g, slot):
            c.wait()
        c_blk = bl_ref[slot]
        r_blk = br_ref[slot]
        s =(_dot_nt(ql, c_blk) + _dot_nt(qr, r_blk)) * scale
        m_new = jnp.maximum(m, jnp.max(s, axis=-1, keepdims=True))
        alpha = jnp.exp(m - m_new)
        p = jnp.exp(s - m_new)
        l = l * alpha + jnp.sum(p, axis=-1, keepdims=True)
        acc = acc * alpha + _dot(p, c_blk)
        return m_new, l, acc

    init = (jnp.full((rows, 1), NEG, F32), jnp.zeros((rows, 1), F32), jnp.zeros((rows, kv), F32))
    m, l, acc = lax.fori_loop(0, n_groups, body, init)

    t_of_row = tok_ref[...]
    s_new = []
    for j in range(t_new):
        sj = (jnp.sum(ql * nl_ref[0, j:j + 1, :], axis=-1, keepdims=True)
              + jnp.sum(qr * nr_ref[0, j:j + 1, :], axis=-1, keepdims=True)) * scale
        s_new.append(jnp.where(t_of_row >= j, sj, NEG))
    m_new = m
    for sj in s_new:
        m_new = jnp.maximum(m_new, sj)
    alpha = jnp.exp(m - m_new)
    l = l * alpha
    acc = acc * alpha
    for j, sj in enumerate(s_new):
        pj = jnp.exp(sj - m_new)
        l = l + pj
        acc = acc + pj * nl_ref[0, j:j + 1, :]
    o_ref[0] = acc / l


def _paged_attention(page_table, q_lat, q_rope, new_lat, new_kr, cache_lat, cache_kr, *, t_new, scale):
    nb, rows, kv = q_lat.shape
    rope = q_rope.shape[2]
    n_pages = page_table.shape[1]
    page = cache_lat.shape[1]
    group = math.gcd(PAGE_GROUP, n_pages)
    n_groups = n_pages // group
    grid_spec = pltpu.PrefetchScalarGridSpec(
        num_scalar_prefetch=1,
        grid=(nb,),
        in_specs=[
            pl.BlockSpec((1, rows, kv), lambda b, pt: (b, 0, 0)),
            pl.BlockSpec((1, rows, rope), lambda b, pt: (b, 0, 0)),
            pl.BlockSpec((rows, 1), lambda b, pt: (0, 0)),
            pl.BlockSpec((1, t_new, kv), lambda b, pt: (b, 0, 0)),
            pl.BlockSpec((1, t_new, rope), lambda b, pt: (b, 0, 0)),
            pl.BlockSpec(memory_space=pl.ANY),
            pl.BlockSpec(memory_space=pl.ANY),
        ],
        out_specs=pl.BlockSpec((1, rows, kv), lambda b, pt: (b, 0, 0)),
        scratch_shapes=[
            pltpu.VMEM((2, group * page, kv), F32),
            pltpu.VMEM((2, group * page, rope), F32),
            pltpu.SemaphoreType.DMA((2, 2, group)),
        ],
    )
    tok = (jnp.arange(rows, dtype=jnp.int32) % t_new).reshape(rows, 1)
    return pl.pallas_call(
        functools.partial(_paged_kernel, n_groups=n_groups, group=group, page=page, t_new=t_new, scale=scale),
        grid_spec=grid_spec,
        out_shape=jax.ShapeDtypeStruct((nb, rows, kv), F32),
        compiler_params=_params(1),
        name="paged_attn",
    )(page_table, q_lat, q_rope, tok, new_lat, new_kr, cache_lat, cache_kr)


def _split3(x):
    hi = x.astype(BF16).astype(F32)
    r1 = x - hi
    mid = r1.astype(BF16).astype(F32)
    lo = (r1 - mid).astype(BF16).astype(F32)
    return hi, mid, lo


def _pad_rows(x, rows):
    if x.shape[0] == rows:
        return x
    return jnp.concatenate([x, jnp.zeros((rows - x.shape[0], x.shape[1]), x.dtype)], axis=0)


def _hgrn_kernel(*refs, heads, dk, chunk, valid, n_chunks, has_s0):
    it = iter(refs)
    hq_ref, hf_ref, hi_ref, hg_ref, lb_ref, gn_ref = (next(it) for _ in range(6))
    s0_ref = next(it) if has_s0 else None
    o_ref, so_ref, st_ref = next(it), next(it), next(it)

    for h in range(heads):
        st_ref[h] = s0_ref[0, h].T if has_s0 else jnp.zeros((dk, dk), F32)

    keys = max(chunk, LANES)
    row = lax.broadcasted_iota(jnp.int32, (chunk, keys), 0)
    col = lax.broadcasted_iota(jnp.int32, (chunk, keys), 1)
    causal = col <= row
    tri = jnp.where(causal, 1.0, 0.0)
    live = lax.broadcasted_iota(jnp.int32, (chunk, dk), 0) < valid

    def body(c, carry):
        r0 = pl.multiple_of(c * chunk, chunk)
        for h in range(heads):
            cols = slice(h * dk, (h + 1) * dk)
            lb = lb_ref[:, cols]
            q = jax.nn.silu(hq_ref[pl.ds(r0, chunk), cols]) * dk ** -0.5
            f = lb + (1.0 - lb) * jax.nn.sigmoid(hf_ref[pl.ds(r0, chunk), cols])
            logf = jnp.log(f)
            k = 1.0 - f
            v = hi_ref[pl.ds(r0, chunk), cols]
            if valid < chunk:
                logf = jnp.where(live, logf, 0.0)
                k = jnp.where(live, k, 0.0)
                v = jnp.where(live, v, 0.0)
            p0, p1, p2 = _split3(_pad_rows(logf, keys))
            bsum = _dot(tri, p0) + _dot(tri, p1) + _dot(tri, p2)
            b_last = bsum[chunk - 1:chunk, :]
            q_dec = q * jnp.exp(bsum)
            k_inv = k * jnp.exp(-bsum)
            v_pad = _pad_rows(v, keys)
            att = jnp.where(causal, _dot_nt(q_dec, _pad_rows(k_inv, keys)), 0.0)
            st = st_ref[h]
            o = _dot(att, v_pad) + _dot_nt(q_dec, st)
            k_end = k * jnp.exp(b_last - bsum)
            st_ref[h] = st * jnp.exp(b_last) + _dot_tn(v_pad, _pad_rows(k_end, keys))
            o = _rms(o, gn_ref[...]) * jax.nn.silu(hg_ref[pl.ds(r0, chunk), cols])
            o_ref[pl.ds(r0, chunk), cols] = o
        return carry

    lax.fori_loop(0, n_chunks, body, 0)
    for h in range(heads):
        so_ref[0, h] = st_ref[h].T


def _hgrn(zq, zf, zi, zg, lb, hg_norm, s0, *, offs, n_seq, rows, heads_total, heads, dk, chunk, valid):
    hw = heads * dk
    n_hb = heads_total // heads
    n_chunks = rows // chunk

    def zspec(off):
        return pl.BlockSpec((rows, hw), lambda s, j: (s, off // hw + j))

    in_specs = [zspec(o) for o in offs] + [
        pl.BlockSpec((1, hw), lambda s, j: (0, j)),
        pl.BlockSpec((1, dk), lambda s, j: (0, 0)),
    ]
    args = [zq, zf, zi, zg, lb.reshape(1, -1), hg_norm.reshape(1, dk)]
    if s0 is not None:
        in_specs.append(pl.BlockSpec((1, heads, dk, dk), lambda s, j: (s, j, 0, 0)))
        args.append(s0)
    return pl.pallas_call(
        functools.partial(_hgrn_kernel, heads=heads, dk=dk, chunk=chunk, valid=valid, n_chunks=n_chunks,
                          has_s0=s0 is not None),
        grid=(n_seq, n_hb),
        in_specs=in_specs,
        out_specs=[
            pl.BlockSpec((rows, hw), lambda s, j: (s, j)),
            pl.BlockSpec((1, heads, dk, dk), lambda s, j: (s, j, 0, 0)),
        ],
        out_shape=[
            jax.ShapeDtypeStruct((n_seq * rows, heads_total * dk), F32),
            jax.ShapeDtypeStruct((n_seq, heads_total, dk, dk), F32),
        ],
        scratch_shapes=[pltpu.VMEM((heads, dk, dk), F32)],
        compiler_params=_params(2),
        name="hgrn",
    )(*args)


def _gated_kernel(a_ref, b_ref, ga_ref, gb_ref, wa_ref, wb_ref, o_ref):
    o_ref[...] = (jax.nn.sigmoid(ga_ref[...]) * _dot(a_ref[...], wa_ref[...])
                  + jax.nn.sigmoid(gb_ref[...]) * _dot(b_ref[...], wb_ref[...]))


def _gated_merge(a, b, z, w_a, w_b, *, off_ga, off_gb, tm=512, tn=512):
    m, ka = a.shape
    kb = b.shape[1]
    n = w_a.shape[1]
    tn = _tile(n, tn)
    return pl.pallas_call(
        _gated_kernel,
        grid=(m // tm, n // tn),
        in_specs=[
            pl.BlockSpec((tm, ka), lambda i, j: (i, 0)),
            pl.BlockSpec((tm, kb), lambda i, j: (i, 0)),
            pl.BlockSpec((tm, tn), lambda i, j: (i, off_ga // tn + j)),
            pl.BlockSpec((tm, tn), lambda i, j: (i, off_gb // tn + j)),
            pl.BlockSpec((ka, tn), lambda i, j: (0, j)),
            pl.BlockSpec((kb, tn), lambda i, j: (0, j)),
        ],
        out_specs=pl.BlockSpec((tm, tn), lambda i, j: (i, j)),
        out_shape=jax.ShapeDtypeStruct((m, n), F32),
        compiler_params=_params(2),
        name="gated_merge",
    )(a, b, z, z, w_a, w_b)


def _cross_kernel(q_ref, k_ref, v_ref, o_ref, *, heads, dh):
    for h in range(heads):
        cols = slice(h * dh, (h + 1) * dh)
        s = _dot_nt(q_ref[0, :, cols], k_ref[0, :, cols]) * dh ** -0.5
        p = jnp.exp(s - jnp.max(s, axis=-1, keepdims=True))
        p = p / jnp.sum(p, axis=-1, keepdims=True)
        o_ref[0, :, cols] = _dot(p, v_ref[0, :, cols])


def _cross_attention(q, k, v, *, heads, groups_per_mem):
    n, r, w = q.shape
    mem = k.shape[1]
    return pl.pallas_call(
        functools.partial(_cross_kernel, heads=heads, dh=w // heads),
        grid=(n,),
        in_specs=[
            pl.BlockSpec((1, r, w), lambda i: (i, 0, 0)),
            pl.BlockSpec((1, mem, w), lambda i: (i // groups_per_mem, 0, 0)),
            pl.BlockSpec((1, mem, w), lambda i: (i // groups_per_mem, 0, 0)),
        ],
        out_specs=pl.BlockSpec((1, r, w), lambda i: (i, 0, 0)),
        out_shape=jax.ShapeDtypeStruct((n, r, w), F32),
        compiler_params=_params(1),
        name="cross_attn",
    )(q, k, v)


def _moe_kernel(be_ref, na_ref, x_ref, gate_ref, g_ref, wgu_ref, wdn_ref, o_ref, *, d_expert):
    @pl.when(pl.program_id(0) < na_ref[0])
    def _():
        h = _rms(x_ref[...], g_ref[...])
        gu = _dot(h, wgu_ref[0])
        act = jax.nn.silu(gu[:, :d_expert]) * gu[:, d_expert:]
        o_ref[...] = _dot(act, wdn_ref[0]) * gate_ref[...]

    @pl.when(pl.program_id(0) >= na_ref[0])
    def _():
        o_ref[...] = jnp.zeros_like(o_ref)


def _moe_ffn(block_expert, n_active, x_sorted, row_gate, norm_ffn, w_gu, w_dn, *, block):
    n_rows, d = x_sorted.shape
    n_blocks = n_rows // block
    d_expert = w_dn.shape[1]
    grid_spec = pltpu.PrefetchScalarGridSpec(
        num_scalar_prefetch=2,
        grid=(n_blocks,),
        in_specs=[
            pl.BlockSpec((block, d), lambda i, be, na: (i, 0)),
            pl.BlockSpec((block, 1), lambda i, be, na: (i, 0)),
            pl.BlockSpec((1, d), lambda i, be, na: (0, 0)),
            pl.BlockSpec((1, d, 2 * d_expert), lambda i, be, na: (be[i], 0, 0)),
            pl.BlockSpec((1, d_expert, d), lambda i, be, na: (be[i], 0, 0)),
        ],
        out_specs=pl.BlockSpec((block, d), lambda i, be, na: (i, 0)),
    )
    return pl.pallas_call(
        functools.partial(_moe_kernel, d_expert=d_expert),
        grid_spec=grid_spec,
        out_shape=jax.ShapeDtypeStruct((n_rows, d), F32),
        compiler_params=_params(1),
        name="moe_ffn",
    )(block_expert, n_active, x_sorted, row_gate, norm_ffn.reshape(1, d), w_gu, w_dn)


def _final_kernel(x_ref, ya_ref, yb_ref, g_ref, o_ref):
    o_ref[...] = _rms(x_ref[...] + (ya_ref[...] + yb_ref[...]), g_ref[...])


def _final_norm(x, ya, yb, g, *, tm=512):
    m, d = x.shape
    spec = pl.BlockSpec((tm, d), lambda i: (i, 0))
    return pl.pallas_call(
        _final_kernel,
        grid=(m // tm,),
        in_specs=[spec, spec, spec, pl.BlockSpec((1, d), lambda i: (0, 0))],
        out_specs=spec,
        out_shape=jax.ShapeDtypeStruct((m, d), F32),
        compiler_params=_params(1),
        name="final_norm",
    )(x, ya, yb, g.reshape(1, d))


def _route(logits, b_rg, b_re, n_groups, n_experts, moe_block):
    n = logits.shape[0]
    epg = n_experts // n_groups
    p_group = jax.nn.softmax(logits[:, :n_groups] + b_rg, axis=-1)
    g_w, g_idx = lax.top_k(p_group, 1)
    e_logits = (logits[:, n_groups:n_groups + n_experts] + b_re).reshape(n, n_groups, epg)
    in_group = jnp.take_along_axis(e_logits, g_idx[:, :, None], axis=1)[:, 0]
    top_logit, top_local = lax.top_k(in_group, TOP_K)
    gate = g_w * jax.nn.softmax(top_logit, axis=-1)
    expert_idx = g_idx * epg + top_local

    a = n * TOP_K
    flat_e = expert_idx.reshape(a).astype(jnp.int32)
    onehot = (flat_e[:, None] == jnp.arange(n_experts, dtype=jnp.int32)[None, :]).astype(jnp.int32)
    rank = jnp.sum((jnp.cumsum(onehot, axis=0) - onehot) * onehot, axis=1)
    counts = jnp.sum(onehot, axis=0)
    padded = (counts + moe_block - 1) // moe_block * moe_block
    pad_end = jnp.cumsum(padded)
    pad_start = pad_end - padded
    dest = pad_start[flat_e] + rank
    n_blocks = (a + n_experts * (moe_block - 1) + moe_block - 1) // moe_block
    n_rows = n_blocks * moe_block
    row_token = jnp.zeros((n_rows,), jnp.int32).at[dest].set(jnp.arange(a, dtype=jnp.int32) // TOP_K)
    row_gate = jnp.zeros((n_rows,), F32).at[dest].set(gate.reshape(a))
    n_active = (pad_end[-1] // moe_block).astype(jnp.int32)
    blk = jnp.minimum(jnp.arange(n_blocks, dtype=jnp.int32), n_active - 1) * moe_block
    block_expert = jnp.minimum(jnp.searchsorted(pad_end, blk, side='right'), n_experts - 1).astype(jnp.int32)
    return row_token, row_gate, dest.reshape(n, TOP_K), block_expert, n_active.reshape(1)


def kernel(x_prompt, mem_prompt, x_sample, cache_kv_latent, cache_k_rope, page_table, state_hgrn, cache_mem_k, cache_mem_v, norm_mix, w_in, q_norm, w_uq, kv_norm, w_uk, w_uv, lb_logits, hg_norm, w_out_mla, w_out_hgrn, w_out, norm_cross, norm_mem, w_cq, w_ckv, w_co, norm_ffn, w_router_group, b_router_group, w_router_expert, b_router_expert, w_expert_gate_up, w_expert_down, norm_final):
    batch, seq, d = x_prompt.shape
    dec_batch, dec_seq, _ = x_sample.shape
    depth = w_in.shape[0]
    assert depth == 1
    page = cache_kv_latent.shape[2]
    kv_lora = cache_kv_latent.shape[3]
    rope = cache_k_rope.shape[3]
    half = rope // 2
    past_len = page_table.shape[1] * page
    q_lora = q_norm.shape[1]
    hg_heads, hg_dk = state_hgrn.shape[2], state_hgrn.shape[3]
    hgw = hg_heads * hg_dk
    mla_heads = (w_uq.shape[2] - w_uk.shape[2]) // rope
    nope = w_uk.shape[2] // mla_heads
    v_head = w_uv.shape[2] // mla_heads
    mem_tokens, x_heads, x_dh = cache_mem_k.shape[2], cache_mem_k.shape[3], cache_mem_k.shape[4]
    xw = x_heads * x_dh
    n_groups = w_router_group.shape[2]
    n_experts = w_router_expert.shape[2]
    mp = batch * seq
    ms = dec_batch * dec_seq
    m_all = mp + ms
    mla_scale = (nope + rope) ** -0.5
    assert nope == LANES and 2 * half <= LANES and hg_dk == LANES

    n_front = q_lora + kv_lora + rope
    off_hq, off_hf, off_hi, off_hg = 0, hgw, 2 * hgw, 3 * hgw
    off_ga, off_gb = 4 * hgw, 4 * hgw + d
    off_cq = 4 * hgw + 2 * d
    off_ckv = off_cq + q_lora
    off_kr = off_ckv + kv_lora
    w_in_p = jnp.concatenate([w_in[0][:, n_front:], w_in[0][:, :n_front], jnp.zeros((d, LANES - rope), F32)], axis=1)

    x0 = jnp.concatenate([x_prompt.reshape(mp, d), x_sample.reshape(ms, d)], axis=0)
    pos = jnp.concatenate([jnp.tile(jnp.arange(seq), batch), jnp.tile(past_len + jnp.arange(dec_seq), dec_batch)])
    cos_t, sin_t = _rope_tables(pos, half)

    z = _matmul(x0, w_in_p, gain=norm_mix[0], tn_cap=1024, name="mm_in")
    lat, kr_pad = _lat_kr(z, kv_norm[0], cos_t, sin_t, off_ckv=off_ckv, off_kr=off_kr, kv_lora=kv_lora, half=half)
    w_uq_p = jnp.pad(w_uq[0].reshape(q_lora, mla_heads, nope + rope).transpose(1, 0, 2),
                     ((0, 0), (0, 0), (0, LANES - rope)))
    q_all = _q_proj(z, w_uq_p, q_norm[0], cos_t, sin_t, off_cq=off_cq, q_lora=q_lora, nope=nope, half=half)

    k_all = _k_proj(lat, w_uk[0], kr_pad, rows=mp, heads=mla_heads, nope=nope)
    v_all = _matmul(lat[:mp], w_uv[0], out_dtype=BF16, tn_cap=1024, name="v_proj")
    a_p = _flash(q_all, k_all, v_all, batch=batch, seq=seq, heads=mla_heads, dv=v_head, scale=mla_scale)

    q_lat = _q_lat(q_all, w_uk[0], row0=mp, rows=ms, nope=nope)
    rows_s = mla_heads * dec_seq
    q_lat = q_lat.reshape(mla_heads, dec_batch, dec_seq, kv_lora).transpose(1, 0, 2, 3).reshape(dec_batch, rows_s, kv_lora)
    q_rope_s = q_all[:, mp:, nope:nope + rope].astype(F32)
    q_rope_s = q_rope_s.reshape(mla_heads, dec_batch, dec_seq, rope).transpose(1, 0, 2, 3).reshape(dec_batch, rows_s, rope)
    lat_s = lat[mp:].reshape(dec_batch, dec_seq, kv_lora)
    kr_s = kr_pad[mp:, :rope].reshape(dec_batch, dec_seq, rope)
    o_lat = _paged_attention(page_table, q_lat, q_rope_s, lat_s, kr_s,
                             cache_kv_latent.reshape(-1, page, kv_lora), cache_k_rope.reshape(-1, page, rope),
                             t_new=dec_seq, scale=mla_scale)
    o_lat = o_lat.reshape(dec_batch, mla_heads, dec_seq, kv_lora).transpose(1, 0, 2, 3).reshape(mla_heads, ms, kv_lora)
    a_s = _o_uv(o_lat, w_uv[0], dv=v_head)

    lower_bounds = jnp.cumsum(jax.nn.softmax(lb_logits.astype(F32), axis=0), axis=0)
    lb = lower_bounds[0]
    chunk_p = math.gcd(HG_CHUNK, seq)
    b_p, s_p = _hgrn(z, z, z, z, lb, hg_norm[0], None, offs=(off_hq, off_hf, off_hi, off_hg), n_seq=batch, rows=seq,
                     heads_total=hg_heads, heads=1, dk=hg_dk, chunk=chunk_p, valid=chunk_p)
    rows_pad = -(-dec_seq // SUBLANES) * SUBLANES
    z_s = jnp.pad(z[mp:, :4 * hgw].reshape(dec_batch, dec_seq, 4 * hgw), ((0, 0), (0, rows_pad - dec_seq), (0, 0)))
    z_s = z_s.reshape(dec_batch * rows_pad, 4 * hgw)
    b_s, s_s = _hgrn(z_s, z_s, z_s, z_s, lb, hg_norm[0], state_hgrn[0], offs=(off_hq, off_hf, off_hi, off_hg),
                     n_seq=dec_batch, rows=rows_pad, heads_total=hg_heads, heads=hg_heads, dk=hg_dk,
                     chunk=rows_pad, valid=dec_seq)
    b_s = b_s.reshape(dec_batch, rows_pad, hgw)[:, :dec_seq].reshape(ms, hgw)

    a_all = jnp.concatenate([a_p, a_s], axis=0)
    b_all = jnp.concatenate([b_p, b_s], axis=0)
    merged = _gated_merge(a_all, b_all, z, w_out_mla[0], w_out_hgrn[0], off_ga=off_ga, off_gb=off_gb)
    x1 = _matmul(merged, w_out[0], res=x0, name="mm_out")

    kv_mem = _matmul(mem_prompt.reshape(batch * mem_tokens, d), w_ckv[0], gain=norm_mem[0], tm=batch * mem_tokens,
                     name="mem_kv")
    mk_p, mv_p = kv_mem[:, :xw].reshape(batch, mem_tokens, xw), kv_mem[:, xw:].reshape(batch, mem_tokens, xw)
    qx = _matmul(x1, w_cq[0], gain=norm_cross[0], name="cross_q")
    tq_x = 512
    ox_p = _cross_attention(qx[:mp].reshape(mp // tq_x, tq_x, xw), mk_p, mv_p, heads=x_heads,
                            groups_per_mem=seq // tq_x)
    qx_s = jnp.pad(qx[mp:].reshape(dec_batch, dec_seq, xw), ((0, 0), (0, rows_pad - dec_seq), (0, 0)))
    ox_s = _cross_attention(qx_s, cache_mem_k[0].reshape(dec_batch, mem_tokens, xw),
                            cache_mem_v[0].reshape(dec_batch, mem_tokens, xw), heads=x_heads, groups_per_mem=1)
    ox = jnp.concatenate([ox_p.reshape(mp, xw), ox_s[:, :dec_seq].reshape(ms, xw)], axis=0)
    x2 = _matmul(ox, w_co[0], res=x1, name="cross_out")

    n_r = n_groups + n_experts
    w_r = jnp.concatenate([w_router_group[0], w_router_expert[0], jnp.zeros((d, -n_r % LANES), F32)], axis=1)
    logits = _matmul(x2, w_r, gain=norm_ffn[0], name="router")
    moe_block = 128

    def route(lg):
        return _route(lg, b_router_group[0], b_router_expert[0], n_groups, n_experts, moe_block)

    row_token, row_gate, dest, block_expert, n_active = route(logits)
    x_sorted = x2[row_token]
    y_sorted = _moe_ffn(block_expert, n_active, x_sorted, row_gate[:, None], norm_ffn[0],
                        w_expert_gate_up[0], w_expert_down[0], block=moe_block)
    y = _final_norm(x2, y_sorted[dest[:, 0]], y_sorted[dest[:, 1]], norm_final)

    y_prompt = y[:mp].reshape(batch, seq, d)
    y_sample = y[mp:].reshape(dec_batch, dec_seq, d)
    n_prompt_pages = mp // page
    kv_latent_prompt = lat[:mp].reshape(depth, n_prompt_pages, page, kv_lora)
    k_rope_prompt = kr_pad[:mp, :rope].reshape(depth, n_prompt_pages, page, rope)
    state_hgrn_prompt = s_p[None]
    mem_k_prompt = mk_p.reshape(depth, batch, mem_tokens, x_heads, x_dh)
    mem_v_prompt = mv_p.reshape(depth, batch, mem_tokens, x_heads, x_dh)
    kv_latent_sample = lat_s[None]
    k_rope_sample = kr_s[None]
    state_hgrn_sample = s_s[None]
    return (y_prompt, y_sample, kv_latent_prompt, k_rope_prompt, state_hgrn_prompt, mem_k_prompt, mem_v_prompt,
            kv_latent_sample, k_rope_sample, state_hgrn_sample)
```

```python
import functools
import math

import jax
import jax.numpy as jnp
from jax import lax
from jax.experimental import pallas as pl
from jax.experimental.pallas import tpu as pltpu

F32 = jnp.float32
BF16 = jnp.bfloat16

EPS = 1e-6
NEG = -1e30
ROPE_THETA = 10000.0
TOP_K = 2
HG_CHUNK = 64
LANES = 128
SUBLANES = 8
VMEM_LIMIT = 56 * 1024 * 1024
PAGE_GROUP = 8
PAGE_SLOTS = 4


def _params(n_axes):
    return pltpu.CompilerParams(dimension_semantics=("arbitrary",) * n_axes, vmem_limit_bytes=VMEM_LIMIT)


def _tile(n, cap, unit=LANES):
    if n <= cap:
        return n
    t = cap - cap % unit
    while t >= unit:
        if n % t == 0:
            return t
        t -= unit
    raise ValueError(f"no tile for {n} under {cap}")


def _rms(x, g):
    return x * lax.rsqrt(jnp.mean(x * x, axis=-1, keepdims=True) + EPS) * g


def _dot(a, b):
    return jnp.dot(a, b, preferred_element_type=F32)


def _dot_nt(a, b):
    return lax.dot_general(a, b, (((1,), (1,)), ((), ())), preferred_element_type=F32)


def _dot_tn(a, b):
    return lax.dot_general(a, b, (((0,), (0,)), ((), ())), preferred_element_type=F32)


def _mm_kernel(*refs, has_gain, has_res, w_is_nk):
    it = iter(refs)
    x_ref, w_ref = next(it), next(it)
    g_ref = next(it) if has_gain else None
    r_ref = next(it) if has_res else None
    o_ref = next(it)
    if has_gain:
        xs_ref = next(it)

        @pl.when(pl.program_id(1) == 0)
        def _():
            xs_ref[...] = _rms(x_ref[...].astype(F32), g_ref[...]).astype(xs_ref.dtype)

        x = xs_ref[...]
    else:
        x = x_ref[...]
    acc = _dot_nt(x, w_ref[...]) if w_is_nk else _dot(x, w_ref[...])
    if has_res:
        acc = r_ref[...] + acc
    o_ref[...] = acc.astype(o_ref.dtype)


def _matmul(x, w, *, w_is_nk=False, gain=None, res=None, x_col=0, tm=512, tn_cap=512, out_dtype=F32, name="mm"):
    m = x.shape[0]
    n, k = w.shape if w_is_nk else w.shape[::-1]
    assert x_col % k == 0 and m % tm == 0
    tn = _tile(n, tn_cap)
    xb = x_col // k
    w_spec = pl.BlockSpec((tn, k), lambda i, j: (j, 0)) if w_is_nk else pl.BlockSpec((k, tn), lambda i, j: (0, j))
    in_specs = [pl.BlockSpec((tm, k), lambda i, j: (i, xb)), w_spec]
    args = [x, w]
    scratch = []
    if gain is not None:
        in_specs.append(pl.BlockSpec((1, k), lambda i, j: (0, 0)))
        args.append(gain.reshape(1, k).astype(F32))
        scratch.append(pltpu.VMEM((tm, k), w.dtype))
    if res is not None:
        in_specs.append(pl.BlockSpec((tm, tn), lambda i, j: (i, j)))
        args.append(res)
    return pl.pallas_call(
        functools.partial(_mm_kernel, has_gain=gain is not None, has_res=res is not None, w_is_nk=w_is_nk),
        grid=(m // tm, n // tn),
        in_specs=in_specs,
        out_specs=pl.BlockSpec((tm, tn), lambda i, j: (i, j)),
        out_shape=jax.ShapeDtypeStruct((m, n), out_dtype),
        scratch_shapes=scratch,
        compiler_params=_params(2),
        name=name,
    )(*args)


def _rope_tile(x, cos_t, sin_t, half):
    lane = lax.broadcasted_iota(jnp.int32, x.shape, 1)
    swapped = jnp.where(lane < half, pltpu.roll(x, LANES - half, 1), pltpu.roll(x, half, 1))
    return x * cos_t + swapped * sin_t


def _rope_tables(pos, half):
    inv = ROPE_THETA ** (-jnp.arange(half, dtype=F32) / half)
    ang = pos.astype(F32)[:, None] * inv[None, :]
    cos, sin = jnp.cos(ang), jnp.sin(ang)
    zeros = jnp.zeros((pos.shape[0], LANES - 2 * half), F32)
    return jnp.concatenate([cos, cos, zeros], axis=1), jnp.concatenate([-sin, sin, zeros], axis=1)


def _latkr_kernel(ckv_ref, kr_ref, g_ref, cos_ref, sin_ref, lat_ref, kro_ref, *, half):
    lat_ref[...] = _rms(ckv_ref[...], g_ref[...])
    kro_ref[...] = _rope_tile(kr_ref[...], cos_ref[...], sin_ref[...], half)


def _lat_kr(z, kv_norm, cos_t, sin_t, *, off_ckv, off_kr, kv_lora, half, tm=512):
    m = z.shape[0]
    return pl.pallas_call(
        functools.partial(_latkr_kernel, half=half),
        grid=(m // tm,),
        in_specs=[
            pl.BlockSpec((tm, kv_lora), lambda i: (i, off_ckv // kv_lora)),
            pl.BlockSpec((tm, LANES), lambda i: (i, off_kr // LANES)),
            pl.BlockSpec((1, kv_lora), lambda i: (0, 0)),
            pl.BlockSpec((tm, LANES), lambda i: (i, 0)),
            pl.BlockSpec((tm, LANES), lambda i: (i, 0)),
        ],
        out_specs=[pl.BlockSpec((tm, kv_lora), lambda i: (i, 0)), pl.BlockSpec((tm, LANES), lambda i: (i, 0))],
        out_shape=[jax.ShapeDtypeStruct((m, kv_lora), F32), jax.ShapeDtypeStruct((m, LANES), F32)],
        compiler_params=_params(1),
        name="lat_kr",
    )(z, z, kv_norm.reshape(1, kv_lora), cos_t, sin_t)


def _qproj_kernel(cq_ref, w_ref, g_ref, cos_ref, sin_ref, o_ref, xs_ref, *, nope, half):
    @pl.when(pl.program_id(1) == 0)
    def _():
        xs_ref[...] = _rms(cq_ref[...], g_ref[...])

    q = _dot(xs_ref[...], w_ref[0])
    o_ref[0, :, :nope] = q[:, :nope].astype(o_ref.dtype)
    o_ref[0, :, nope:] = _rope_tile(q[:, nope:], cos_ref[...], sin_ref[...], half).astype(o_ref.dtype)


def _q_proj(z, w_uq_p, q_norm, cos_t, sin_t, *, off_cq, q_lora, nope, half, tm=512):
    m = z.shape[0]
    heads, _, width = w_uq_p.shape
    return pl.pallas_call(
        functools.partial(_qproj_kernel, nope=nope, half=half),
        grid=(m // tm, heads),
        in_specs=[
            pl.BlockSpec((tm, q_lora), lambda i, h: (i, off_cq // q_lora)),
            pl.BlockSpec((1, q_lora, width), lambda i, h: (h, 0, 0)),
            pl.BlockSpec((1, q_lora), lambda i, h: (0, 0)),
            pl.BlockSpec((tm, LANES), lambda i, h: (i, 0)),
            pl.BlockSpec((tm, LANES), lambda i, h: (i, 0)),
        ],
        out_specs=pl.BlockSpec((1, tm, width), lambda i, h: (h, i, 0)),
        out_shape=jax.ShapeDtypeStruct((heads, m, width), BF16),
        scratch_shapes=[pltpu.VMEM((tm, q_lora), F32)],
        compiler_params=_params(2),
        name="q_proj",
    )(z, w_uq_p, q_norm.reshape(1, q_lora), cos_t, sin_t)


def _kproj_kernel(lat_ref, w_ref, kr_ref, o_ref, *, nope):
    o_ref[0, :, :nope] = _dot(lat_ref[...], w_ref[...]).astype(o_ref.dtype)
    o_ref[0, :, nope:] = kr_ref[...].astype(o_ref.dtype)


def _k_proj(lat, w_uk, kr_pad, *, rows, heads, nope, tm=512):
    kv_lora = lat.shape[1]
    return pl.pallas_call(
        functools.partial(_kproj_kernel, nope=nope),
        grid=(rows // tm, heads),
        in_specs=[
            pl.BlockSpec((tm, kv_lora), lambda i, h: (i, 0)),
            pl.BlockSpec((kv_lora, nope), lambda i, h: (0, h)),
            pl.BlockSpec((tm, LANES), lambda i, h: (i, 0)),
        ],
        out_specs=pl.BlockSpec((1, tm, nope + LANES), lambda i, h: (h, i, 0)),
        out_shape=jax.ShapeDtypeStruct((heads, rows, nope + LANES), BF16),
        compiler_params=_params(2),
        name="k_proj",
    )(lat, w_uk, kr_pad)


def _flash_kernel(q_ref, k_ref, v_ref, o_ref, *, tq, scale):
    qi = pl.program_id(2)
    q = q_ref[0]
    dv = v_ref.shape[1]

    def step(kb, carry, masked):
        m, l, acc = carry
        start = pl.multiple_of(kb * tq, tq)
        k = k_ref[0, pl.ds(start, tq), :]
        v = v_ref[pl.ds(start, tq), :]
        s = _dot_nt(q, k) * scale
        if masked:
            row = lax.broadcasted_iota(jnp.int32, s.shape, 0)
            col = lax.broadcasted_iota(jnp.int32, s.shape, 1)
            s = jnp.where(col <= row, s, NEG)
        m_new = jnp.maximum(m, jnp.max(s, axis=-1, keepdims=True))
        alpha = jnp.exp(m - m_new)
        p = jnp.exp(s - m_new)
        l = l * alpha + jnp.sum(p, axis=-1, keepdims=True)
        acc = acc * alpha + _dot(p.astype(v.dtype), v)
        return m_new, l, acc

    init = (jnp.full((tq, 1), NEG, F32), jnp.zeros((tq, 1), F32), jnp.zeros((tq, dv), F32))
    carry = lax.fori_loop(0, qi, functools.partial(step, masked=False), init)
    _, l, acc = step(qi, carry, True)
    o_ref[...] = acc / l


def _flash(q_all, k_all, v_all, *, batch, seq, heads, dv, scale, tq=512):
    nq = seq // tq
    width = q_all.shape[2]
    return pl.pallas_call(
        functools.partial(_flash_kernel, tq=tq, scale=scale),
        grid=(batch, heads, nq),
        in_specs=[
            pl.BlockSpec((1, tq, width), lambda b, h, i: (h, b * nq + i, 0)),
            pl.BlockSpec((1, seq, width), lambda b, h, i: (h, b, 0)),
            pl.BlockSpec((seq, dv), lambda b, h, i: (b, h)),
        ],
        out_specs=pl.BlockSpec((tq, dv), lambda b, h, i: (b * nq + i, h)),
        out_shape=jax.ShapeDtypeStruct((batch * seq, heads * dv), F32),
        compiler_params=_params(3),
        name="flash",
    )(q_all, k_all, v_all)


def _qlat_kernel(q_ref, w_ref, o_ref, *, nope):
    o_ref[0] = _dot_nt(q_ref[0, :, :nope], w_ref[...].astype(q_ref.dtype))


def _q_lat(q_all, w_uk, *, row0, rows, nope):
    heads, _, width = q_all.shape
    kv_lora = w_uk.shape[0]
    assert row0 % rows == 0
    return pl.pallas_call(
        functools.partial(_qlat_kernel, nope=nope),
        grid=(heads,),
        in_specs=[
            pl.BlockSpec((1, rows, width), lambda h: (h, row0 // rows, 0)),
            pl.BlockSpec((kv_lora, nope), lambda h: (0, h)),
        ],
        out_specs=pl.BlockSpec((1, rows, kv_lora), lambda h: (h, 0, 0)),
        out_shape=jax.ShapeDtypeStruct((heads, rows, kv_lora), F32),
        compiler_params=_params(1),
        name="q_lat",
    )(q_all, w_uk)


def _ouv_kernel(x_ref, w_ref, o_ref):
    o_ref[...] = _dot(x_ref[0], w_ref[...])


def _o_uv(o_lat, w_uv, *, dv):
    heads, rows, kv_lora = o_lat.shape
    return pl.pallas_call(
        _ouv_kernel,
        grid=(heads,),
        in_specs=[
            pl.BlockSpec((1, rows, kv_lora), lambda h: (h, 0, 0)),
            pl.BlockSpec((kv_lora, dv), lambda h: (0, h)),
        ],
        out_specs=pl.BlockSpec((rows, dv), lambda h: (0, h)),
        out_shape=jax.ShapeDtypeStruct((rows, heads * dv), F32),
        compiler_params=_params(1),
        name="o_uv",
    )(o_lat, w_uv)


def _paged_kernel(pt_ref, ql_ref, qr_ref, tok_ref, nl_ref, nr_ref, cl_hbm, cr_hbm, o_ref, bl_ref, br_ref, sem_ref,
                  *, n_seq, n_groups, group, page, slots, t_new, scale):
    b = pl.program_id(0)
    total = n_seq * n_groups
    ahead = slots - 1

    def copies(t):
        seq = lax.div(t, n_groups)
        g = lax.rem(t, n_groups)
        slot = lax.rem(t, slots)
        out = []
        for i in range(group):
            pid = pt_ref[seq, g * group + i]
            keys = pl.ds(i * page, page)
            out.append(pltpu.make_async_copy(cl_hbm.at[pid], bl_ref.at[slot, keys], sem_ref.at[0, slot, i]))
            out.append(pltpu.make_async_copy(cr_hbm.at[pid], br_ref.at[slot, :, keys], sem_ref.at[1, slot, i]))
        return out

    @pl.when(b == 0)
    def _():
        for t0 in range(min(ahead, total)):
            for c in copies(jnp.int32(t0)):
                c.start()

    ql = ql_ref[0]
    qr = qr_ref[0]
    ql_lo = ql.astype(BF16)
    qr_lo = qr.astype(BF16)
    rows = ql.shape[0]
    kv = ql.shape[1]

    def body(g, carry):
        m, l, acc = carry
        t = b * n_groups + g

        @pl.when(t + ahead < total)
        def _():
            for c in copies(t + ahead):
                c.start()

        for c in copies(t):
            c.wait()
        slot = lax.rem(t, slots)
        c_blk = bl_ref[slot].astype(BF16)
        r_blk = br_ref[slot].astype(BF16)
        s = (_dot_nt(ql_lo, c_blk) + _dot(qr_lo, r_blk)) * scale
        m_new = jnp.maximum(m, jnp.max(s, axis=-1, keepdims=True))
        alpha = jnp.exp(m - m_new)
        p = jnp.exp(s - m_new)
        l = l * alpha + jnp.sum(p, axis=-1, keepdims=True)
        acc = acc * alpha + _dot(p.astype(BF16), c_blk)
        return m_new, l, acc

    init = (jnp.full((rows, 1), NEG, F32), jnp.zeros((rows, 1), F32), jnp.zeros((rows, kv), F32))
    m, l, acc = lax.fori_loop(0, n_groups, body, init)

    t_of_row = tok_ref[...]
    s_new = []
    for j in range(t_new):
        sj = (jnp.sum(ql * nl_ref[0, j:j + 1, :], axis=-1, keepdims=True)
              + jnp.sum(qr * nr_ref[0, j:j + 1, :], axis=-1, keepdims=True)) * scale
        s_new.append(jnp.where(t_of_row >= j, sj, NEG))
    m_new = m
    for sj in s_new:
        m_new = jnp.maximum(m_new, sj)
    alpha = jnp.exp(m - m_new)
    l = l * alpha
    acc = acc * alpha
    for j, sj in enumerate(s_new):
        pj = jnp.exp(sj - m_new)
        l = l + pj
        acc = acc + pj * nl_ref[0, j:j + 1, :]
    o_ref[0] = acc / l


def _paged_attention(page_table, q_lat, q_rope, new_lat, new_kr, cache_lat, cache_kr_t, *, t_new, scale):
    nb, rows, kv = q_lat.shape
    rope = q_rope.shape[2]
    n_pages = page_table.shape[1]
    page = cache_lat.shape[1]
    group = math.gcd(PAGE_GROUP, n_pages)
    n_groups = n_pages // group
    slots = PAGE_SLOTS
    grid_spec = pltpu.PrefetchScalarGridSpec(
        num_scalar_prefetch=1,
        grid=(nb,),
        in_specs=[
            pl.BlockSpec((1, rows, kv), lambda b, pt: (b, 0, 0)),
            pl.BlockSpec((1, rows, rope), lambda b, pt: (b, 0, 0)),
            pl.BlockSpec((rows, 1), lambda b, pt: (0, 0)),
            pl.BlockSpec((1, t_new, kv), lambda b, pt: (b, 0, 0)),
            pl.BlockSpec((1, t_new, rope), lambda b, pt: (b, 0, 0)),
            pl.BlockSpec(memory_space=pl.ANY),
            pl.BlockSpec(memory_space=pl.ANY),
        ],
        out_specs=pl.BlockSpec((1, rows, kv), lambda b, pt: (b, 0, 0)),
        scratch_shapes=[
            pltpu.VMEM((slots, group * page, kv), F32),
            pltpu.VMEM((slots, rope, group * page), F32),
            pltpu.SemaphoreType.DMA((2, slots, group)),
        ],
    )
    tok = (jnp.arange(rows, dtype=jnp.int32) % t_new).reshape(rows, 1)
    return pl.pallas_call(
        functools.partial(_paged_kernel, n_seq=nb, n_groups=n_groups, group=group, page=page, slots=slots,
                          t_new=t_new, scale=scale),
        grid_spec=grid_spec,
        out_shape=jax.ShapeDtypeStruct((nb, rows, kv), F32),
        compiler_params=_params(1),
        name="paged_attn",
    )(page_table, q_lat, q_rope, tok, new_lat, new_kr, cache_lat, cache_kr_t)


def _split3(x):
    hi = x.astype(BF16).astype(F32)
    r1 = x - hi
    mid = r1.astype(BF16).astype(F32)
    lo = (r1 - mid).astype(BF16).astype(F32)
    return hi, mid, lo


def _pad_rows(x, rows):
    if x.shape[0] == rows:
        return x
    return jnp.concatenate([x, jnp.zeros((rows - x.shape[0], x.shape[1]), x.dtype)], axis=0)


def _hgrn_kernel(*refs, heads, dk, chunk, valid, n_chunks, has_s0):
    it = iter(refs)
    hq_ref, hf_ref, hi_ref, hg_ref, lb_ref, gn_ref = (next(it) for _ in range(6))
    s0_ref = next(it) if has_s0 else None
    o_ref, so_ref, st_ref = next(it), next(it), next(it)

    @pl.when(pl.program_id(2) == 0)
    def _():
        for h in range(heads):
            st_ref[h] = s0_ref[0, h].T if has_s0 else jnp.zeros((dk, dk), F32)

    keys = max(chunk, LANES)
    row = lax.broadcasted_iota(jnp.int32, (chunk, keys), 0)
    col = lax.broadcasted_iota(jnp.int32, (chunk, keys), 1)
    causal = col <= row
    tri = jnp.where(causal, 1.0, 0.0)
    live = lax.broadcasted_iota(jnp.int32, (chunk, dk), 0) < valid

    def body(c, carry):
        r0 = pl.multiple_of(c * chunk, chunk)
        for h in range(heads):
            cols = slice(h * dk, (h + 1) * dk)
            lb = lb_ref[:, cols]
            q = jax.nn.silu(hq_ref[pl.ds(r0, chunk), cols]) * dk ** -0.5
            f = lb + (1.0 - lb) * jax.nn.sigmoid(hf_ref[pl.ds(r0, chunk), cols])
            logf = jnp.log(f)
            k = 1.0 - f
            v = hi_ref[pl.ds(r0, chunk), cols]
            if valid < chunk:
                logf = jnp.where(live, logf, 0.0)
                k = jnp.where(live, k, 0.0)
                v = jnp.where(live, v, 0.0)
            p0, p1, p2 = _split3(_pad_rows(logf, keys))
            bsum = _dot(tri, p0) + _dot(tri, p1) + _dot(tri, p2)
            b_last = bsum[chunk - 1:chunk, :]
            q_dec = q * jnp.exp(bsum)
            k_inv = k * jnp.exp(-bsum)
            v_pad = _pad_rows(v, keys)
            att = jnp.where(causal, _dot_nt(q_dec, _pad_rows(k_inv, keys)), 0.0)
            st = st_ref[h]
            o = _dot(att, v_pad) + _dot_nt(q_dec, st)
            k_end = k * jnp.exp(b_last - bsum)
            st_ref[h] = st * jnp.exp(b_last) + _dot_tn(v_pad, _pad_rows(k_end, keys))
            o = _rms(o, gn_ref[...]) * jax.nn.silu(hg_ref[pl.ds(r0, chunk), cols])
            o_ref[pl.ds(r0, chunk), cols] = o
        return carry

    lax.fori_loop(0, n_chunks, body, 0)

    @pl.when(pl.program_id(2) == pl.num_programs(2) - 1)
    def _():
        for h in range(heads):
            so_ref[0, h] = st_ref[h].T


def _hgrn(zq, zf, zi, zg, lb, hg_norm, s0, *, offs, n_seq, rows, row_block, heads_total, heads, dk, chunk, valid):
    hw = heads * dk
    n_hb = heads_total // heads
    n_rb = rows // row_block
    n_chunks = row_block // chunk

    def zspec(off):
        return pl.BlockSpec((row_block, hw), lambda s, j, r: (s * n_rb + r, off // hw + j))

    in_specs = [zspec(o) for o in offs] + [
        pl.BlockSpec((1, hw), lambda s, j, r: (0, j)),
        pl.BlockSpec((1, dk), lambda s, j, r: (0, 0)),
    ]
    args = [zq, zf, zi, zg, lb.reshape(1, -1), hg_norm.reshape(1, dk)]
    if s0 is not None:
        in_specs.append(pl.BlockSpec((1, heads, dk, dk), lambda s, j, r: (s, j, 0, 0)))
        args.append(s0)
    return pl.pallas_call(
        functools.partial(_hgrn_kernel, heads=heads, dk=dk, chunk=chunk, valid=valid, n_chunks=n_chunks,
                          has_s0=s0 is not None),
        grid=(n_seq, n_hb, n_rb),
        in_specs=in_specs,
        out_specs=[
            pl.BlockSpec((row_block, hw), lambda s, j, r: (s * n_rb + r, j)),
            pl.BlockSpec((1, heads, dk, dk), lambda s, j, r: (s, j, 0, 0)),
        ],
        out_shape=[
            jax.ShapeDtypeStruct((n_seq * rows, heads_total * dk), F32),
            jax.ShapeDtypeStruct((n_seq, heads_total, dk, dk), F32),
        ],
        scratch_shapes=[pltpu.VMEM((heads, dk, dk), F32)],
        compiler_params=_params(3),
        name="hgrn",
    )(*args)


def _gated_kernel(a_ref, b_ref, ga_ref, gb_ref, wa_ref, wb_ref, o_ref):
    o_ref[...] = (jax.nn.sigmoid(ga_ref[...]) * _dot(a_ref[...], wa_ref[...])
                  + jax.nn.sigmoid(gb_ref[...]) * _dot(b_ref[...], wb_ref[...]))


def _gated_merge(a, b, z, w_a, w_b, *, off_ga, off_gb, tm=512, tn=512):
    m, ka = a.shape
    kb = b.shape[1]
    n = w_a.shape[1]
    tn = _tile(n, tn)
    return pl.pallas_call(
        _gated_kernel,
        grid=(m // tm, n // tn),
        in_specs=[
            pl.BlockSpec((tm, ka), lambda i, j: (i, 0)),
            pl.BlockSpec((tm, kb), lambda i, j: (i, 0)),
            pl.BlockSpec((tm, tn), lambda i, j: (i, off_ga // tn + j)),
            pl.BlockSpec((tm, tn), lambda i, j: (i, off_gb // tn + j)),
            pl.BlockSpec((ka, tn), lambda i, j: (0, j)),
            pl.BlockSpec((kb, tn), lambda i, j: (0, j)),
        ],
        out_specs=pl.BlockSpec((tm, tn), lambda i, j: (i, j)),
        out_shape=jax.ShapeDtypeStruct((m, n), F32),
        compiler_params=_params(2),
        name="gated_merge",
    )(a, b, z, z, w_a, w_b)


def _cross_kernel(q_ref, k_ref, v_ref, o_ref, *, heads, dh):
    for h in range(heads):
        cols = slice(h * dh, (h + 1) * dh)
        s = _dot_nt(q_ref[0, :, cols], k_ref[0, :, cols]) * dh ** -0.5
        p = jnp.exp(s - jnp.max(s, axis=-1, keepdims=True))
        p = p / jnp.sum(p, axis=-1, keepdims=True)
        o_ref[0, :, cols] = _dot(p, v_ref[0, :, cols])


def _cross_attention(q, k, v, *, heads, groups_per_mem):
    n, r, w = q.shape
    mem = k.shape[1]
    return pl.pallas_call(
        functools.partial(_cross_kernel, heads=heads, dh=w // heads),
        grid=(n,),
        in_specs=[
            pl.BlockSpec((1, r, w), lambda i: (i, 0, 0)),
            pl.BlockSpec((1, mem, w), lambda i: (i // groups_per_mem, 0, 0)),
            pl.BlockSpec((1, mem, w), lambda i: (i // groups_per_mem, 0, 0)),
        ],
        out_specs=pl.BlockSpec((1, r, w), lambda i: (i, 0, 0)),
        out_shape=jax.ShapeDtypeStruct((n, r, w), F32),
        compiler_params=_params(1),
        name="cross_attn",
    )(q, k, v)


def _moe_kernel(be_ref, na_ref, x_ref, gate_ref, g_ref, wgu_ref, wdn_ref, o_ref, *, d_expert):
    @pl.when(pl.program_id(0) < na_ref[0])
    def _():
        h = _rms(x_ref[...], g_ref[...])
        gu = _dot(h, wgu_ref[0])
        act = jax.nn.silu(gu[:, :d_expert]) * gu[:, d_expert:]
        o_ref[...] = _dot(act, wdn_ref[0]) * gate_ref[...]

    @pl.when(pl.program_id(0) >= na_ref[0])
    def _():
        o_ref[...] = jnp.zeros_like(o_ref)


def _moe_ffn(block_expert, n_active, x_sorted, row_gate, norm_ffn, w_gu, w_dn, *, block):
    n_rows, d = x_sorted.shape
    n_blocks = n_rows // block
    d_expert = w_dn.shape[1]
    grid_spec = pltpu.PrefetchScalarGridSpec(
        num_scalar_prefetch=2,
        grid=(n_blocks,),
        in_specs=[
            pl.BlockSpec((block, d), lambda i, be, na: (i, 0)),
            pl.BlockSpec((block, 1), lambda i, be, na: (i, 0)),
            pl.BlockSpec((1, d), lambda i, be, na: (0, 0)),
            pl.BlockSpec((1, d, 2 * d_expert), lambda i, be, na: (be[i], 0, 0)),
            pl.BlockSpec((1, d_expert, d), lambda i, be, na: (be[i], 0, 0)),
        ],
        out_specs=pl.BlockSpec((block, d), lambda i, be, na: (i, 0)),
    )
    return pl.pallas_call(
        functools.partial(_moe_kernel, d_expert=d_expert),
        grid_spec=grid_spec,
        out_shape=jax.ShapeDtypeStruct((n_rows, d), F32),
        compiler_params=_params(1),
        name="moe_ffn",
    )(block_expert, n_active, x_sorted, row_gate, norm_ffn.reshape(1, d), w_gu, w_dn)


def _final_kernel(x_ref, ya_ref, yb_ref, g_ref, o_ref):
    o_ref[...] = _rms(x_ref[...] + (ya_ref[...] + yb_ref[...]), g_ref[...])


def _final_norm(x, ya, yb, g, *, tm=512):
    m, d = x.shape
    spec = pl.BlockSpec((tm, d), lambda i: (i, 0))
    return pl.pallas_call(
        _final_kernel,
        grid=(m // tm,),
        in_specs=[spec, spec, spec, pl.BlockSpec((1, d), lambda i: (0, 0))],
        out_specs=spec,
        out_shape=jax.ShapeDtypeStruct((m, d), F32),
        compiler_params=_params(1),
        name="final_norm",
    )(x, ya, yb, g.reshape(1, d))


def _top_k(x, k):
    lane = jnp.arange(x.shape[-1], dtype=jnp.int32)
    vals, idxs = [], []
    for _ in range(k):
        i = jnp.argmax(x, axis=-1).astype(jnp.int32)
        vals.append(jnp.max(x, axis=-1))
        idxs.append(i)
        x = jnp.where(lane == i[..., None], -jnp.inf, x)
    return jnp.stack(vals, axis=-1), jnp.stack(idxs, axis=-1)


def _route(logits, b_rg, b_re, n_groups, n_experts, moe_block):
    n = logits.shape[0]
    epg = n_experts // n_groups
    p_group = jax.nn.softmax(logits[:, :n_groups] + b_rg, axis=-1)
    g_w, g_idx = _top_k(p_group, 1)
    e_logits = (logits[:, n_groups:n_groups + n_experts] + b_re).reshape(n, n_groups, epg)
    in_group = jnp.take_along_axis(e_logits, g_idx[:, :, None], axis=1)[:, 0]
    top_logit, top_local = _top_k(in_group, TOP_K)
    gate = g_w * jax.nn.softmax(top_logit, axis=-1)
    expert_idx = g_idx * epg + top_local

    a = n * TOP_K
    flat_e = expert_idx.reshape(a).astype(jnp.int32)
    onehot = (flat_e[:, None] == jnp.arange(n_experts, dtype=jnp.int32)[None, :]).astype(jnp.int32)
    rank = jnp.sum((jnp.cumsum(onehot, axis=0) - onehot) * onehot, axis=1)
    counts = jnp.sum(onehot, axis=0)
    padded = (counts + moe_block - 1) // moe_block * moe_block
    pad_end = jnp.cumsum(padded)
    pad_start = pad_end - padded
    dest = pad_start[flat_e] + rank
    n_blocks = (a + n_experts * (moe_block - 1) + moe_block - 1) // moe_block
    n_rows = n_blocks * moe_block
    row_token = (jnp.arange(n_rows, dtype=jnp.int32) % n).at[dest].set(jnp.arange(a, dtype=jnp.int32) // TOP_K)
    row_gate = jnp.zeros((n_rows,), F32).at[dest].set(gate.reshape(a))
    n_active = (pad_end[-1] // moe_block).astype(jnp.int32)
    blk = jnp.minimum(jnp.arange(n_blocks, dtype=jnp.int32), n_active - 1) * moe_block
    block_expert = jnp.minimum(jnp.searchsorted(pad_end, blk, side='right'), n_experts - 1).astype(jnp.int32)
    return row_token, row_gate, dest.reshape(n, TOP_K), block_expert, n_active.reshape(1)


def kernel(x_prompt, mem_prompt, x_sample, cache_kv_latent, cache_k_rope, page_table, state_hgrn, cache_mem_k, cache_mem_v, norm_mix, w_in, q_norm, w_uq, kv_norm, w_uk, w_uv, lb_logits, hg_norm, w_out_mla, w_out_hgrn, w_out, norm_cross, norm_mem, w_cq, w_ckv, w_co, norm_ffn, w_router_group, b_router_group, w_router_expert, b_router_expert, w_expert_gate_up, w_expert_down, norm_final):
    batch, seq, d = x_prompt.shape
    dec_batch, dec_seq, _ = x_sample.shape
    depth = w_in.shape[0]
    assert depth == 1
    page = cache_kv_latent.shape[2]
    kv_lora = cache_kv_latent.shape[3]
    rope = cache_k_rope.shape[3]
    half = rope // 2
    past_len = page_table.shape[1] * page
    q_lora = q_norm.shape[1]
    hg_heads, hg_dk = state_hgrn.shape[2], state_hgrn.shape[3]
    hgw = hg_heads * hg_dk
    mla_heads = (w_uq.shape[2] - w_uk.shape[2]) // rope
    nope = w_uk.shape[2] // mla_heads
    v_head = w_uv.shape[2] // mla_heads
    mem_tokens, x_heads, x_dh = cache_mem_k.shape[2], cache_mem_k.shape[3], cache_mem_k.shape[4]
    xw = x_heads * x_dh
    n_groups = w_router_group.shape[2]
    n_experts = w_router_expert.shape[2]
    mp = batch * seq
    ms = dec_batch * dec_seq
    m_all = mp + ms
    mla_scale = (nope + rope) ** -0.5
    assert nope == LANES and 2 * half <= LANES and hg_dk == LANES

    n_front = q_lora + kv_lora + rope
    off_cq, off_ckv, off_kr = 0, q_lora, q_lora + kv_lora
    off_hq, off_hf, off_hi, off_hg = 0, hgw, 2 * hgw, 3 * hgw
    off_ga, off_gb = 4 * hgw, 4 * hgw + d
    w_in_t = jnp.swapaxes(w_in[0], 0, 1)
    w_back = w_in_t[n_front:].astype(BF16)
    w_front = jnp.concatenate([w_in_t[:n_front], jnp.zeros((LANES - rope, d), F32)], axis=0).astype(BF16)

    x0 = jnp.concatenate([x_prompt.reshape(mp, d), x_sample.reshape(ms, d)], axis=0)
    pos = jnp.concatenate([jnp.tile(jnp.arange(seq), batch), jnp.tile(past_len + jnp.arange(dec_seq), dec_batch)])
    cos_t, sin_t = _rope_tables(pos, half)

    tm_in = _tile(m_all, 1152, unit=2 * SUBLANES)
    z = _matmul(x0, w_back, w_is_nk=True, gain=norm_mix[0], tm=tm_in, tn_cap=1024, name="mm_in")
    z_front = _matmul(x0, w_front, w_is_nk=True, gain=norm_mix[0], tm=tm_in, tn_cap=1024, name="mm_in_front")
    lat, kr_pad = _lat_kr(z_front, kv_norm[0], cos_t, sin_t, off_ckv=off_ckv, off_kr=off_kr, kv_lora=kv_lora,
                          half=half)
    w_uq_p = jnp.pad(w_uq[0].reshape(q_lora, mla_heads, nope + rope).transpose(1, 0, 2),
                     ((0, 0), (0, 0), (0, LANES - rope)))
    q_all = _q_proj(z_front, w_uq_p, q_norm[0], cos_t, sin_t, off_cq=off_cq, q_lora=q_lora, nope=nope, half=half)

    k_all = _k_proj(lat, w_uk[0], kr_pad, rows=mp, heads=mla_heads, nope=nope)
    v_all = _matmul(lat[:mp], w_uv[0], out_dtype=BF16, tn_cap=1024, name="v_proj")
    a_p = _flash(q_all, k_all, v_all, batch=batch, seq=seq, heads=mla_heads, dv=v_head, scale=mla_scale)

    q_lat = _q_lat(q_all, w_uk[0], row0=mp, rows=ms, nope=nope)
    rows_s = mla_heads * dec_seq
    q_lat = q_lat.reshape(mla_heads, dec_batch, dec_seq, kv_lora).transpose(1, 0, 2, 3).reshape(dec_batch, rows_s, kv_lora)
    q_rope_s = q_all[:, mp:, nope:nope + rope].astype(F32)
    q_rope_s = q_rope_s.reshape(mla_heads, dec_batch, dec_seq, rope).transpose(1, 0, 2, 3).reshape(dec_batch, rows_s, rope)
    lat_s = lat[mp:].reshape(dec_batch, dec_seq, kv_lora)
    kr_s = kr_pad[mp:, :rope].reshape(dec_batch, dec_seq, rope)
    o_lat = _paged_attention(page_table, q_lat, q_rope_s, lat_s, kr_s,
                             cache_kv_latent.reshape(-1, page, kv_lora),
                             jnp.swapaxes(cache_k_rope, 2, 3).reshape(-1, rope, page),
                             t_new=dec_seq, scale=mla_scale)
    o_lat = o_lat.reshape(dec_batch, mla_heads, dec_seq, kv_lora).transpose(1, 0, 2, 3).reshape(mla_heads, ms, kv_lora)
    a_s = _o_uv(o_lat, w_uv[0], dv=v_head)

    lower_bounds = jnp.cumsum(jax.nn.softmax(lb_logits.astype(F32), axis=0), axis=0)
    lb = lower_bounds[0]
    chunk_p = math.gcd(HG_CHUNK, seq)
    b_p, s_p = _hgrn(z, z, z, z, lb, hg_norm[0], None, offs=(off_hq, off_hf, off_hi, off_hg), n_seq=batch, rows=seq,
                     row_block=_tile(seq, 1024, unit=chunk_p), heads_total=hg_heads, heads=math.gcd(4, hg_heads),
                     dk=hg_dk, chunk=chunk_p, valid=chunk_p)
    rows_pad = -(-dec_seq // SUBLANES) * SUBLANES
    z_s = jnp.pad(z[mp:, :4 * hgw].reshape(dec_batch, dec_seq, 4 * hgw), ((0, 0), (0, rows_pad - dec_seq), (0, 0)))
    z_s = z_s.reshape(dec_batch * rows_pad, 4 * hgw)
    b_s, s_s = _hgrn(z_s, z_s, z_s, z_s, lb, hg_norm[0], state_hgrn[0], offs=(off_hq, off_hf, off_hi, off_hg),
                     n_seq=dec_batch, rows=rows_pad, row_block=rows_pad, heads_total=hg_heads, heads=hg_heads,
                     dk=hg_dk, chunk=rows_pad, valid=dec_seq)
    b_s = b_s.reshape(dec_batch, rows_pad, hgw)[:, :dec_seq].reshape(ms, hgw)

    a_all = jnp.concatenate([a_p, a_s], axis=0)
    b_all = jnp.concatenate([b_p, b_s], axis=0)
    merged = _gated_merge(a_all, b_all, z, w_out_mla[0], w_out_hgrn[0], off_ga=off_ga, off_gb=off_gb)
    x1 = _matmul(merged, w_out[0], res=x0, name="mm_out")

    kv_mem = _matmul(mem_prompt.reshape(batch * mem_tokens, d), w_ckv[0], gain=norm_mem[0], tm=batch * mem_tokens,
                     name="mem_kv")
    mk_p, mv_p = kv_mem[:, :xw].reshape(batch, mem_tokens, xw), kv_mem[:, xw:].reshape(batch, mem_tokens, xw)
    qx = _matmul(x1, w_cq[0], gain=norm_cross[0], name="cross_q")
    tq_x = 512
    ox_p = _cross_attention(qx[:mp].reshape(mp // tq_x, tq_x, xw), mk_p, mv_p, heads=x_heads,
                            groups_per_mem=seq // tq_x)
    qx_s = jnp.pad(qx[mp:].reshape(dec_batch, dec_seq, xw), ((0, 0), (0, rows_pad - dec_seq), (0, 0)))
    ox_s = _cross_attention(qx_s, cache_mem_k[0].reshape(dec_batch, mem_tokens, xw),
                            cache_mem_v[0].reshape(dec_batch, mem_tokens, xw), heads=x_heads, groups_per_mem=1)
    ox = jnp.concatenate([ox_p.reshape(mp, xw), ox_s[:, :dec_seq].reshape(ms, xw)], axis=0)
    x2 = _matmul(ox, w_co[0], res=x1, name="cross_out")

    n_r = n_groups + n_experts
    w_r = jnp.concatenate([jnp.swapaxes(w_router_group[0], 0, 1), jnp.swapaxes(w_router_expert[0], 0, 1),
                           jnp.zeros((-n_r % LANES, d), F32)], axis=0)
    logits = _matmul(x2, w_r, w_is_nk=True, gain=norm_ffn[0], name="router")
    moe_block = 128

    def route(lg):
        return _route(lg, b_router_group[0], b_router_expert[0], n_groups, n_experts, moe_block)

    row_token, row_gate, dest, block_expert, n_active = route(logits)
    x_sorted = x2[row_token]
    y_sorted = _moe_ffn(block_expert, n_active, x_sorted, row_gate[:, None], norm_ffn[0],
                        w_expert_gate_up[0], w_expert_down[0], block=moe_block)
    y = _final_norm(x2, y_sorted[dest[:, 0]], y_sorted[dest[:, 1]], norm_final)

    y_prompt = y[:mp].reshape(batch, seq, d)
    y_sample = y[mp:].reshape(dec_batch, dec_seq, d)
    n_prompt_pages = mp // page
    kv_latent_prompt = lat[:mp].reshape(depth, n_prompt_pages, page, kv_lora)
    k_rope_prompt = kr_pad[:mp, :rope].reshape(depth, n_prompt_pages, page, rope)
    state_hgrn_prompt = s_p[None]
    mem_k_prompt = mk_p.reshape(depth, batch, mem_tokens, x_heads, x_dh)
    mem_v_prompt = mv_p.reshape(depth, batch, mem_tokens, x_heads, x_dh)
    kv_latent_sample = lat_s[None]
    k_rope_sample = kr_s[None]
    state_hgrn_sample = s_s[None]
    return (y_prompt, y_sample, kv_latent_prompt, k_rope_prompt, state_hgrn_prompt, mem_k_prompt, mem_v_prompt,
            kv_latent_sample, k_rope_sample, state_hgrn_sample)
```

```python
import functools
import math

import jax
import jax.numpy as jnp
from jax import lax
from jax.experimental import pallas as pl
from jax.experimental.pallas import tpu as pltpu

F32 = jnp.float32
BF16 = jnp.bfloat16

EPS = 1e-6
NEG = -1e30
ROPE_THETA = 10000.0
TOP_K = 2
HG_CHUNK = 64
LANES = 128
SUBLANES = 8
VMEM_LIMIT = 56 * 1024 * 1024
PAGE_GROUP = 8
PAGE_SLOTS = 4


def _params(n_axes):
    return pltpu.CompilerParams(dimension_semantics=("arbitrary",) * n_axes, vmem_limit_bytes=VMEM_LIMIT)


def _tile(n, cap, unit=LANES):
    if n <= cap:
        return n
    t = cap - cap % unit
    while t >= unit:
        if n % t == 0:
            return t
        t -= unit
    raise ValueError(f"no tile for {n} under {cap}")


def _rms(x, g):
    return x * lax.rsqrt(jnp.mean(x * x, axis=-1, keepdims=True) + EPS) * g


def _dot(a, b):
    return jnp.dot(a, b, preferred_element_type=F32)


def _dot_nt(a, b):
    return lax.dot_general(a, b, (((1,), (1,)), ((), ())), preferred_element_type=F32)


def _dot_tn(a, b):
    return lax.dot_general(a, b, (((0,), (0,)), ((), ())), preferred_element_type=F32)


def _mm_kernel(*refs, has_gain, has_res, w_is_nk):
    it = iter(refs)
    x_ref, w_ref = next(it), next(it)
    g_ref = next(it) if has_gain else None
    r_ref = next(it) if has_res else None
    o_ref = next(it)
    if has_gain:
        xs_ref = next(it)

        @pl.when(pl.program_id(1) == 0)
        def _():
            xs_ref[...] = _rms(x_ref[...].astype(F32), g_ref[...]).astype(xs_ref.dtype)

        x = xs_ref[...]
    else:
        x = x_ref[...].astype(w_ref.dtype)
    acc = _dot_nt(x, w_ref[...]) if w_is_nk else _dot(x, w_ref[...])
    if has_res:
        acc = r_ref[...] + acc
    o_ref[...] = acc.astype(o_ref.dtype)


def _matmul(x, w, *, w_is_nk=False, gain=None, res=None, x_col=0, tm=512, tn_cap=512, out_dtype=F32, name="mm"):
    m = x.shape[0]
    n, k = w.shape if w_is_nk else w.shape[::-1]
    assert x_col % k == 0 and m % tm == 0
    tn = _tile(n, tn_cap)
    xb = x_col // k
    w_spec = pl.BlockSpec((tn, k), lambda i, j: (j, 0)) if w_is_nk else pl.BlockSpec((k, tn), lambda i, j: (0, j))
    in_specs = [pl.BlockSpec((tm, k), lambda i, j: (i, xb)), w_spec]
    args = [x, w]
    scratch = []
    if gain is not None:
        in_specs.append(pl.BlockSpec((1, k), lambda i, j: (0, 0)))
        args.append(gain.reshape(1, k).astype(F32))
        scratch.append(pltpu.VMEM((tm, k), w.dtype))
    if res is not None:
        in_specs.append(pl.BlockSpec((tm, tn), lambda i, j: (i, j)))
        args.append(res)
    return pl.pallas_call(
        functools.partial(_mm_kernel, has_gain=gain is not None, has_res=res is not None, w_is_nk=w_is_nk),
        grid=(m // tm, n // tn),
        in_specs=in_specs,
        out_specs=pl.BlockSpec((tm, tn), lambda i, j: (i, j)),
        out_shape=jax.ShapeDtypeStruct((m, n), out_dtype),
        scratch_shapes=scratch,
        compiler_params=_params(2),
        name=name,
    )(*args)


def _rope_tile(x, cos_t, sin_t, half):
    lane = lax.broadcasted_iota(jnp.int32, x.shape, 1)
    swapped = jnp.where(lane < half, pltpu.roll(x, LANES - half, 1), pltpu.roll(x, half, 1))
    return x * cos_t + swapped * sin_t


def _rope_tables(pos, half):
    inv = ROPE_THETA ** (-jnp.arange(half, dtype=F32) / half)
    ang = pos.astype(F32)[:, None] * inv[None, :]
    cos, sin = jnp.cos(ang), jnp.sin(ang)
    zeros = jnp.zeros((pos.shape[0], LANES - 2 * half), F32)
    return jnp.concatenate([cos, cos, zeros], axis=1), jnp.concatenate([-sin, sin, zeros], axis=1)


def _latkr_kernel(ckv_ref, kr_ref, g_ref, cos_ref, sin_ref, lat_ref, kro_ref, *, half):
    lat_ref[...] = _rms(ckv_ref[...], g_ref[...])
    kro_ref[...] = _rope_tile(kr_ref[...], cos_ref[...], sin_ref[...], half)


def _lat_kr(z, kv_norm, cos_t, sin_t, *, off_ckv, off_kr, kv_lora, half, tm=512):
    m = z.shape[0]
    return pl.pallas_call(
        functools.partial(_latkr_kernel, half=half),
        grid=(m // tm,),
        in_specs=[
            pl.BlockSpec((tm, kv_lora), lambda i: (i, off_ckv // kv_lora)),
            pl.BlockSpec((tm, LANES), lambda i: (i, off_kr // LANES)),
            pl.BlockSpec((1, kv_lora), lambda i: (0, 0)),
            pl.BlockSpec((tm, LANES), lambda i: (i, 0)),
            pl.BlockSpec((tm, LANES), lambda i: (i, 0)),
        ],
        out_specs=[pl.BlockSpec((tm, kv_lora), lambda i: (i, 0)), pl.BlockSpec((tm, LANES), lambda i: (i, 0))],
        out_shape=[jax.ShapeDtypeStruct((m, kv_lora), F32), jax.ShapeDtypeStruct((m, LANES), F32)],
        compiler_params=_params(1),
        name="lat_kr",
    )(z, z, kv_norm.reshape(1, kv_lora), cos_t, sin_t)


def _qproj_kernel(cq_ref, w_ref, g_ref, cos_ref, sin_ref, o_ref, *, heads, nope, half, scale):
    width = nope + LANES
    q = _dot(_rms(cq_ref[...], g_ref[...]).astype(w_ref.dtype), w_ref[...]) * scale
    cos_t, sin_t = cos_ref[...], sin_ref[...]
    for h in range(heads):
        o_ref[h, :, :nope] = q[:, h * width:h * width + nope].astype(o_ref.dtype)
        o_ref[h, :, nope:] = _rope_tile(q[:, h * width + nope:(h + 1) * width], cos_t, sin_t, half).astype(o_ref.dtype)


def _q_proj(z, w_uq_p, q_norm, cos_t, sin_t, *, off_cq, q_lora, heads, nope, half, scale, tm=512):
    m = z.shape[0]
    width = nope + LANES
    return pl.pallas_call(
        functools.partial(_qproj_kernel, heads=heads, nope=nope, half=half, scale=scale),
        grid=(m // tm,),
        in_specs=[
            pl.BlockSpec((tm, q_lora), lambda i: (i, off_cq // q_lora)),
            pl.BlockSpec((q_lora, heads * width), lambda i: (0, 0)),
            pl.BlockSpec((1, q_lora), lambda i: (0, 0)),
            pl.BlockSpec((tm, LANES), lambda i: (i, 0)),
            pl.BlockSpec((tm, LANES), lambda i: (i, 0)),
        ],
        out_specs=pl.BlockSpec((heads, tm, width), lambda i: (0, i, 0)),
        out_shape=jax.ShapeDtypeStruct((heads, m, width), BF16),
        compiler_params=_params(1),
        name="q_proj",
    )(z, w_uq_p, q_norm.reshape(1, q_lora), cos_t, sin_t)


def _kvproj_kernel(lat_ref, wk_ref, wv_ref, kr_ref, k_ref, v_ref, *, heads, nope):
    lat = lat_ref[...].astype(wk_ref.dtype)
    k = _dot(lat, wk_ref[...])
    kr = kr_ref[...].astype(k_ref.dtype)
    for h in range(heads):
        k_ref[h, :, :nope] = k[:, h * nope:(h + 1) * nope].astype(k_ref.dtype)
        k_ref[h, :, nope:] = kr
    v_ref[...] = _dot(lat, wv_ref[...]).astype(v_ref.dtype)


def _kv_proj(lat, w_uk, w_uv, kr_pad, *, rows, heads, nope, tm=512):
    kv_lora = lat.shape[1]
    nv = w_uv.shape[1]
    return pl.pallas_call(
        functools.partial(_kvproj_kernel, heads=heads, nope=nope),
        grid=(rows // tm,),
        in_specs=[
            pl.BlockSpec((tm, kv_lora), lambda i: (i, 0)),
            pl.BlockSpec((kv_lora, heads * nope), lambda i: (0, 0)),
            pl.BlockSpec((kv_lora, nv), lambda i: (0, 0)),
            pl.BlockSpec((tm, LANES), lambda i: (i, 0)),
        ],
        out_specs=[
            pl.BlockSpec((heads, tm, nope + LANES), lambda i: (0, i, 0)),
            pl.BlockSpec((tm, nv), lambda i: (i, 0)),
        ],
        out_shape=[
            jax.ShapeDtypeStruct((heads, rows, nope + LANES), BF16),
            jax.ShapeDtypeStruct((rows, nv), BF16),
        ],
        compiler_params=_params(1),
        name="kv_proj",
    )(lat, w_uk, w_uv, kr_pad)


def _flash_kernel(q_ref, k_ref, v_ref, buf_ref, o_ref, *, tq, tk):
    del buf_ref
    qi = pl.program_id(2)
    q = q_ref[0]
    dv = v_ref.shape[1]

    def step(kb, carry, diag):
        m, l, acc = carry
        start = pl.multiple_of(kb * tk, tk)
        k = k_ref[0, pl.ds(start, tk), :]
        v = v_ref[pl.ds(start, tk), :]
        s = _dot_nt(q, k)
        if diag is not None:
            row = lax.broadcasted_iota(jnp.int32, s.shape, 0)
            col = lax.broadcasted_iota(jnp.int32, s.shape, 1) + diag * tk
            s = jnp.where(col <= row, s, NEG)
        m_new = jnp.maximum(m, jnp.max(s, axis=-1, keepdims=True))
        alpha = jnp.exp(m - m_new)
        p = jnp.exp(s - m_new)
        l = l * alpha + jnp.sum(p, axis=-1, keepdims=True)
        acc = acc * alpha + _dot(p.astype(v.dtype), v)
        return m_new, l, acc

    n_full = qi * (tq // tk)
    carry = (jnp.full((tq, 1), NEG, F32), jnp.zeros((tq, 1), F32), jnp.zeros((tq, dv), F32))
    carry = lax.fori_loop(0, n_full, functools.partial(step, diag=None), carry)
    for j in range(tq // tk):
        carry = step(n_full + j, carry, j)
    _, l, acc = carry
    o_ref[...] = (acc / l).astype(o_ref.dtype)


def _flash(q_all, k_all, v_all, buf, *, batch, seq, heads, dv, tq=1024, tk=512):
    nq = seq // tq
    width = q_all.shape[2]
    assert tq % tk == 0
    return pl.pallas_call(
        functools.partial(_flash_kernel, tq=tq, tk=tk),
        grid=(batch, heads, nq),
        in_specs=[
            pl.BlockSpec((1, tq, width), lambda b, h, i: (h, b * nq + i, 0)),
            pl.BlockSpec((1, seq, width), lambda b, h, i: (h, b, 0)),
            pl.BlockSpec((seq, dv), lambda b, h, i: (b, h)),
            pl.BlockSpec(memory_space=pl.ANY),
        ],
        out_specs=pl.BlockSpec((tq, dv), lambda b, h, i: (b * nq + i, h)),
        out_shape=jax.ShapeDtypeStruct(buf.shape, buf.dtype),
        input_output_aliases={3: 0},
        compiler_params=_params(3),
        name="flash",
    )(q_all, k_all, v_all, buf)


def _qlat_kernel(q_ref, w_ref, o_ref, *, nope):
    o_ref[0] = _dot_nt(q_ref[0, :, :nope], w_ref[...].astype(q_ref.dtype))


def _q_lat(q_all, w_uk, *, row0, rows, nope):
    heads, _, width = q_all.shape
    kv_lora = w_uk.shape[0]
    assert row0 % rows == 0
    return pl.pallas_call(
        functools.partial(_qlat_kernel, nope=nope),
        grid=(heads,),
        in_specs=[
            pl.BlockSpec((1, rows, width), lambda h: (h, row0 // rows, 0)),
            pl.BlockSpec((kv_lora, nope), lambda h: (0, h)),
        ],
        out_specs=pl.BlockSpec((1, rows, kv_lora), lambda h: (h, 0, 0)),
        out_shape=jax.ShapeDtypeStruct((heads, rows, kv_lora), F32),
        compiler_params=_params(1),
        name="q_lat",
    )(q_all, w_uk)


def _ouv_kernel(x_ref, w_ref, buf_ref, o_ref):
    del buf_ref
    o_ref[...] = _dot(x_ref[0].astype(w_ref.dtype), w_ref[...]).astype(o_ref.dtype)


def _o_uv(o_lat, w_uv, buf, *, dv, row0):
    heads, rows, kv_lora = o_lat.shape
    assert row0 % rows == 0
    return pl.pallas_call(
        _ouv_kernel,
        grid=(heads,),
        in_specs=[
            pl.BlockSpec((1, rows, kv_lora), lambda h: (h, 0, 0)),
            pl.BlockSpec((kv_lora, dv), lambda h: (0, h)),
            pl.BlockSpec(memory_space=pl.ANY),
        ],
        out_specs=pl.BlockSpec((rows, dv), lambda h: (row0 // rows, h)),
        out_shape=jax.ShapeDtypeStruct(buf.shape, buf.dtype),
        input_output_aliases={2: 0},
        compiler_params=_params(1),
        name="o_uv",
    )(o_lat, w_uv, buf)


def _paged_kernel(pt_ref, ql_ref, qr_ref, tok_ref, nl_ref, nr_ref, cl_hbm, cr_hbm, o_ref, bl_ref, br_ref, sem_ref,
                  *, n_seq, n_groups, group, page, slots, t_new):
    b = pl.program_id(0)
    total = n_seq * n_groups
    ahead = slots - 1

    def copies(t):
        seq = lax.div(t, n_groups)
        g = lax.rem(t, n_groups)
        slot = lax.rem(t, slots)
        out = []
        for i in range(group):
            pid = pt_ref[seq, g * group + i]
            keys = pl.ds(i * page, page)
            out.append(pltpu.make_async_copy(cl_hbm.at[pid], bl_ref.at[slot, keys], sem_ref.at[0, slot, i]))
            out.append(pltpu.make_async_copy(cr_hbm.at[pid], br_ref.at[slot, :, keys], sem_ref.at[1, slot, i]))
        return out

    @pl.when(b == 0)
    def _():
        for t0 in range(min(ahead, total)):
            for c in copies(jnp.int32(t0)):
                c.start()

    ql = ql_ref[0]
    qr = qr_ref[0]
    ql_lo = ql.astype(BF16)
    qr_lo = qr.astype(BF16)
    rows = ql.shape[0]
    kv = ql.shape[1]

    def body(g, carry):
        m, l, acc = carry
        t = b * n_groups + g

        @pl.when(t + ahead < total)
        def _():
            for c in copies(t + ahead):
                c.start()

        for c in copies(t):
            c.wait()
        slot = lax.rem(t, slots)
        c_blk = bl_ref[slot].astype(BF16)
        r_blk = br_ref[slot].astype(BF16)
        s = _dot_nt(ql_lo, c_blk) + _dot(qr_lo, r_blk)
        m_new = jnp.maximum(m, jnp.max(s, axis=-1, keepdims=True))
        alpha = jnp.exp(m - m_new)
        p = jnp.exp(s - m_new)
        l = l * alpha + jnp.sum(p, axis=-1, keepdims=True)
        acc = acc * alpha + _dot(p.astype(BF16), c_blk)
        return m_new, l, acc

    init = (jnp.full((rows, 1), NEG, F32), jnp.zeros((rows, 1), F32), jnp.zeros((rows, kv), F32))
    m, l, acc = lax.fori_loop(0, n_groups, body, init)

    t_of_row = tok_ref[...]
    s_new = []
    for j in range(t_new):
        sj = (jnp.sum(ql * nl_ref[0, j:j + 1, :], axis=-1, keepdims=True)
              + jnp.sum(qr * nr_ref[0, j:j + 1, :], axis=-1, keepdims=True))
        s_new.append(jnp.where(t_of_row >= j, sj, NEG))
    m_new = m
    for sj in s_new:
        m_new = jnp.maximum(m_new, sj)
    alpha = jnp.exp(m - m_new)
    l = l * alpha
    acc = acc * alpha
    for j, sj in enumerate(s_new):
        pj = jnp.exp(sj - m_new)
        l = l + pj
        acc = acc + pj * nl_ref[0, j:j + 1, :]
    o_ref[0] = acc / l


def _paged_attention(page_table, q_lat, q_rope, new_lat, new_kr, cache_lat, cache_kr_t, *, t_new):
    nb, rows, kv = q_lat.shape
    rope = q_rope.shape[2]
    n_pages = page_table.shape[1]
    page = cache_lat.shape[1]
    group = math.gcd(PAGE_GROUP, n_pages)
    n_groups = n_pages // group
    slots = PAGE_SLOTS
    grid_spec = pltpu.PrefetchScalarGridSpec(
        num_scalar_prefetch=1,
        grid=(nb,),
        in_specs=[
            pl.BlockSpec((1, rows, kv), lambda b, pt: (b, 0, 0)),
            pl.BlockSpec((1, rows, rope), lambda b, pt: (b, 0, 0)),
            pl.BlockSpec((rows, 1), lambda b, pt: (0, 0)),
            pl.BlockSpec((1, t_new, kv), lambda b, pt: (b, 0, 0)),
            pl.BlockSpec((1, t_new, rope), lambda b, pt: (b, 0, 0)),
            pl.BlockSpec(memory_space=pl.ANY),
            pl.BlockSpec(memory_space=pl.ANY),
        ],
        out_specs=pl.BlockSpec((1, rows, kv), lambda b, pt: (b, 0, 0)),
        scratch_shapes=[
            pltpu.VMEM((slots, group * page, kv), F32),
            pltpu.VMEM((slots, rope, group * page), F32),
            pltpu.SemaphoreType.DMA((2, slots, group)),
        ],
    )
    tok = (jnp.arange(rows, dtype=jnp.int32) % t_new).reshape(rows, 1)
    return pl.pallas_call(
        functools.partial(_paged_kernel, n_seq=nb, n_groups=n_groups, group=group, page=page, slots=slots,
                          t_new=t_new),
        grid_spec=grid_spec,
        out_shape=jax.ShapeDtypeStruct((nb, rows, kv), F32),
        compiler_params=_params(1),
        name="paged_attn",
    )(page_table, q_lat, q_rope, tok, new_lat, new_kr, cache_lat, cache_kr_t)


def _split3(x):
    hi = x.astype(BF16).astype(F32)
    r1 = x - hi
    mid = r1.astype(BF16).astype(F32)
    lo = (r1 - mid).astype(BF16).astype(F32)
    return hi, mid, lo


def _pad_rows(x, rows):
    if x.shape[0] == rows:
        return x
    return jnp.concatenate([x, jnp.zeros((rows - x.shape[0], x.shape[1]), x.dtype)], axis=0)


def _hgrn_kernel(hq_ref, hf_ref, hi_ref, hg_ref, lb_ref, gn_ref, buf_ref, o_ref, so_ref, st_ref,
                 *, heads, dk, chunk, n_chunks):
    del buf_ref
    @pl.when(pl.program_id(2) == 0)
    def _():
        st_ref[...] = jnp.zeros_like(st_ref)

    keys = max(chunk, LANES)
    row = lax.broadcasted_iota(jnp.int32, (chunk, keys), 0)
    col = lax.broadcasted_iota(jnp.int32, (chunk, keys), 1)
    causal = col <= row
    tri = jnp.where(causal, 1.0, 0.0)

    def body(c, carry):
        r0 = pl.multiple_of(c * chunk, chunk)
        for h in range(heads):
            cols = slice(h * dk, (h + 1) * dk)
            lb = lb_ref[:, cols]
            q = jax.nn.silu(hq_ref[pl.ds(r0, chunk), cols]) * dk ** -0.5
            f = lb + (1.0 - lb) * jax.nn.sigmoid(hf_ref[pl.ds(r0, chunk), cols])
            logf = jnp.log(f)
            k = 1.0 - f
            v = hi_ref[pl.ds(r0, chunk), cols]
            p0, p1, p2 = _split3(_pad_rows(logf, keys))
            bsum = _dot(tri, p0) + _dot(tri, p1) + _dot(tri, p2)
            b_last = bsum[chunk - 1:chunk, :]
            q_dec = q * jnp.exp(bsum)
            k_inv = k * jnp.exp(-bsum)
            v_pad = _pad_rows(v, keys)
            att = jnp.where(causal, _dot_nt(q_dec, _pad_rows(k_inv, keys)), 0.0)
            st = st_ref[h]
            o = _dot(att, v_pad) + _dot_nt(q_dec, st)
            k_end = k * jnp.exp(b_last - bsum)
            st_ref[h] = st * jnp.exp(b_last) + _dot_tn(v_pad, _pad_rows(k_end, keys))
            o = _rms(o, gn_ref[...]) * jax.nn.silu(hg_ref[pl.ds(r0, chunk), cols])
            o_ref[pl.ds(r0, chunk), cols] = o.astype(o_ref.dtype)
        return carry

    lax.fori_loop(0, n_chunks, body, 0)

    @pl.when(pl.program_id(2) == pl.num_programs(2) - 1)
    def _():
        for h in range(heads):
            so_ref[0, h] = st_ref[h].T


def _hgrn(z, lb, hg_norm, buf, *, offs, n_seq, rows, row_block, heads_total, heads, dk, chunk):
    hw = heads * dk
    n_hb = heads_total // heads
    n_rb = rows // row_block
    n_chunks = row_block // chunk

    def zspec(off):
        return pl.BlockSpec((row_block, hw), lambda s, j, r: (s * n_rb + r, off // hw + j))

    return pl.pallas_call(
        functools.partial(_hgrn_kernel, heads=heads, dk=dk, chunk=chunk, n_chunks=n_chunks),
        grid=(n_seq, n_hb, n_rb),
        in_specs=[zspec(o) for o in offs] + [
            pl.BlockSpec((1, hw), lambda s, j, r: (0, j)),
            pl.BlockSpec((1, dk), lambda s, j, r: (0, 0)),
            pl.BlockSpec(memory_space=pl.ANY),
        ],
        out_specs=[
            pl.BlockSpec((row_block, hw), lambda s, j, r: (s * n_rb + r, j)),
            pl.BlockSpec((1, heads, dk, dk), lambda s, j, r: (s, j, 0, 0)),
        ],
        out_shape=[
            jax.ShapeDtypeStruct(buf.shape, buf.dtype),
            jax.ShapeDtypeStruct((n_seq, heads_total, dk, dk), F32),
        ],
        input_output_aliases={6: 0},
        scratch_shapes=[pltpu.VMEM((heads, dk, dk), F32)],
        compiler_params=_params(3),
        name="hgrn",
    )(z, z, z, z, lb.reshape(1, -1), hg_norm.reshape(1, dk), buf)


def _hgrn_step_kernel(hq_ref, hf_ref, hi_ref, hg_ref, lb_ref, gn_ref, s0_ref, buf_ref, o_ref, so_ref,
                      *, heads, dk, n_seq, t_new):
    del buf_ref
    rows = n_seq * t_new
    for h in range(heads):
        cols = slice(h * dk, (h + 1) * dk)
        lb = lb_ref[:, cols]
        q = jax.nn.silu(hq_ref[:, cols]) * dk ** -0.5
        f = lb + (1.0 - lb) * jax.nn.sigmoid(hf_ref[:, cols])
        v = hi_ref[:, cols]
        row_id = lax.broadcasted_iota(jnp.int32, (rows, dk), 0)
        o = jnp.zeros((rows, dk), F32)
        for s in range(n_seq):
            st = s0_ref[s, h]
            for t in range(t_new):
                r = s * t_new + t
                f_col = jnp.broadcast_to(f[r:r + 1, :], (dk, dk)).T
                q_col = jnp.broadcast_to(q[r:r + 1, :], (dk, dk)).T
                st = st * f_col + (1.0 - f_col) * v[r:r + 1, :]
                o = jnp.where(row_id == r, jnp.sum(st * q_col, axis=0, keepdims=True), o)
            so_ref[s, h] = st
        o_ref[:, cols] = (_rms(o, gn_ref[...]) * jax.nn.silu(hg_ref[:, cols])).astype(o_ref.dtype)


def _hgrn_step(z, lb, hg_norm, s0, buf, *, offs, row0, n_seq_total, t_new, heads, dk):
    n_seq = 2 * SUBLANES // t_new
    rows = n_seq * t_new
    hw = heads * dk
    assert rows == 2 * SUBLANES and row0 % rows == 0 and n_seq_total % n_seq == 0
    rb0 = row0 // rows

    def zspec(off):
        return pl.BlockSpec((rows, hw), lambda i: (rb0 + i, off // hw))

    return pl.pallas_call(
        functools.partial(_hgrn_step_kernel, heads=heads, dk=dk, n_seq=n_seq, t_new=t_new),
        grid=(n_seq_total // n_seq,),
        in_specs=[zspec(o) for o in offs] + [
            pl.BlockSpec((1, hw), lambda i: (0, 0)),
            pl.BlockSpec((1, dk), lambda i: (0, 0)),
            pl.BlockSpec((n_seq, heads, dk, dk), lambda i: (i, 0, 0, 0)),
            pl.BlockSpec(memory_space=pl.ANY),
        ],
        out_specs=[
            pl.BlockSpec((rows, hw), lambda i: (rb0 + i, 0)),
            pl.BlockSpec((n_seq, heads, dk, dk), lambda i: (i, 0, 0, 0)),
        ],
        out_shape=[
            jax.ShapeDtypeStruct(buf.shape, buf.dtype),
            jax.ShapeDtypeStruct(s0.shape, F32),
        ],
        input_output_aliases={7: 0},
        compiler_params=_params(1),
        name="hgrn_step",
    )(z, z, z, z, lb.reshape(1, -1), hg_norm.reshape(1, dk), s0, buf)


def _gated_kernel(a_ref, b_ref, ga_ref, gb_ref, wa_ref, wb_ref, o_ref):
    o_ref[...] = (jax.nn.sigmoid(ga_ref[...]) * _dot(a_ref[...], wa_ref[...])
                  + jax.nn.sigmoid(gb_ref[...]) * _dot(b_ref[...], wb_ref[...])).astype(o_ref.dtype)


def _gated_merge(a, b, z, w_a, w_b, *, off_ga, off_gb, tm=512, tn=512):
    m, ka = a.shape
    kb = b.shape[1]
    n = w_a.shape[1]
    tn = _tile(n, tn)
    return pl.pallas_call(
        _gated_kernel,
        grid=(m // tm, n // tn),
        in_specs=[
            pl.BlockSpec((tm, ka), lambda i, j: (i, 0)),
            pl.BlockSpec((tm, kb), lambda i, j: (i, 0)),
            pl.BlockSpec((tm, tn), lambda i, j: (i, off_ga // tn + j)),
            pl.BlockSpec((tm, tn), lambda i, j: (i, off_gb // tn + j)),
            pl.BlockSpec((ka, tn), lambda i, j: (0, j)),
            pl.BlockSpec((kb, tn), lambda i, j: (0, j)),
        ],
        out_specs=pl.BlockSpec((tm, tn), lambda i, j: (i, j)),
        out_shape=jax.ShapeDtypeStruct((m, n), a.dtype),
        compiler_params=_params(2),
        name="gated_merge",
    )(a, b, z, z, w_a, w_b)


def _cross_kernel(q_ref, k_ref, v_ref, o_ref, *, heads, dh):
    for h in range(heads):
        cols = slice(h * dh, (h + 1) * dh)
        s = _dot_nt(q_ref[0, :, cols], k_ref[0, :, cols]) * dh ** -0.5
        p = jnp.exp(s - jnp.max(s, axis=-1, keepdims=True))
        p = p / jnp.sum(p, axis=-1, keepdims=True)
        o_ref[0, :, cols] = _dot(p, v_ref[0, :, cols])


def _cross_attention(q, k, v, *, heads, groups_per_mem):
    n, r, w = q.shape
    mem = k.shape[1]
    return pl.pallas_call(
        functools.partial(_cross_kernel, heads=heads, dh=w // heads),
        grid=(n,),
        in_specs=[
            pl.BlockSpec((1, r, w), lambda i: (i, 0, 0)),
            pl.BlockSpec((1, mem, w), lambda i: (i // groups_per_mem, 0, 0)),
            pl.BlockSpec((1, mem, w), lambda i: (i // groups_per_mem, 0, 0)),
        ],
        out_specs=pl.BlockSpec((1, r, w), lambda i: (i, 0, 0)),
        out_shape=jax.ShapeDtypeStruct((n, r, w), F32),
        compiler_params=_params(1),
        name="cross_attn",
    )(q, k, v)


def _moe_kernel(be_ref, na_ref, x_ref, gate_ref, g_ref, wgu_ref, wdn_ref, o_ref, *, d_expert):
    @pl.when(pl.program_id(0) < na_ref[0])
    def _():
        h = _rms(x_ref[...], g_ref[...])
        gu = _dot(h, wgu_ref[0])
        act = jax.nn.silu(gu[:, :d_expert]) * gu[:, d_expert:]
        o_ref[...] = _dot(act, wdn_ref[0]) * gate_ref[...]

    @pl.when(pl.program_id(0) >= na_ref[0])
    def _():
        o_ref[...] = jnp.zeros_like(o_ref)


def _moe_ffn(block_expert, n_active, x_sorted, row_gate, norm_ffn, w_gu, w_dn, *, block):
    n_rows, d = x_sorted.shape
    n_blocks = n_rows // block
    d_expert = w_dn.shape[1]
    grid_spec = pltpu.PrefetchScalarGridSpec(
        num_scalar_prefetch=2,
        grid=(n_blocks,),
        in_specs=[
            pl.BlockSpec((block, d), lambda i, be, na: (i, 0)),
            pl.BlockSpec((block, 1), lambda i, be, na: (i, 0)),
            pl.BlockSpec((1, d), lambda i, be, na: (0, 0)),
            pl.BlockSpec((1, d, 2 * d_expert), lambda i, be, na: (be[i], 0, 0)),
            pl.BlockSpec((1, d_expert, d), lambda i, be, na: (be[i], 0, 0)),
        ],
        out_specs=pl.BlockSpec((block, d), lambda i, be, na: (i, 0)),
    )
    return pl.pallas_call(
        functools.partial(_moe_kernel, d_expert=d_expert),
        grid_spec=grid_spec,
        out_shape=jax.ShapeDtypeStruct((n_rows, d), F32),
        compiler_params=_params(1),
        name="moe_ffn",
    )(block_expert, n_active, x_sorted, row_gate, norm_ffn.reshape(1, d), w_gu, w_dn)


def _final_kernel(x_ref, ya_ref, yb_ref, g_ref, o_ref):
    o_ref[...] = _rms(x_ref[...] + (ya_ref[...] + yb_ref[...]), g_ref[...])


def _final_norm(x, ya, yb, g, *, tm=512):
    m, d = x.shape
    spec = pl.BlockSpec((tm, d), lambda i: (i, 0))
    return pl.pallas_call(
        _final_kernel,
        grid=(m // tm,),
        in_specs=[spec, spec, spec, pl.BlockSpec((1, d), lambda i: (0, 0))],
        out_specs=spec,
        out_shape=jax.ShapeDtypeStruct((m, d), F32),
        compiler_params=_params(1),
        name="final_norm",
    )(x, ya, yb, g.reshape(1, d))


def _top_k(x, k):
    lane = jnp.arange(x.shape[-1], dtype=jnp.int32)
    vals, idxs = [], []
    for _ in range(k):
        i = jnp.argmax(x, axis=-1).astype(jnp.int32)
        vals.append(jnp.max(x, axis=-1))
        idxs.append(i)
        x = jnp.where(lane == i[..., None], -jnp.inf, x)
    return jnp.stack(vals, axis=-1), jnp.stack(idxs, axis=-1)


def _route(logits, b_rg, b_re, n_groups, n_experts, moe_block):
    n = logits.shape[0]
    epg = n_experts // n_groups
    p_group = jax.nn.softmax(logits[:, :n_groups] + b_rg, axis=-1)
    g_w, g_idx = _top_k(p_group, 1)
    e_logits = (logits[:, n_groups:n_groups + n_experts] + b_re).reshape(n, n_groups, epg)
    in_group = jnp.take_along_axis(e_logits, g_idx[:, :, None], axis=1)[:, 0]
    top_logit, top_local = _top_k(in_group, TOP_K)
    gate = g_w * jax.nn.softmax(top_logit, axis=-1)
    expert_idx = g_idx * epg + top_local

    a = n * TOP_K
    flat_e = expert_idx.reshape(a).astype(jnp.int32)
    onehot = (flat_e[:, None] == jnp.arange(n_experts, dtype=jnp.int32)[None, :]).astype(jnp.int32)
    rank = jnp.sum((jnp.cumsum(onehot, axis=0) - onehot) * onehot, axis=1)
    counts = jnp.sum(onehot, axis=0)
    padded = (counts + moe_block - 1) // moe_block * moe_block
    pad_end = jnp.cumsum(padded)
    pad_start = pad_end - padded
    dest = pad_start[flat_e] + rank
    n_blocks = (a + n_experts * (moe_block - 1) + moe_block - 1) // moe_block
    n_rows = n_blocks * moe_block
    fill = jnp.stack([jnp.arange(n_rows, dtype=jnp.int32) % n, jnp.zeros((n_rows,), jnp.int32)], axis=1)
    vals = jnp.stack([jnp.arange(a, dtype=jnp.int32) // TOP_K,
                      lax.bitcast_convert_type(gate.reshape(a).astype(F32), jnp.int32)], axis=1)
    packed = fill.at[dest].set(vals)
    row_token = packed[:, 0]
    row_gate = lax.bitcast_convert_type(packed[:, 1], F32)
    n_active = (pad_end[-1] // moe_block).astype(jnp.int32)
    blk = jnp.minimum(jnp.arange(n_blocks, dtype=jnp.int32), n_active - 1) * moe_block
    block_expert = jnp.minimum(jnp.searchsorted(pad_end, blk, side='right'), n_experts - 1).astype(jnp.int32)
    return row_token, row_gate, dest.reshape(n, TOP_K), block_expert, n_active.reshape(1)


def kernel(x_prompt, mem_prompt, x_sample, cache_kv_latent, cache_k_rope, page_table, state_hgrn, cache_mem_k, cache_mem_v, norm_mix, w_in, q_norm, w_uq, kv_norm, w_uk, w_uv, lb_logits, hg_norm, w_out_mla, w_out_hgrn, w_out, norm_cross, norm_mem, w_cq, w_ckv, w_co, norm_ffn, w_router_group, b_router_group, w_router_expert, b_router_expert, w_expert_gate_up, w_expert_down, norm_final):
    batch, seq, d = x_prompt.shape
    dec_batch, dec_seq, _ = x_sample.shape
    depth = w_in.shape[0]
    assert depth == 1
    page = cache_kv_latent.shape[2]
    kv_lora = cache_kv_latent.shape[3]
    rope = cache_k_rope.shape[3]
    half = rope // 2
    past_len = page_table.shape[1] * page
    q_lora = q_norm.shape[1]
    hg_heads, hg_dk = state_hgrn.shape[2], state_hgrn.shape[3]
    hgw = hg_heads * hg_dk
    mla_heads = (w_uq.shape[2] - w_uk.shape[2]) // rope
    nope = w_uk.shape[2] // mla_heads
    v_head = w_uv.shape[2] // mla_heads
    mem_tokens, x_heads, x_dh = cache_mem_k.shape[2], cache_mem_k.shape[3], cache_mem_k.shape[4]
    xw = x_heads * x_dh
    n_groups = w_router_group.shape[2]
    n_experts = w_router_expert.shape[2]
    mp = batch * seq
    ms = dec_batch * dec_seq
    m_all = mp + ms
    mla_scale = (nope + rope) ** -0.5
    assert nope == LANES and 2 * half <= LANES and hg_dk == LANES

    n_front = q_lora + kv_lora + rope
    off_cq, off_ckv, off_kr = 0, q_lora, q_lora + kv_lora
    off_hq, off_hf, off_hi, off_hg = 0, hgw, 2 * hgw, 3 * hgw
    off_ga, off_gb = 4 * hgw, 4 * hgw + d
    w_in_t = jnp.swapaxes(w_in[0], 0, 1)
    w_back = w_in_t[n_front:].astype(BF16)
    w_front = jnp.concatenate([w_in_t[:n_front], jnp.zeros((LANES - rope, d), F32)], axis=0).astype(BF16)

    x0 = jnp.concatenate([x_prompt.reshape(mp, d), x_sample.reshape(ms, d)], axis=0)
    pos = jnp.concatenate([jnp.tile(jnp.arange(seq), batch), jnp.tile(past_len + jnp.arange(dec_seq), dec_batch)])
    cos_t, sin_t = _rope_tables(pos, half)

    tm_in = _tile(m_all, 1152, unit=2 * SUBLANES)
    z = _matmul(x0, w_back, w_is_nk=True, gain=norm_mix[0], tm=tm_in, tn_cap=1024, name="mm_in")
    z_front = _matmul(x0, w_front, w_is_nk=True, gain=norm_mix[0], tm=tm_in, tn_cap=1024, name="mm_in_front")
    lat, kr_pad = _lat_kr(z_front, kv_norm[0], cos_t, sin_t, off_ckv=off_ckv, off_kr=off_kr, kv_lora=kv_lora,
                          half=half)
    w_uq_p = jnp.pad(w_uq[0].reshape(q_lora, mla_heads, nope + rope), ((0, 0), (0, 0), (0, LANES - rope)))
    w_uq_p = w_uq_p.reshape(q_lora, mla_heads * (nope + LANES)).astype(BF16)
    q_all = _q_proj(z_front, w_uq_p, q_norm[0], cos_t, sin_t, off_cq=off_cq, q_lora=q_lora, heads=mla_heads,
                    nope=nope, half=half, scale=mla_scale)
    w_uk_lo, w_uv_lo = w_uk[0].astype(BF16), w_uv[0].astype(BF16)

    k_all, v_all = _kv_proj(lat, w_uk_lo, w_uv_lo, kr_pad, rows=mp, heads=mla_heads, nope=nope)
    a_all = _flash(q_all, k_all, v_all, jnp.zeros((m_all, mla_heads * v_head), BF16), batch=batch, seq=seq,
                   heads=mla_heads, dv=v_head)

    q_lat = _q_lat(q_all, w_uk_lo, row0=mp, rows=ms, nope=nope)
    rows_s = mla_heads * dec_seq
    q_lat = q_lat.reshape(mla_heads, dec_batch, dec_seq, kv_lora).transpose(1, 0, 2, 3).reshape(dec_batch, rows_s, kv_lora)
    q_rope_s = q_all[:, mp:, nope:nope + rope].astype(F32)
    q_rope_s = q_rope_s.reshape(mla_heads, dec_batch, dec_seq, rope).transpose(1, 0, 2, 3).reshape(dec_batch, rows_s, rope)
    lat_s = lat[mp:].reshape(dec_batch, dec_seq, kv_lora)
    kr_s = kr_pad[mp:, :rope].reshape(dec_batch, dec_seq, rope)
    o_lat = _paged_attention(page_table, q_lat, q_rope_s, lat_s, kr_s,
                             cache_kv_latent.reshape(-1, page, kv_lora),
                             jnp.swapaxes(cache_k_rope, 2, 3).reshape(-1, rope, page),
                             t_new=dec_seq)
    o_lat = o_lat.reshape(dec_batch, mla_heads, dec_seq, kv_lora).transpose(1, 0, 2, 3).reshape(mla_heads, ms, kv_lora)
    a_all = _o_uv(o_lat, w_uv_lo, a_all, dv=v_head, row0=mp)

    lower_bounds = jnp.cumsum(jax.nn.softmax(lb_logits.astype(F32), axis=0), axis=0)
    lb = lower_bounds[0]
    chunk_p = math.gcd(HG_CHUNK, seq)
    hg_offs = (off_hq, off_hf, off_hi, off_hg)
    b_all, s_p = _hgrn(z, lb, hg_norm[0], jnp.zeros((m_all, hgw), BF16), offs=hg_offs, n_seq=batch, rows=seq,
                       row_block=_tile(seq, 512, unit=chunk_p), heads_total=hg_heads, heads=math.gcd(8, hg_heads),
                       dk=hg_dk, chunk=chunk_p)
    b_all, s_s = _hgrn_step(z, lb, hg_norm[0], state_hgrn[0], b_all, offs=hg_offs, row0=mp, n_seq_total=dec_batch,
                            t_new=dec_seq, heads=hg_heads, dk=hg_dk)

    merged = _gated_merge(a_all, b_all, z, w_out_mla[0].astype(BF16), w_out_hgrn[0].astype(BF16), off_ga=off_ga,
                          off_gb=off_gb, tm=tm_in)
    x1 = _matmul(merged, w_out[0].astype(BF16), res=x0, tm=tm_in, name="mm_out")

    kv_mem = _matmul(mem_prompt.reshape(batch * mem_tokens, d), w_ckv[0], gain=norm_mem[0], tm=batch * mem_tokens,
                     name="mem_kv")
    mk_p, mv_p = kv_mem[:, :xw].reshape(batch, mem_tokens, xw), kv_mem[:, xw:].reshape(batch, mem_tokens, xw)
    qx = _matmul(x1, w_cq[0], gain=norm_cross[0], name="cross_q")
    tq_x = 512
    ox_p = _cross_attention(qx[:mp].reshape(mp // tq_x, tq_x, xw), mk_p, mv_p, heads=x_heads,
                            groups_per_mem=seq // tq_x)
    rows_pad = -(-dec_seq // SUBLANES) * SUBLANES
    qx_s = jnp.pad(qx[mp:].reshape(dec_batch, dec_seq, xw), ((0, 0), (0, rows_pad - dec_seq), (0, 0)))
    ox_s = _cross_attention(qx_s, cache_mem_k[0].reshape(dec_batch, mem_tokens, xw),
                            cache_mem_v[0].reshape(dec_batch, mem_tokens, xw), heads=x_heads, groups_per_mem=1)
    ox = jnp.concatenate([ox_p.reshape(mp, xw), ox_s[:, :dec_seq].reshape(ms, xw)], axis=0)
    x2 = _matmul(ox, w_co[0], res=x1, name="cross_out")

    n_r = n_groups + n_experts
    w_r = jnp.concatenate([jnp.swapaxes(w_router_group[0], 0, 1), jnp.swapaxes(w_router_expert[0], 0, 1),
                           jnp.zeros((-n_r % LANES, d), F32)], axis=0)
    logits = _matmul(x2, w_r, w_is_nk=True, gain=norm_ffn[0], name="router")
    moe_block = 256

    def route(lg):
        return _route(lg, b_router_group[0], b_router_expert[0], n_groups, n_experts, moe_block)

    row_token, row_gate, dest, block_expert, n_active = route(logits)
    x_sorted = x2[row_token]
    y_sorted = _moe_ffn(block_expert, n_active, x_sorted, row_gate[:, None], norm_ffn[0],
                        w_expert_gate_up[0], w_expert_down[0], block=moe_block)
    y = _final_norm(x2, y_sorted[dest[:, 0]], y_sorted[dest[:, 1]], norm_final)

    y_prompt = y[:mp].reshape(batch, seq, d)
    y_sample = y[mp:].reshape(dec_batch, dec_seq, d)
    n_prompt_pages = mp // page
    kv_latent_prompt = lat[:mp].reshape(depth, n_prompt_pages, page, kv_lora)
    k_rope_prompt = kr_pad[:mp, :rope].reshape(depth, n_prompt_pages, page, rope)
    state_hgrn_prompt = s_p[None]
    mem_k_prompt = mk_p.reshape(depth, batch, mem_tokens, x_heads, x_dh)
    mem_v_prompt = mv_p.reshape(depth, batch, mem_tokens, x_heads, x_dh)
    kv_latent_sample = lat_s[None]
    k_rope_sample = kr_s[None]
    state_hgrn_sample = s_s[None]
    return (y_prompt, y_sample, kv_latent_prompt, k_rope_prompt, state_hgrn_prompt, mem_k_prompt, mem_v_prompt,
            kv_latent_sample, k_rope_sample, state_hgrn_sample)
```

```python
import functools
import math

import jax
import jax.numpy as jnp
from jax import lax
from jax.experimental import pallas as pl
from jax.experimental.pallas import tpu as pltpu

F32 = jnp.float32
BF16 = jnp.bfloat16

EPS = 1e-6
NEG = -1e30
ROPE_THETA = 10000.0
TOP_K = 2
HG_CHUNK = 64
LANES = 128
SUBLANES = 8
VMEM_LIMIT = 56 * 1024 * 1024
PAGE_GROUP = 8
PAGE_SLOTS = 4


def _params(n_axes):
    return pltpu.CompilerParams(dimension_semantics=("arbitrary",) * n_axes, vmem_limit_bytes=VMEM_LIMIT)


def _tile(n, cap, unit=LANES):
    if n <= cap:
        return n
    t = cap - cap % unit
    while t >= unit:
        if n % t == 0:
            return t
        t -= unit
    raise ValueError(f"no tile for {n} under {cap}")


def _rms(x, g):
    return x * lax.rsqrt(jnp.mean(x * x, axis=-1, keepdims=True) + EPS) * g


def _dot(a, b):
    return jnp.dot(a, b, preferred_element_type=F32)


def _dot_nt(a, b):
    return lax.dot_general(a, b, (((1,), (1,)), ((), ())), preferred_element_type=F32)


def _dot_tn(a, b):
    return lax.dot_general(a, b, (((0,), (0,)), ((), ())), preferred_element_type=F32)


def _mm_kernel(*refs, has_gain, has_res, w_is_nk):
    it = iter(refs)
    x_ref, w_ref = next(it), next(it)
    g_ref = next(it) if has_gain else None
    r_ref = next(it) if has_res else None
    o_ref = next(it)
    if has_gain:
        xs_ref = next(it)

        @pl.when(pl.program_id(1) == 0)
        def _():
            xs_ref[...] = _rms(x_ref[...].astype(F32), g_ref[...]).astype(xs_ref.dtype)

        x = xs_ref[...]
    else:
        x = x_ref[...].astype(w_ref.dtype)
    acc = _dot_nt(x, w_ref[...]) if w_is_nk else _dot(x, w_ref[...])
    if has_res:
        acc = r_ref[...] + acc
    o_ref[...] = acc.astype(o_ref.dtype)


def _matmul(x, w, *, w_is_nk=False, gain=None, res=None, x_col=0, tm=512, tn_cap=512, out_dtype=F32, name="mm"):
    m = x.shape[0]
    n, k = w.shape if w_is_nk else w.shape[::-1]
    assert x_col % k == 0 and m % tm == 0
    tn = _tile(n, tn_cap)
    xb = x_col // k
    w_spec = pl.BlockSpec((tn, k), lambda i, j: (j, 0)) if w_is_nk else pl.BlockSpec((k, tn), lambda i, j: (0, j))
    in_specs = [pl.BlockSpec((tm, k), lambda i, j: (i, xb)), w_spec]
    args = [x, w]
    scratch = []
    if gain is not None:
        in_specs.append(pl.BlockSpec((1, k), lambda i, j: (0, 0)))
        args.append(gain.reshape(1, k).astype(F32))
        scratch.append(pltpu.VMEM((tm, k), w.dtype))
    if res is not None:
        in_specs.append(pl.BlockSpec((tm, tn), lambda i, j: (i, j)))
        args.append(res)
    return pl.pallas_call(
        functools.partial(_mm_kernel, has_gain=gain is not None, has_res=res is not None, w_is_nk=w_is_nk),
        grid=(m // tm, n // tn),
        in_specs=in_specs,
        out_specs=pl.BlockSpec((tm, tn), lambda i, j: (i, j)),
        out_shape=jax.ShapeDtypeStruct((m, n), out_dtype),
        scratch_shapes=scratch,
        compiler_params=_params(2),
        name=name,
    )(*args)


def _rope_tile(x, cos_t, sin_t, half):
    lane = lax.broadcasted_iota(jnp.int32, x.shape, 1)
    swapped = jnp.where(lane < half, pltpu.roll(x, LANES - half, 1), pltpu.roll(x, half, 1))
    return x * cos_t + swapped * sin_t


def _rope_tables(pos, half):
    inv = ROPE_THETA ** (-jnp.arange(half, dtype=F32) / half)
    ang = pos.astype(F32)[:, None] * inv[None, :]
    cos, sin = jnp.cos(ang), jnp.sin(ang)
    zeros = jnp.zeros((pos.shape[0], LANES - 2 * half), F32)
    return jnp.concatenate([cos, cos, zeros], axis=1), jnp.concatenate([-sin, sin, zeros], axis=1)


def _latkr_kernel(ckv_ref, kr_ref, g_ref, cos_ref, sin_ref, lat_ref, kro_ref, *, half):
    lat_ref[...] = _rms(ckv_ref[...], g_ref[...])
    kro_ref[...] = _rope_tile(kr_ref[...], cos_ref[...], sin_ref[...], half)


def _lat_kr(z, kv_norm, cos_t, sin_t, *, off_ckv, off_kr, kv_lora, half, tm=512):
    m = z.shape[0]
    return pl.pallas_call(
        functools.partial(_latkr_kernel, half=half),
        grid=(m // tm,),
        in_specs=[
            pl.BlockSpec((tm, kv_lora), lambda i: (i, off_ckv // kv_lora)),
            pl.BlockSpec((tm, LANES), lambda i: (i, off_kr // LANES)),
            pl.BlockSpec((1, kv_lora), lambda i: (0, 0)),
            pl.BlockSpec((tm, LANES), lambda i: (i, 0)),
            pl.BlockSpec((tm, LANES), lambda i: (i, 0)),
        ],
        out_specs=[pl.BlockSpec((tm, kv_lora), lambda i: (i, 0)), pl.BlockSpec((tm, LANES), lambda i: (i, 0))],
        out_shape=[jax.ShapeDtypeStruct((m, kv_lora), F32), jax.ShapeDtypeStruct((m, LANES), F32)],
        compiler_params=_params(1),
        name="lat_kr",
    )(z, z, kv_norm.reshape(1, kv_lora), cos_t, sin_t)


def _qproj_kernel(cq_ref, w_ref, g_ref, cos_ref, sin_ref, o_ref, os_ref, *, heads, nope, half, scale, sample_tile0):
    width = nope + LANES
    q = _dot(_rms(cq_ref[...], g_ref[...]).astype(w_ref.dtype), w_ref[...]) * scale
    cos_t, sin_t = cos_ref[...], sin_ref[...]
    parts = []
    for h in range(heads):
        q_nope = q[:, h * width:h * width + nope].astype(o_ref.dtype)
        q_rope = _rope_tile(q[:, h * width + nope:(h + 1) * width], cos_t, sin_t, half).astype(o_ref.dtype)
        parts.append((q_nope, q_rope))
        o_ref[h, :, :nope] = q_nope
        o_ref[h, :, nope:] = q_rope

    @pl.when(pl.program_id(0) >= sample_tile0)
    def _():
        for h, (q_nope, q_rope) in enumerate(parts):
            os_ref[:, h * width:h * width + nope] = q_nope
            os_ref[:, h * width + nope:(h + 1) * width] = q_rope


def _q_proj(z, w_uq_p, q_norm, cos_t, sin_t, *, off_cq, q_lora, heads, nope, half, scale, sample_row0, tm=512):
    m = z.shape[0]
    width = nope + LANES
    assert sample_row0 % tm == 0
    tile0 = sample_row0 // tm
    return pl.pallas_call(
        functools.partial(_qproj_kernel, heads=heads, nope=nope, half=half, scale=scale, sample_tile0=tile0),
        grid=(m // tm,),
        in_specs=[
            pl.BlockSpec((tm, q_lora), lambda i: (i, off_cq // q_lora)),
            pl.BlockSpec((q_lora, heads * width), lambda i: (0, 0)),
            pl.BlockSpec((1, q_lora), lambda i: (0, 0)),
            pl.BlockSpec((tm, LANES), lambda i: (i, 0)),
            pl.BlockSpec((tm, LANES), lambda i: (i, 0)),
        ],
        out_specs=[
            pl.BlockSpec((heads, tm, width), lambda i: (0, i, 0)),
            pl.BlockSpec((tm, heads * width), lambda i: (jnp.maximum(i - tile0, 0), 0)),
        ],
        out_shape=[
            jax.ShapeDtypeStruct((heads, m, width), BF16),
            jax.ShapeDtypeStruct((m - sample_row0, heads * width), BF16),
        ],
        compiler_params=_params(1),
        name="q_proj",
    )(z, w_uq_p, q_norm.reshape(1, q_lora), cos_t, sin_t)


def _kvproj_kernel(lat_ref, wk_ref, wv_ref, kr_ref, k_ref, v_ref, *, heads, nope):
    lat = lat_ref[...].astype(wk_ref.dtype)
    k = _dot(lat, wk_ref[...])
    kr = kr_ref[...].astype(k_ref.dtype)
    for h in range(heads):
        k_ref[h, :, :nope] = k[:, h * nope:(h + 1) * nope].astype(k_ref.dtype)
        k_ref[h, :, nope:] = kr
    v_ref[...] = _dot(lat, wv_ref[...]).astype(v_ref.dtype)


def _kv_proj(lat, w_uk, w_uv, kr_pad, *, rows, heads, nope, tm=512):
    kv_lora = lat.shape[1]
    nv = w_uv.shape[1]
    return pl.pallas_call(
        functools.partial(_kvproj_kernel, heads=heads, nope=nope),
        grid=(rows // tm,),
        in_specs=[
            pl.BlockSpec((tm, kv_lora), lambda i: (i, 0)),
            pl.BlockSpec((kv_lora, heads * nope), lambda i: (0, 0)),
            pl.BlockSpec((kv_lora, nv), lambda i: (0, 0)),
            pl.BlockSpec((tm, LANES), lambda i: (i, 0)),
        ],
        out_specs=[
            pl.BlockSpec((heads, tm, nope + LANES), lambda i: (0, i, 0)),
            pl.BlockSpec((tm, nv), lambda i: (i, 0)),
        ],
        out_shape=[
            jax.ShapeDtypeStruct((heads, rows, nope + LANES), BF16),
            jax.ShapeDtypeStruct((rows, nv), BF16),
        ],
        compiler_params=_params(1),
        name="kv_proj",
    )(lat, w_uk, w_uv, kr_pad)


def _flash_kernel(q_ref, k_ref, v_ref, buf_ref, o_ref, *, tq, tk):
    del buf_ref
    qi = pl.program_id(2)
    q = q_ref[0]
    dv = v_ref.shape[1]

    def step(kb, carry, diag):
        m, l, acc = carry
        start = pl.multiple_of(kb * tk, tk)
        k = k_ref[0, pl.ds(start, tk), :]
        v = v_ref[pl.ds(start, tk), :]
        s = _dot_nt(q, k)
        if diag is not None:
            row = lax.broadcasted_iota(jnp.int32, s.shape, 0)
            col = lax.broadcasted_iota(jnp.int32, s.shape, 1) + diag * tk
            s = jnp.where(col <= row, s, NEG)
        m_new = jnp.maximum(m, jnp.max(s, axis=-1, keepdims=True))
        alpha = jnp.exp(m - m_new)
        p = jnp.exp(s - m_new)
        l = l * alpha + jnp.sum(p, axis=-1, keepdims=True)
        acc = acc * alpha + _dot(p.astype(v.dtype), v)
        return m_new, l, acc

    n_full = qi * (tq // tk)
    carry = (jnp.full((tq, 1), NEG, F32), jnp.zeros((tq, 1), F32), jnp.zeros((tq, dv), F32))
    carry = lax.fori_loop(0, n_full, functools.partial(step, diag=None), carry)
    for j in range(tq // tk):
        carry = step(n_full + j, carry, j)
    _, l, acc = carry
    o_ref[...] = (acc / l).astype(o_ref.dtype)


def _flash(q_all, k_all, v_all, buf, *, batch, seq, heads, dv, tq=1024, tk=512):
    nq = seq // tq
    width = q_all.shape[2]
    assert tq % tk == 0
    return pl.pallas_call(
        functools.partial(_flash_kernel, tq=tq, tk=tk),
        grid=(batch, heads, nq),
        in_specs=[
            pl.BlockSpec((1, tq, width), lambda b, h, i: (h, b * nq + i, 0)),
            pl.BlockSpec((1, seq, width), lambda b, h, i: (h, b, 0)),
            pl.BlockSpec((seq, dv), lambda b, h, i: (b, h)),
            pl.BlockSpec(memory_space=pl.ANY),
        ],
        out_specs=pl.BlockSpec((tq, dv), lambda b, h, i: (b * nq + i, h)),
        out_shape=jax.ShapeDtypeStruct(buf.shape, buf.dtype),
        input_output_aliases={3: 0},
        compiler_params=_params(3),
        name="flash",
    )(q_all, k_all, v_all, buf)


def _qlat_kernel(q_ref, w_ref, o_ref, *, nope):
    o_ref[...] = _dot_nt(q_ref[:, :nope], w_ref[...])


def _q_lat(q_tok, w_uk, *, heads, nope):
    rows = q_tok.shape[0]
    width = q_tok.shape[1] // heads
    kv_lora = w_uk.shape[0]
    return pl.pallas_call(
        functools.partial(_qlat_kernel, nope=nope),
        grid=(heads,),
        in_specs=[
            pl.BlockSpec((rows, width), lambda h: (0, h)),
            pl.BlockSpec((kv_lora, nope), lambda h: (0, h)),
        ],
        out_specs=pl.BlockSpec((rows, kv_lora), lambda h: (0, h)),
        out_shape=jax.ShapeDtypeStruct((rows, heads * kv_lora), F32),
        compiler_params=_params(1),
        name="q_lat",
    )(q_tok, w_uk)


def _ouv_kernel(x_ref, w_ref, buf_ref, o_ref):
    del buf_ref
    o_ref[...] = _dot(x_ref[...].astype(w_ref.dtype), w_ref[...]).astype(o_ref.dtype)


def _o_uv(o_lat, w_uv, buf, *, heads, dv, row0):
    rows = o_lat.shape[0]
    kv_lora = o_lat.shape[1] // heads
    assert row0 % rows == 0
    return pl.pallas_call(
        _ouv_kernel,
        grid=(heads,),
        in_specs=[
            pl.BlockSpec((rows, kv_lora), lambda h: (0, h)),
            pl.BlockSpec((kv_lora, dv), lambda h: (0, h)),
            pl.BlockSpec(memory_space=pl.ANY),
        ],
        out_specs=pl.BlockSpec((rows, dv), lambda h: (row0 // rows, h)),
        out_shape=jax.ShapeDtypeStruct(buf.shape, buf.dtype),
        input_output_aliases={2: 0},
        compiler_params=_params(1),
        name="o_uv",
    )(o_lat, w_uv, buf)


def _paged_kernel(pt_ref, ql_ref, qr_ref, tok_ref, nl_ref, nr_ref, cl_hbm, cr_hbm, o_ref, bl_ref, br_ref, sem_ref,
                  *, n_seq, n_groups, group, page, slots, t_new):
    b = pl.program_id(0)
    total = n_seq * n_groups
    ahead = slots - 1

    def copies(t):
        seq = lax.div(t, n_groups)
        g = lax.rem(t, n_groups)
        slot = lax.rem(t, slots)
        out = []
        for i in range(group):
            pid = pt_ref[seq, g * group + i]
            keys = pl.ds(i * page, page)
            out.append(pltpu.make_async_copy(cl_hbm.at[pid], bl_ref.at[slot, keys], sem_ref.at[0, slot, i]))
            out.append(pltpu.make_async_copy(cr_hbm.at[pid], br_ref.at[slot, i], sem_ref.at[1, slot, i]))
        return out

    @pl.when(b == 0)
    def _():
        for t0 in range(min(ahead, total)):
            for c in copies(jnp.int32(t0)):
                c.start()

    ql = ql_ref[0]
    qr = qr_ref[0]
    ql_lo = ql.astype(BF16)
    qr_lo = qr.astype(BF16)
    rows = ql.shape[0]
    kv = ql.shape[1]

    def body(g, carry):
        m, l, acc = carry
        t = b * n_groups + g

        @pl.when(t + ahead < total)
        def _():
            for c in copies(t + ahead):
                c.start()

        for c in copies(t):
            c.wait()
        slot = lax.rem(t, slots)
        c_blk = bl_ref[slot].astype(BF16)
        s_rope = jnp.concatenate([_dot(qr_lo, br_ref[slot, i].astype(BF16)) for i in range(group)], axis=1)
        s = _dot_nt(ql_lo, c_blk) + s_rope
        m_new = jnp.maximum(m, jnp.max(s, axis=-1, keepdims=True))
        alpha = jnp.exp(m - m_new)
        p = jnp.exp(s - m_new)
        l = l * alpha + jnp.sum(p, axis=-1, keepdims=True)
        acc = acc * alpha + _dot(p.astype(BF16), c_blk)
        return m_new, l, acc

    init = (jnp.full((rows, 1), NEG, F32), jnp.zeros((rows, 1), F32), jnp.zeros((rows, kv), F32))
    m, l, acc = lax.fori_loop(0, n_groups, body, init)

    t_of_row = tok_ref[...]
    s_new = []
    for j in range(t_new):
        sj = (jnp.sum(ql * nl_ref[0, j:j + 1, :], axis=-1, keepdims=True)
              + jnp.sum(qr * nr_ref[0, j:j + 1, :], axis=-1, keepdims=True))
        s_new.append(jnp.where(t_of_row >= j, sj, NEG))
    m_new = m
    for sj in s_new:
        m_new = jnp.maximum(m_new, sj)
    alpha = jnp.exp(m - m_new)
    l = l * alpha
    acc = acc * alpha
    for j, sj in enumerate(s_new):
        pj = jnp.exp(sj - m_new)
        l = l + pj
        acc = acc + pj * nl_ref[0, j:j + 1, :]
    o_ref[0] = acc / l


def _paged_attention(page_table, q_lat, q_rope, new_lat, new_kr, cache_lat, cache_kr_t, *, t_new):
    nb, rows, kv = q_lat.shape
    rope = q_rope.shape[2]
    n_pages = page_table.shape[1]
    page = cache_lat.shape[1]
    group = math.gcd(PAGE_GROUP, n_pages)
    n_groups = n_pages // group
    slots = PAGE_SLOTS
    grid_spec = pltpu.PrefetchScalarGridSpec(
        num_scalar_prefetch=1,
        grid=(nb,),
        in_specs=[
            pl.BlockSpec((1, rows, kv), lambda b, pt: (b, 0, 0)),
            pl.BlockSpec((1, rows, rope), lambda b, pt: (b, 0, 0)),
            pl.BlockSpec((rows, 1), lambda b, pt: (0, 0)),
            pl.BlockSpec((1, t_new, kv), lambda b, pt: (b, 0, 0)),
            pl.BlockSpec((1, t_new, rope), lambda b, pt: (b, 0, 0)),
            pl.BlockSpec(memory_space=pl.ANY),
            pl.BlockSpec(memory_space=pl.ANY),
        ],
        out_specs=pl.BlockSpec((1, rows, kv), lambda b, pt: (b, 0, 0)),
        scratch_shapes=[
            pltpu.VMEM((slots, group * page, kv), F32),
            pltpu.VMEM((slots, group, rope, page), F32),
            pltpu.SemaphoreType.DMA((2, slots, group)),
        ],
    )
    tok = (jnp.arange(rows, dtype=jnp.int32) // (rows // t_new)).reshape(rows, 1)
    return pl.pallas_call(
        functools.partial(_paged_kernel, n_seq=nb, n_groups=n_groups, group=group, page=page, slots=slots,
                          t_new=t_new),
        grid_spec=grid_spec,
        out_shape=jax.ShapeDtypeStruct((nb, rows, kv), F32),
        compiler_params=_params(1),
        name="paged_attn",
    )(page_table, q_lat, q_rope, tok, new_lat, new_kr, cache_lat, cache_kr_t)


def _pad_rows(x, rows):
    if x.shape[0] == rows:
        return x
    return jnp.concatenate([x, jnp.zeros((rows - x.shape[0], x.shape[1]), x.dtype)], axis=0)


def _hgrn_kernel(hq_ref, hf_ref, hi_ref, hg_ref, lb_ref, gn_ref, buf_ref, o_ref, so_ref, st_ref,
                 *, heads, dk, chunk, n_chunks):
    del buf_ref
    @pl.when(pl.program_id(2) == 0)
    def _():
        st_ref[...] = jnp.zeros_like(st_ref)

    keys = max(chunk, LANES)
    row = lax.broadcasted_iota(jnp.int32, (chunk, keys), 0)
    col = lax.broadcasted_iota(jnp.int32, (chunk, keys), 1)
    causal = col <= row
    row_id = lax.broadcasted_iota(jnp.int32, (chunk, dk), 0)

    def body(c, carry):
        r0 = pl.multiple_of(c * chunk, chunk)
        for h in range(heads):
            cols = slice(h * dk, (h + 1) * dk)
            lb = lb_ref[:, cols]
            q = jax.nn.silu(hq_ref[pl.ds(r0, chunk), cols]) * dk ** -0.5
            f = lb + (1.0 - lb) * jax.nn.sigmoid(hf_ref[pl.ds(r0, chunk), cols])
            logf = jnp.log(f)
            k = 1.0 - f
            v = hi_ref[pl.ds(r0, chunk), cols]
            bsum = logf
            shift = 1
            while shift < chunk:
                bsum = bsum + jnp.where(row_id >= shift, pltpu.roll(bsum, shift, 0), 0.0)
                shift *= 2
            b_last = bsum[chunk - 1:chunk, :]
            q_dec = q * jnp.exp(bsum)
            k_inv = k * jnp.exp(-bsum)
            v_pad = _pad_rows(v, keys)
            att = jnp.where(causal, _dot_nt(q_dec, _pad_rows(k_inv, keys)), 0.0)
            st = st_ref[h]
            o = _dot(att, v_pad) + _dot_nt(q_dec, st)
            k_end = k * jnp.exp(b_last - bsum)
            st_ref[h] = st * jnp.exp(b_last) + _dot_tn(v_pad, _pad_rows(k_end, keys))
            o = _rms(o, gn_ref[...]) * jax.nn.silu(hg_ref[pl.ds(r0, chunk), cols])
            o_ref[pl.ds(r0, chunk), cols] = o.astype(o_ref.dtype)
        return carry

    lax.fori_loop(0, n_chunks, body, 0)

    @pl.when(pl.program_id(2) == pl.num_programs(2) - 1)
    def _():
        for h in range(heads):
            so_ref[0, h] = st_ref[h].T


def _hgrn(z, lb, hg_norm, buf, *, offs, n_seq, rows, row_block, heads_total, heads, dk, chunk):
    hw = heads * dk
    n_hb = heads_total // heads
    n_rb = rows // row_block
    n_chunks = row_block // chunk

    def zspec(off):
        return pl.BlockSpec((row_block, hw), lambda s, j, r: (s * n_rb + r, off // hw + j))

    return pl.pallas_call(
        functools.partial(_hgrn_kernel, heads=heads, dk=dk, chunk=chunk, n_chunks=n_chunks),
        grid=(n_seq, n_hb, n_rb),
        in_specs=[zspec(o) for o in offs] + [
            pl.BlockSpec((1, hw), lambda s, j, r: (0, j)),
            pl.BlockSpec((1, dk), lambda s, j, r: (0, 0)),
            pl.BlockSpec(memory_space=pl.ANY),
        ],
        out_specs=[
            pl.BlockSpec((row_block, hw), lambda s, j, r: (s * n_rb + r, j)),
            pl.BlockSpec((1, heads, dk, dk), lambda s, j, r: (s, j, 0, 0)),
        ],
        out_shape=[
            jax.ShapeDtypeStruct(buf.shape, buf.dtype),
            jax.ShapeDtypeStruct((n_seq, heads_total, dk, dk), F32),
        ],
        input_output_aliases={6: 0},
        scratch_shapes=[pltpu.VMEM((heads, dk, dk), F32)],
        compiler_params=_params(3),
        name="hgrn",
    )(z, z, z, z, lb.reshape(1, -1), hg_norm.reshape(1, dk), buf)


def _hgrn_step_kernel(hq_ref, hf_ref, hi_ref, hg_ref, lb_ref, gn_ref, s0_ref, buf_ref, o_ref, so_ref,
                      *, heads, dk, n_seq, t_new):
    del buf_ref
    rows = n_seq * t_new
    for h in range(heads):
        cols = slice(h * dk, (h + 1) * dk)
        lb = lb_ref[:, cols]
        q = jax.nn.silu(hq_ref[:, cols]) * dk ** -0.5
        f = lb + (1.0 - lb) * jax.nn.sigmoid(hf_ref[:, cols])
        v = hi_ref[:, cols]
        row_id = lax.broadcasted_iota(jnp.int32, (rows, dk), 0)
        o = jnp.zeros((rows, dk), F32)
        for s in range(n_seq):
            st = s0_ref[s, h]
            for t in range(t_new):
                r = s * t_new + t
                f_col = jnp.broadcast_to(f[r:r + 1, :], (dk, dk)).T
                q_col = jnp.broadcast_to(q[r:r + 1, :], (dk, dk)).T
                st = st * f_col + (1.0 - f_col) * v[r:r + 1, :]
                o = jnp.where(row_id == r, jnp.sum(st * q_col, axis=0, keepdims=True), o)
            so_ref[s, h] = st
        o_ref[:, cols] = (_rms(o, gn_ref[...]) * jax.nn.silu(hg_ref[:, cols])).astype(o_ref.dtype)


def _hgrn_step(z, lb, hg_norm, s0, buf, *, offs, row0, n_seq_total, t_new, heads, dk):
    n_seq = 2 * SUBLANES // t_new
    rows = n_seq * t_new
    hw = heads * dk
    assert rows == 2 * SUBLANES and row0 % rows == 0 and n_seq_total % n_seq == 0
    rb0 = row0 // rows

    def zspec(off):
        return pl.BlockSpec((rows, hw), lambda i: (rb0 + i, off // hw))

    return pl.pallas_call(
        functools.partial(_hgrn_step_kernel, heads=heads, dk=dk, n_seq=n_seq, t_new=t_new),
        grid=(n_seq_total // n_seq,),
        in_specs=[zspec(o) for o in offs] + [
            pl.BlockSpec((1, hw), lambda i: (0, 0)),
            pl.BlockSpec((1, dk), lambda i: (0, 0)),
            pl.BlockSpec((n_seq, heads, dk, dk), lambda i: (i, 0, 0, 0)),
            pl.BlockSpec(memory_space=pl.ANY),
        ],
        out_specs=[
            pl.BlockSpec((rows, hw), lambda i: (rb0 + i, 0)),
            pl.BlockSpec((n_seq, heads, dk, dk), lambda i: (i, 0, 0, 0)),
        ],
        out_shape=[
            jax.ShapeDtypeStruct(buf.shape, buf.dtype),
            jax.ShapeDtypeStruct(s0.shape, F32),
        ],
        input_output_aliases={7: 0},
        compiler_params=_params(1),
        name="hgrn_step",
    )(z, z, z, z, lb.reshape(1, -1), hg_norm.reshape(1, dk), s0, buf)


def _gated_kernel(a_ref, b_ref, ga_ref, gb_ref, wa_ref, wb_ref, o_ref):
    o_ref[...] = (jax.nn.sigmoid(ga_ref[...]) * _dot(a_ref[...], wa_ref[...])
                  + jax.nn.sigmoid(gb_ref[...]) * _dot(b_ref[...], wb_ref[...])).astype(o_ref.dtype)


def _gated_merge(a, b, z, w_a, w_b, *, off_ga, off_gb, tm=512, tn=512):
    m, ka = a.shape
    kb = b.shape[1]
    n = w_a.shape[1]
    tn = _tile(n, tn)
    return pl.pallas_call(
        _gated_kernel,
        grid=(m // tm, n // tn),
        in_specs=[
            pl.BlockSpec((tm, ka), lambda i, j: (i, 0)),
            pl.BlockSpec((tm, kb), lambda i, j: (i, 0)),
            pl.BlockSpec((tm, tn), lambda i, j: (i, off_ga // tn + j)),
            pl.BlockSpec((tm, tn), lambda i, j: (i, off_gb // tn + j)),
            pl.BlockSpec((ka, tn), lambda i, j: (0, j)),
            pl.BlockSpec((kb, tn), lambda i, j: (0, j)),
        ],
        out_specs=pl.BlockSpec((tm, tn), lambda i, j: (i, j)),
        out_shape=jax.ShapeDtypeStruct((m, n), a.dtype),
        compiler_params=_params(2),
        name="gated_merge",
    )(a, b, z, z, w_a, w_b)


def _cross_kernel(q_ref, k_ref, v_ref, o_ref, *, heads, dh):
    for h in range(heads):
        cols = slice(h * dh, (h + 1) * dh)
        s = _dot_nt(q_ref[0, :, cols], k_ref[0, :, cols]) * dh ** -0.5
        p = jnp.exp(s - jnp.max(s, axis=-1, keepdims=True))
        p = p / jnp.sum(p, axis=-1, keepdims=True)
        o_ref[0, :, cols] = _dot(p, v_ref[0, :, cols])


def _cross_attention(q, k, v, *, heads, groups_per_mem):
    n, r, w = q.shape
    mem = k.shape[1]
    return pl.pallas_call(
        functools.partial(_cross_kernel, heads=heads, dh=w // heads),
        grid=(n,),
        in_specs=[
            pl.BlockSpec((1, r, w), lambda i: (i, 0, 0)),
            pl.BlockSpec((1, mem, w), lambda i: (i // groups_per_mem, 0, 0)),
            pl.BlockSpec((1, mem, w), lambda i: (i // groups_per_mem, 0, 0)),
        ],
        out_specs=pl.BlockSpec((1, r, w), lambda i: (i, 0, 0)),
        out_shape=jax.ShapeDtypeStruct((n, r, w), F32),
        compiler_params=_params(1),
        name="cross_attn",
    )(q, k, v)


def _cross_step_kernel(q_ref, k_ref, v_ref, o_ref, *, n_seq, heads, dh):
    rows = q_ref.shape[1]
    keys = k_ref.shape[1]
    row = lax.broadcasted_iota(jnp.int32, (heads * rows, keys), 0)
    col_head = jnp.bitwise_and(lax.broadcasted_iota(jnp.int32, (heads * rows, keys), 1), heads - 1)
    own = row < 0
    for h in range(heads):
        own = own | ((row >= h * rows) & (row < (h + 1) * rows) & (col_head == h))
    for s in range(n_seq):
        q = jnp.concatenate([q_ref[s, :, h * dh:(h + 1) * dh] for h in range(heads)], axis=0)
        sc = jnp.where(own, _dot_nt(q, k_ref[s]) * dh ** -0.5, NEG)
        p = jnp.exp(sc - jnp.max(sc, axis=-1, keepdims=True))
        p = p / jnp.sum(p, axis=-1, keepdims=True)
        o = _dot(p, v_ref[s])
        for h in range(heads):
            o_ref[s, :, h * dh:(h + 1) * dh] = o[h * rows:(h + 1) * rows, :]


def _cross_attention_step(q, k, v, *, heads, n_seq=4):
    b, rows, w = q.shape
    keys, dh = k.shape[1], k.shape[2]
    assert heads & (heads - 1) == 0 and b % n_seq == 0
    return pl.pallas_call(
        functools.partial(_cross_step_kernel, n_seq=n_seq, heads=heads, dh=dh),
        grid=(b // n_seq,),
        in_specs=[
            pl.BlockSpec((n_seq, rows, w), lambda i: (i, 0, 0)),
            pl.BlockSpec((n_seq, keys, dh), lambda i: (i, 0, 0)),
            pl.BlockSpec((n_seq, keys, dh), lambda i: (i, 0, 0)),
        ],
        out_specs=pl.BlockSpec((n_seq, rows, w), lambda i: (i, 0, 0)),
        out_shape=jax.ShapeDtypeStruct((b, rows, w), F32),
        compiler_params=_params(1),
        name="cross_attn_step",
    )(q, k, v)


def _router_kernel(x_ref, w_ref, g_ref, lg_ref, h_ref):
    h = _rms(x_ref[...], g_ref[...])
    lg_ref[...] = _dot_nt(h, w_ref[...])
    h_ref[...] = h.astype(h_ref.dtype)


def _router(x, w_nk, gain, *, tm=512):
    m, d = x.shape
    n = w_nk.shape[0]
    return pl.pallas_call(
        _router_kernel,
        grid=(m // tm,),
        in_specs=[
            pl.BlockSpec((tm, d), lambda i: (i, 0)),
            pl.BlockSpec((n, d), lambda i: (0, 0)),
            pl.BlockSpec((1, d), lambda i: (0, 0)),
        ],
        out_specs=[pl.BlockSpec((tm, n), lambda i: (i, 0)), pl.BlockSpec((tm, d), lambda i: (i, 0))],
        out_shape=[jax.ShapeDtypeStruct((m, n), F32), jax.ShapeDtypeStruct((m, d), BF16)],
        compiler_params=_params(1),
        name="router",
    )(x, w_nk, gain.reshape(1, d))


def _moe_kernel(be_ref, na_ref, h_ref, gate_ref, wgu_ref, wdn_ref, o_ref, *, d_expert):
    @pl.when(pl.program_id(0) < na_ref[0])
    def _():
        gu = _dot(h_ref[...], wgu_ref[0].astype(h_ref.dtype))
        act = jax.nn.silu(gu[:, :d_expert]) * gu[:, d_expert:]
        o_ref[...] = _dot(act.astype(h_ref.dtype), wdn_ref[0].astype(h_ref.dtype)) * gate_ref[...]

    @pl.when(pl.program_id(0) >= na_ref[0])
    def _():
        o_ref[...] = jnp.zeros_like(o_ref)


def _moe_ffn(block_expert, n_active, h_sorted, row_gate, w_gu, w_dn, *, block):
    n_rows, d = h_sorted.shape
    n_blocks = n_rows // block
    d_expert = w_dn.shape[1]

    def row_block(i, be, na):
        return (jnp.minimum(i, na[0] - 1), 0)

    grid_spec = pltpu.PrefetchScalarGridSpec(
        num_scalar_prefetch=2,
        grid=(n_blocks,),
        in_specs=[
            pl.BlockSpec((block, d), row_block),
            pl.BlockSpec((block, 1), row_block),
            pl.BlockSpec((1, d, 2 * d_expert), lambda i, be, na: (be[i], 0, 0)),
            pl.BlockSpec((1, d_expert, d), lambda i, be, na: (be[i], 0, 0)),
        ],
        out_specs=pl.BlockSpec((block, d), lambda i, be, na: (i, 0)),
    )
    return pl.pallas_call(
        functools.partial(_moe_kernel, d_expert=d_expert),
        grid_spec=grid_spec,
        out_shape=jax.ShapeDtypeStruct((n_rows, d), F32),
        compiler_params=_params(1),
        name="moe_ffn",
    )(block_expert, n_active, h_sorted, row_gate, w_gu, w_dn)


def _final_kernel(x_ref, ya_ref, yb_ref, g_ref, op_ref, os_ref, *, sample_tile0):
    y = _rms(x_ref[...] + (ya_ref[...] + yb_ref[...]), g_ref[...])

    @pl.when(pl.program_id(0) < sample_tile0)
    def _():
        op_ref[...] = y

    @pl.when(pl.program_id(0) >= sample_tile0)
    def _():
        os_ref[...] = y


def _final_norm(x, ya, yb, g, *, sample_row0, tm=512):
    m, d = x.shape
    assert sample_row0 % tm == 0 and m % tm == 0
    tile0 = sample_row0 // tm
    spec = pl.BlockSpec((tm, d), lambda i: (i, 0))
    return pl.pallas_call(
        functools.partial(_final_kernel, sample_tile0=tile0),
        grid=(m // tm,),
        in_specs=[spec, spec, spec, pl.BlockSpec((1, d), lambda i: (0, 0))],
        out_specs=[
            pl.BlockSpec((tm, d), lambda i: (jnp.minimum(i, tile0 - 1), 0)),
            pl.BlockSpec((tm, d), lambda i: (jnp.maximum(i - tile0, 0), 0)),
        ],
        out_shape=[jax.ShapeDtypeStruct((sample_row0, d), F32), jax.ShapeDtypeStruct((m - sample_row0, d), F32)],
        compiler_params=_params(1),
        name="final_norm",
    )(x, ya, yb, g.reshape(1, d))


def _top_k(x, k):
    lane = jnp.arange(x.shape[-1], dtype=jnp.int32)
    vals, idxs = [], []
    for _ in range(k):
        i = jnp.argmax(x, axis=-1).astype(jnp.int32)
        vals.append(jnp.max(x, axis=-1))
        idxs.append(i)
        x = jnp.where(lane == i[..., None], -jnp.inf, x)
    return jnp.stack(vals, axis=-1), jnp.stack(idxs, axis=-1)


def _route(logits, b_rg, b_re, n_groups, n_experts, moe_block):
    n = logits.shape[0]
    epg = n_experts // n_groups
    p_group = jax.nn.softmax(logits[:, :n_groups] + b_rg, axis=-1)
    g_w, g_idx = _top_k(p_group, 1)
    e_logits = (logits[:, n_groups:n_groups + n_experts] + b_re).reshape(n, n_groups, epg)
    in_group = jnp.take_along_axis(e_logits, g_idx[:, :, None], axis=1)[:, 0]
    top_logit, top_local = _top_k(in_group, TOP_K)
    gate = g_w * jax.nn.softmax(top_logit, axis=-1)
    expert_idx = g_idx * epg + top_local

    a = n * TOP_K
    flat_e = expert_idx.reshape(a).astype(jnp.int32)
    onehot = (flat_e[:, None] == jnp.arange(n_experts, dtype=jnp.int32)[None, :]).astype(jnp.int32)
    rank = jnp.sum((jnp.cumsum(onehot, axis=0) - onehot) * onehot, axis=1)
    counts = jnp.sum(onehot, axis=0)
    padded = (counts + moe_block - 1) // moe_block * moe_block
    pad_end = jnp.cumsum(padded)
    pad_start = pad_end - padded
    dest = pad_start[flat_e] + rank
    n_blocks = (a + n_experts * (moe_block - 1) + moe_block - 1) // moe_block
    n_rows = n_blocks * moe_block
    fill = jnp.stack([jnp.arange(n_rows, dtype=jnp.int32) % n, jnp.zeros((n_rows,), jnp.int32)], axis=1)
    vals = jnp.stack([jnp.arange(a, dtype=jnp.int32) // TOP_K,
                      lax.bitcast_convert_type(gate.reshape(a).astype(F32), jnp.int32)], axis=1)
    packed = fill.at[dest].set(vals)
    row_token = packed[:, 0]
    row_gate = lax.bitcast_convert_type(packed[:, 1], F32)
    n_active = (pad_end[-1] // moe_block).astype(jnp.int32)
    blk = jnp.minimum(jnp.arange(n_blocks, dtype=jnp.int32), n_active - 1) * moe_block
    block_expert = jnp.minimum(jnp.searchsorted(pad_end, blk, side='right'), n_experts - 1).astype(jnp.int32)
    return row_token, row_gate, dest.reshape(n, TOP_K), block_expert, n_active.reshape(1)


def kernel(x_prompt, mem_prompt, x_sample, cache_kv_latent, cache_k_rope, page_table, state_hgrn, cache_mem_k, cache_mem_v, norm_mix, w_in, q_norm, w_uq, kv_norm, w_uk, w_uv, lb_logits, hg_norm, w_out_mla, w_out_hgrn, w_out, norm_cross, norm_mem, w_cq, w_ckv, w_co, norm_ffn, w_router_group, b_router_group, w_router_expert, b_router_expert, w_expert_gate_up, w_expert_down, norm_final):
    batch, seq, d = x_prompt.shape
    dec_batch, dec_seq, _ = x_sample.shape
    depth = w_in.shape[0]
    assert depth == 1
    page = cache_kv_latent.shape[2]
    kv_lora = cache_kv_latent.shape[3]
    rope = cache_k_rope.shape[3]
    half = rope // 2
    past_len = page_table.shape[1] * page
    q_lora = q_norm.shape[1]
    hg_heads, hg_dk = state_hgrn.shape[2], state_hgrn.shape[3]
    hgw = hg_heads * hg_dk
    mla_heads = (w_uq.shape[2] - w_uk.shape[2]) // rope
    nope = w_uk.shape[2] // mla_heads
    v_head = w_uv.shape[2] // mla_heads
    mem_tokens, x_heads, x_dh = cache_mem_k.shape[2], cache_mem_k.shape[3], cache_mem_k.shape[4]
    xw = x_heads * x_dh
    n_groups = w_router_group.shape[2]
    n_experts = w_router_expert.shape[2]
    mp = batch * seq
    ms = dec_batch * dec_seq
    m_all = mp + ms
    mla_scale = (nope + rope) ** -0.5
    assert nope == LANES and 2 * half <= LANES and hg_dk == LANES

    n_front = q_lora + kv_lora + rope
    off_cq, off_ckv, off_kr = 0, q_lora, q_lora + kv_lora
    off_hq, off_hf, off_hi, off_hg = 0, hgw, 2 * hgw, 3 * hgw
    off_ga, off_gb = 4 * hgw, 4 * hgw + d
    w_in_t = jnp.swapaxes(w_in[0], 0, 1)
    w_back = w_in_t[n_front:].astype(BF16)
    w_front = jnp.concatenate([w_in_t[:n_front], jnp.zeros((LANES - rope, d), F32)], axis=0).astype(BF16)

    x0 = jnp.concatenate([x_prompt.reshape(mp, d), x_sample.reshape(ms, d)], axis=0)
    pos = jnp.concatenate([jnp.tile(jnp.arange(seq), batch), jnp.tile(past_len + jnp.arange(dec_seq), dec_batch)])
    cos_t, sin_t = _rope_tables(pos, half)

    tm_in = _tile(m_all, 1152, unit=2 * SUBLANES)
    z = _matmul(x0, w_back, w_is_nk=True, gain=norm_mix[0], tm=tm_in, tn_cap=1024, name="mm_in")
    z_front = _matmul(x0, w_front, w_is_nk=True, gain=norm_mix[0], tm=tm_in, tn_cap=1024, name="mm_in_front")
    lat, kr_pad = _lat_kr(z_front, kv_norm[0], cos_t, sin_t, off_ckv=off_ckv, off_kr=off_kr, kv_lora=kv_lora,
                          half=half)
    w_uq_p = jnp.pad(w_uq[0].reshape(q_lora, mla_heads, nope + rope), ((0, 0), (0, 0), (0, LANES - rope)))
    w_uq_p = w_uq_p.reshape(q_lora, mla_heads * (nope + LANES)).astype(BF16)
    q_all, q_tok = _q_proj(z_front, w_uq_p, q_norm[0], cos_t, sin_t, off_cq=off_cq, q_lora=q_lora, heads=mla_heads,
                           nope=nope, half=half, scale=mla_scale, sample_row0=mp)
    w_uk_lo, w_uv_lo = w_uk[0].astype(BF16), w_uv[0].astype(BF16)

    k_all, v_all = _kv_proj(lat, w_uk_lo, w_uv_lo, kr_pad, rows=mp, heads=mla_heads, nope=nope)
    a_all = _flash(q_all, k_all, v_all, jnp.zeros((m_all, mla_heads * v_head), BF16), batch=batch, seq=seq,
                   heads=mla_heads, dv=v_head)

    rows_s = dec_seq * mla_heads
    q_lat = _q_lat(q_tok, w_uk_lo, heads=mla_heads, nope=nope).reshape(dec_batch, rows_s, kv_lora)
    q_rope_s = q_tok.reshape(ms, mla_heads, nope + LANES)[:, :, nope:nope + rope].astype(F32)
    q_rope_s = q_rope_s.reshape(dec_batch, rows_s, rope)
    lat_s = lat[mp:].reshape(dec_batch, dec_seq, kv_lora)
    kr_s = kr_pad[mp:, :rope].reshape(dec_batch, dec_seq, rope)
    o_lat = _paged_attention(page_table, q_lat, q_rope_s, lat_s, kr_s,
                             cache_kv_latent.reshape(-1, page, kv_lora),
                             jnp.swapaxes(cache_k_rope, 2, 3).reshape(-1, rope, page),
                             t_new=dec_seq)
    a_all = _o_uv(o_lat.reshape(ms, mla_heads * kv_lora), w_uv_lo, a_all, heads=mla_heads, dv=v_head, row0=mp)

    lower_bounds = jnp.cumsum(jax.nn.softmax(lb_logits.astype(F32), axis=0), axis=0)
    lb = lower_bounds[0]
    chunk_p = math.gcd(HG_CHUNK, seq)
    hg_offs = (off_hq, off_hf, off_hi, off_hg)
    b_all, s_p = _hgrn(z, lb, hg_norm[0], jnp.zeros((m_all, hgw), BF16), offs=hg_offs, n_seq=batch, rows=seq,
                       row_block=_tile(seq, 512, unit=chunk_p), heads_total=hg_heads, heads=math.gcd(8, hg_heads),
                       dk=hg_dk, chunk=chunk_p)
    b_all, s_s = _hgrn_step(z, lb, hg_norm[0], state_hgrn[0], b_all, offs=hg_offs, row0=mp, n_seq_total=dec_batch,
                            t_new=dec_seq, heads=hg_heads, dk=hg_dk)

    merged = _gated_merge(a_all, b_all, z, w_out_mla[0].astype(BF16), w_out_hgrn[0].astype(BF16), off_ga=off_ga,
                          off_gb=off_gb, tm=tm_in)
    x1 = _matmul(merged, w_out[0].astype(BF16), res=x0, tm=tm_in, name="mm_out")

    kv_mem = _matmul(mem_prompt.reshape(batch * mem_tokens, d), w_ckv[0], gain=norm_mem[0], tm=batch * mem_tokens,
                     name="mem_kv")
    mk_p, mv_p = kv_mem[:, :xw].reshape(batch, mem_tokens, xw), kv_mem[:, xw:].reshape(batch, mem_tokens, xw)
    qx = _matmul(x1, w_cq[0], gain=norm_cross[0], name="cross_q")
    tq_x = 512
    ox_p = _cross_attention(qx[:mp].reshape(mp // tq_x, tq_x, xw), mk_p, mv_p, heads=x_heads,
                            groups_per_mem=seq // tq_x)
    rows_pad = -(-dec_seq // SUBLANES) * SUBLANES
    qx_s = jnp.pad(qx[mp:].reshape(dec_batch, dec_seq, xw), ((0, 0), (0, rows_pad - dec_seq), (0, 0)))
    ox_s = _cross_attention_step(qx_s, cache_mem_k.reshape(dec_batch, mem_tokens * x_heads, x_dh),
                                 cache_mem_v.reshape(dec_batch, mem_tokens * x_heads, x_dh), heads=x_heads,
                                 n_seq=math.gcd(4, dec_batch))
    ox = jnp.concatenate([ox_p.reshape(mp, xw), ox_s[:, :dec_seq].reshape(ms, xw)], axis=0)
    x2 = _matmul(ox, w_co[0], res=x1, name="cross_out")

    n_r = n_groups + n_experts
    w_r = jnp.concatenate([jnp.swapaxes(w_router_group[0], 0, 1), jnp.swapaxes(w_router_expert[0], 0, 1),
                           jnp.zeros((-n_r % LANES, d), F32)], axis=0)
    logits, h_ffn = _router(x2, w_r, norm_ffn[0])
    moe_block = 256

    def route(lg):
        return _route(lg, b_router_group[0], b_router_expert[0], n_groups, n_experts, moe_block)

    row_token, row_gate, dest, block_expert, n_active = route(logits)
    y_sorted = _moe_ffn(block_expert, n_active, h_ffn[row_token], row_gate[:, None],
                        w_expert_gate_up[0], w_expert_down[0], block=moe_block)
    y_p, y_s = _final_norm(x2, y_sorted[dest[:, 0]], y_sorted[dest[:, 1]], norm_final, sample_row0=mp)

    y_prompt = y_p.reshape(batch, seq, d)
    y_sample = y_s.reshape(dec_batch, dec_seq, d)
    n_prompt_pages = mp // page
    kv_latent_prompt = lat[:mp].reshape(depth, n_prompt_pages, page, kv_lora)
    k_rope_prompt = kr_pad[:mp, :rope].reshape(depth, n_prompt_pages, page, rope)
    state_hgrn_prompt = s_p[None]
    mem_k_prompt = mk_p.reshape(depth, batch, mem_tokens, x_heads, x_dh)
    mem_v_prompt = mv_p.reshape(depth, batch, mem_tokens, x_heads, x_dh)
    kv_latent_sample = lat_s[None]
    k_rope_sample = kr_s[None]
    state_hgrn_sample = s_s[None]
    return (y_prompt, y_sample, kv_latent_prompt, k_rope_prompt, state_hgrn_prompt, mem_k_prompt, mem_v_prompt,
            kv_latent_sample, k_rope_sample, state_hgrn_sample)
```

```python
import functools
import math

import jax
import jax.numpy as jnp
from jax import lax
from jax.experimental import pallas as pl
from jax.experimental.pallas import tpu as pltpu

F32 = jnp.float32
BF16 = jnp.bfloat16

EPS = 1e-6
NEG = -1e30
ROPE_THETA = 10000.0
TOP_K = 2
HG_CHUNK = 64
LANES = 128
SUBLANES = 8
VMEM_LIMIT = 56 * 1024 * 1024
PAGE_GROUP = 8
PAGE_SLOTS = 6


def _params(n_axes):
    return pltpu.CompilerParams(dimension_semantics=("arbitrary",) * n_axes, vmem_limit_bytes=VMEM_LIMIT)


def _tile(n, cap, unit=LANES):
    if n <= cap:
        return n
    t = cap - cap % unit
    while t >= unit:
        if n % t == 0:
            return t
        t -= unit
    raise ValueError(f"no tile for {n} under {cap}")


def _rms(x, g):
    return x * lax.rsqrt(jnp.mean(x * x, axis=-1, keepdims=True) + EPS) * g


def _dot(a, b):
    return jnp.dot(a, b, preferred_element_type=F32)


def _dot_nt(a, b):
    return lax.dot_general(a, b, (((1,), (1,)), ((), ())), preferred_element_type=F32)


def _dot_tn(a, b):
    return lax.dot_general(a, b, (((0,), (0,)), ((), ())), preferred_element_type=F32)


def _mm_kernel(*refs, has_gain, has_res, w_is_nk):
    it = iter(refs)
    x_ref, w_ref = next(it), next(it)
    g_ref = next(it) if has_gain else None
    r_ref = next(it) if has_res else None
    o_ref = next(it)
    if has_gain:
        xs_ref = next(it)

        @pl.when(pl.program_id(1) == 0)
        def _():
            xs_ref[...] = _rms(x_ref[...].astype(F32), g_ref[...]).astype(xs_ref.dtype)

        x = xs_ref[...]
    else:
        x = x_ref[...].astype(w_ref.dtype)
    acc = _dot_nt(x, w_ref[...]) if w_is_nk else _dot(x, w_ref[...])
    if has_res:
        acc = r_ref[...] + acc
    o_ref[...] = acc.astype(o_ref.dtype)


def _matmul(x, w, *, w_is_nk=False, gain=None, res=None, x_col=0, tm=512, tn_cap=512, out_dtype=F32, name="mm"):
    m = x.shape[0]
    n, k = w.shape if w_is_nk else w.shape[::-1]
    assert x_col % k == 0 and m % tm == 0
    tn = _tile(n, tn_cap)
    xb = x_col // k
    w_spec = pl.BlockSpec((tn, k), lambda i, j: (j, 0)) if w_is_nk else pl.BlockSpec((k, tn), lambda i, j: (0, j))
    in_specs = [pl.BlockSpec((tm, k), lambda i, j: (i, xb)), w_spec]
    args = [x, w]
    scratch = []
    if gain is not None:
        in_specs.append(pl.BlockSpec((1, k), lambda i, j: (0, 0)))
        args.append(gain.reshape(1, k).astype(F32))
        scratch.append(pltpu.VMEM((tm, k), w.dtype))
    if res is not None:
        in_specs.append(pl.BlockSpec((tm, tn), lambda i, j: (i, j)))
        args.append(res)
    return pl.pallas_call(
        functools.partial(_mm_kernel, has_gain=gain is not None, has_res=res is not None, w_is_nk=w_is_nk),
        grid=(m // tm, n // tn),
        in_specs=in_specs,
        out_specs=pl.BlockSpec((tm, tn), lambda i, j: (i, j)),
        out_shape=jax.ShapeDtypeStruct((m, n), out_dtype),
        scratch_shapes=scratch,
        compiler_params=_params(2),
        name=name,
    )(*args)


def _rope_tile(x, cos_t, sin_t, half):
    lane = lax.broadcasted_iota(jnp.int32, x.shape, 1)
    swapped = jnp.where(lane < half, pltpu.roll(x, LANES - half, 1), pltpu.roll(x, half, 1))
    return x * cos_t + swapped * sin_t


def _rope_tables(pos, half):
    inv = ROPE_THETA ** (-jnp.arange(half, dtype=F32) / half)
    ang = pos.astype(F32)[:, None] * inv[None, :]
    cos, sin = jnp.cos(ang), jnp.sin(ang)
    zeros = jnp.zeros((pos.shape[0], LANES - 2 * half), F32)
    return jnp.concatenate([cos, cos, zeros], axis=1), jnp.concatenate([-sin, sin, zeros], axis=1)


def _latkr_kernel(ckv_ref, kr_ref, g_ref, cos_ref, sin_ref, lat_ref, kro_ref, *, half):
    lat_ref[...] = _rms(ckv_ref[...], g_ref[...])
    kro_ref[...] = _rope_tile(kr_ref[...], cos_ref[...], sin_ref[...], half)


def _lat_kr(z, kv_norm, cos_t, sin_t, *, off_ckv, off_kr, kv_lora, half, tm=512):
    m = z.shape[0]
    return pl.pallas_call(
        functools.partial(_latkr_kernel, half=half),
        grid=(m // tm,),
        in_specs=[
            pl.BlockSpec((tm, kv_lora), lambda i: (i, off_ckv // kv_lora)),
            pl.BlockSpec((tm, LANES), lambda i: (i, off_kr // LANES)),
            pl.BlockSpec((1, kv_lora), lambda i: (0, 0)),
            pl.BlockSpec((tm, LANES), lambda i: (i, 0)),
            pl.BlockSpec((tm, LANES), lambda i: (i, 0)),
        ],
        out_specs=[pl.BlockSpec((tm, kv_lora), lambda i: (i, 0)), pl.BlockSpec((tm, LANES), lambda i: (i, 0))],
        out_shape=[jax.ShapeDtypeStruct((m, kv_lora), F32), jax.ShapeDtypeStruct((m, LANES), F32)],
        compiler_params=_params(1),
        name="lat_kr",
    )(z, z, kv_norm.reshape(1, kv_lora), cos_t, sin_t)


def _qproj_kernel(cq_ref, w_ref, g_ref, cos_ref, sin_ref, o_ref, os_ref, *, heads, nope, half, scale, sample_tile0):
    width = nope + LANES
    q = _dot(_rms(cq_ref[...], g_ref[...]).astype(w_ref.dtype), w_ref[...]) * scale
    cos_t, sin_t = cos_ref[...], sin_ref[...]
    parts = []
    for h in range(heads):
        q_nope = q[:, h * width:h * width + nope].astype(o_ref.dtype)
        q_rope = _rope_tile(q[:, h * width + nope:(h + 1) * width], cos_t, sin_t, half).astype(o_ref.dtype)
        parts.append((q_nope, q_rope))
        o_ref[h, :, :nope] = q_nope
        o_ref[h, :, nope:] = q_rope

    @pl.when(pl.program_id(0) >= sample_tile0)
    def _():
        for h, (q_nope, q_rope) in enumerate(parts):
            os_ref[:, h * width:h * width + nope] = q_nope
            os_ref[:, h * width + nope:(h + 1) * width] = q_rope


def _q_proj(z, w_uq_p, q_norm, cos_t, sin_t, *, off_cq, q_lora, heads, nope, half, scale, sample_row0, tm=512):
    m = z.shape[0]
    width = nope + LANES
    assert sample_row0 % tm == 0
    tile0 = sample_row0 // tm
    return pl.pallas_call(
        functools.partial(_qproj_kernel, heads=heads, nope=nope, half=half, scale=scale, sample_tile0=tile0),
        grid=(m // tm,),
        in_specs=[
            pl.BlockSpec((tm, q_lora), lambda i: (i, off_cq // q_lora)),
            pl.BlockSpec((q_lora, heads * width), lambda i: (0, 0)),
            pl.BlockSpec((1, q_lora), lambda i: (0, 0)),
            pl.BlockSpec((tm, LANES), lambda i: (i, 0)),
            pl.BlockSpec((tm, LANES), lambda i: (i, 0)),
        ],
        out_specs=[
            pl.BlockSpec((heads, tm, width), lambda i: (0, i, 0)),
            pl.BlockSpec((tm, heads * width), lambda i: (jnp.maximum(i - tile0, 0), 0)),
        ],
        out_shape=[
            jax.ShapeDtypeStruct((heads, m, width), BF16),
            jax.ShapeDtypeStruct((m - sample_row0, heads * width), BF16),
        ],
        compiler_params=_params(1),
        name="q_proj",
    )(z, w_uq_p, q_norm.reshape(1, q_lora), cos_t, sin_t)


def _kvproj_kernel(lat_ref, wk_ref, wv_ref, kr_ref, k_ref, v_ref, *, heads, nope):
    lat = lat_ref[...].astype(wk_ref.dtype)
    k = _dot(lat, wk_ref[...])
    kr = kr_ref[...].astype(k_ref.dtype)
    for h in range(heads):
        k_ref[h, :, :nope] = k[:, h * nope:(h + 1) * nope].astype(k_ref.dtype)
        k_ref[h, :, nope:] = kr
    v_ref[...] = _dot(lat, wv_ref[...]).astype(v_ref.dtype)


def _kv_proj(lat, w_uk, w_uv, kr_pad, *, rows, heads, nope, tm=512):
    kv_lora = lat.shape[1]
    nv = w_uv.shape[1]
    return pl.pallas_call(
        functools.partial(_kvproj_kernel, heads=heads, nope=nope),
        grid=(rows // tm,),
        in_specs=[
            pl.BlockSpec((tm, kv_lora), lambda i: (i, 0)),
            pl.BlockSpec((kv_lora, heads * nope), lambda i: (0, 0)),
            pl.BlockSpec((kv_lora, nv), lambda i: (0, 0)),
            pl.BlockSpec((tm, LANES), lambda i: (i, 0)),
        ],
        out_specs=[
            pl.BlockSpec((heads, tm, nope + LANES), lambda i: (0, i, 0)),
            pl.BlockSpec((tm, nv), lambda i: (i, 0)),
        ],
        out_shape=[
            jax.ShapeDtypeStruct((heads, rows, nope + LANES), BF16),
            jax.ShapeDtypeStruct((rows, nv), BF16),
        ],
        compiler_params=_params(1),
        name="kv_proj",
    )(lat, w_uk, w_uv, kr_pad)


def _flash_kernel(q_ref, k_ref, v_ref, buf_ref, o_ref, *, tq, tk):
    del buf_ref
    qi = pl.program_id(2)
    q = q_ref[0]
    dv = v_ref.shape[1]

    def step(kb, carry, diag):
        m, l, acc = carry
        start = pl.multiple_of(kb * tk, tk)
        k = k_ref[0, pl.ds(start, tk), :]
        v = v_ref[pl.ds(start, tk), :]
        s = _dot_nt(q, k)
        if diag is not None:
            row = lax.broadcasted_iota(jnp.int32, s.shape, 0)
            col = lax.broadcasted_iota(jnp.int32, s.shape, 1) + diag * tk
            s = jnp.where(col <= row, s, NEG)
        m_new = jnp.maximum(m, jnp.max(s, axis=-1, keepdims=True))
        alpha = jnp.exp(m - m_new)
        p = jnp.exp(s - m_new)
        l = l * alpha + jnp.sum(p, axis=-1, keepdims=True)
        acc = acc * alpha + _dot(p.astype(v.dtype), v)
        return m_new, l, acc

    n_full = qi * (tq // tk)
    carry = (jnp.full((tq, 1), NEG, F32), jnp.zeros((tq, 1), F32), jnp.zeros((tq, dv), F32))
    carry = lax.fori_loop(0, n_full, functools.partial(step, diag=None), carry)
    for j in range(tq // tk):
        carry = step(n_full + j, carry, j)
    _, l, acc = carry
    o_ref[...] = (acc / l).astype(o_ref.dtype)


def _flash(q_all, k_all, v_all, buf, *, batch, seq, heads, dv, tq=1024, tk=512):
    nq = seq // tq
    width = q_all.shape[2]
    assert tq % tk == 0
    return pl.pallas_call(
        functools.partial(_flash_kernel, tq=tq, tk=tk),
        grid=(batch, heads, nq),
        in_specs=[
            pl.BlockSpec((1, tq, width), lambda b, h, i: (h, b * nq + i, 0)),
            pl.BlockSpec((1, seq, width), lambda b, h, i: (h, b, 0)),
            pl.BlockSpec((seq, dv), lambda b, h, i: (b, h)),
            pl.BlockSpec(memory_space=pl.ANY),
        ],
        out_specs=pl.BlockSpec((tq, dv), lambda b, h, i: (b * nq + i, h)),
        out_shape=jax.ShapeDtypeStruct(buf.shape, buf.dtype),
        input_output_aliases={3: 0},
        compiler_params=_params(3),
        name="flash",
    )(q_all, k_all, v_all, buf)


def _qlat_kernel(q_ref, w_ref, o_ref, *, nope):
    o_ref[...] = _dot_nt(q_ref[:, :nope], w_ref[...])


def _q_lat(q_tok, w_uk, *, heads, nope):
    rows = q_tok.shape[0]
    width = q_tok.shape[1] // heads
    kv_lora = w_uk.shape[0]
    return pl.pallas_call(
        functools.partial(_qlat_kernel, nope=nope),
        grid=(heads,),
        in_specs=[
            pl.BlockSpec((rows, width), lambda h: (0, h)),
            pl.BlockSpec((kv_lora, nope), lambda h: (0, h)),
        ],
        out_specs=pl.BlockSpec((rows, kv_lora), lambda h: (0, h)),
        out_shape=jax.ShapeDtypeStruct((rows, heads * kv_lora), F32),
        compiler_params=_params(1),
        name="q_lat",
    )(q_tok, w_uk)


def _ouv_kernel(x_ref, w_ref, buf_ref, o_ref):
    del buf_ref
    o_ref[...] = _dot(x_ref[...].astype(w_ref.dtype), w_ref[...]).astype(o_ref.dtype)


def _o_uv(o_lat, w_uv, buf, *, heads, dv, row0):
    rows = o_lat.shape[0]
    kv_lora = o_lat.shape[1] // heads
    assert row0 % rows == 0
    return pl.pallas_call(
        _ouv_kernel,
        grid=(heads,),
        in_specs=[
            pl.BlockSpec((rows, kv_lora), lambda h: (0, h)),
            pl.BlockSpec((kv_lora, dv), lambda h: (0, h)),
            pl.BlockSpec(memory_space=pl.ANY),
        ],
        out_specs=pl.BlockSpec((rows, dv), lambda h: (row0 // rows, h)),
        out_shape=jax.ShapeDtypeStruct(buf.shape, buf.dtype),
        input_output_aliases={2: 0},
        compiler_params=_params(1),
        name="o_uv",
    )(o_lat, w_uv, buf)


def _paged_kernel(pt_ref, ql_ref, qr_ref, tok_ref, nl_ref, nr_ref, cl_hbm, cr_hbm, o_ref, bl_ref, br_ref, sem_ref,
                  *, n_seq, n_groups, group, page, slots, t_new):
    b = pl.program_id(0)
    total = n_seq * n_groups
    ahead = slots - 1

    def copies(t):
        seq = lax.div(t, n_groups)
        g = lax.rem(t, n_groups)
        slot = lax.rem(t, slots)
        out = []
        for i in range(group):
            pid = pt_ref[seq, g * group + i]
            keys = pl.ds(i * page, page)
            out.append(pltpu.make_async_copy(cl_hbm.at[pid], bl_ref.at[slot, keys], sem_ref.at[0, slot, i]))
            out.append(pltpu.make_async_copy(cr_hbm.at[pid], br_ref.at[slot, i], sem_ref.at[1, slot, i]))
        return out

    @pl.when(b == 0)
    def _():
        for t0 in range(min(ahead, total)):
            for c in copies(jnp.int32(t0)):
                c.start()

    ql = ql_ref[0]
    qr = qr_ref[0]
    ql_lo = ql.astype(BF16)
    qr_lo = qr.astype(BF16)
    rows = ql.shape[0]
    kv = ql.shape[1]

    def body(g, carry):
        m, l, acc = carry
        t = b * n_groups + g

        @pl.when(t + ahead < total)
        def _():
            for c in copies(t + ahead):
                c.start()

        for c in copies(t):
            c.wait()
        slot = lax.rem(t, slots)
        c_blk = bl_ref[slot].astype(BF16)
        s_rope = jnp.concatenate([_dot(qr_lo, br_ref[slot, i].astype(BF16)) for i in range(group)], axis=1)
        s = _dot_nt(ql_lo, c_blk) + s_rope
        m_new = jnp.maximum(m, jnp.max(s, axis=-1, keepdims=True))
        alpha = jnp.exp(m - m_new)
        p = jnp.exp(s - m_new)
        l = l * alpha + jnp.sum(p, axis=-1, keepdims=True)
        acc = acc * alpha + _dot(p.astype(BF16), c_blk)
        return m_new, l, acc

    init = (jnp.full((rows, 1), NEG, F32), jnp.zeros((rows, 1), F32), jnp.zeros((rows, kv), F32))
    m, l, acc = lax.fori_loop(0, n_groups, body, init)

    t_of_row = tok_ref[...]
    s_new = []
    for j in range(t_new):
        sj = (jnp.sum(ql * nl_ref[0, j:j + 1, :], axis=-1, keepdims=True)
              + jnp.sum(qr * nr_ref[0, j:j + 1, :], axis=-1, keepdims=True))
        s_new.append(jnp.where(t_of_row >= j, sj, NEG))
    m_new = m
    for sj in s_new:
        m_new = jnp.maximum(m_new, sj)
    alpha = jnp.exp(m - m_new)
    l = l * alpha
    acc = acc * alpha
    for j, sj in enumerate(s_new):
        pj = jnp.exp(sj - m_new)
        l = l + pj
        acc = acc + pj * nl_ref[0, j:j + 1, :]
    o_ref[0] = acc / l


def _paged_attention(page_table, q_lat, q_rope, new_lat, new_kr, cache_lat, cache_kr_t, *, t_new):
    nb, rows, kv = q_lat.shape
    rope = q_rope.shape[2]
    n_pages = page_table.shape[1]
    page = cache_lat.shape[1]
    group = math.gcd(PAGE_GROUP, n_pages)
    n_groups = n_pages // group
    slots = PAGE_SLOTS
    grid_spec = pltpu.PrefetchScalarGridSpec(
        num_scalar_prefetch=1,
        grid=(nb,),
        in_specs=[
            pl.BlockSpec((1, rows, kv), lambda b, pt: (b, 0, 0)),
            pl.BlockSpec((1, rows, rope), lambda b, pt: (b, 0, 0)),
            pl.BlockSpec((rows, 1), lambda b, pt: (0, 0)),
            pl.BlockSpec((1, t_new, kv), lambda b, pt: (b, 0, 0)),
            pl.BlockSpec((1, t_new, rope), lambda b, pt: (b, 0, 0)),
            pl.BlockSpec(memory_space=pl.ANY),
            pl.BlockSpec(memory_space=pl.ANY),
        ],
        out_specs=pl.BlockSpec((1, rows, kv), lambda b, pt: (b, 0, 0)),
        scratch_shapes=[
            pltpu.VMEM((slots, group * page, kv), F32),
            pltpu.VMEM((slots, group, rope, page), F32),
            pltpu.SemaphoreType.DMA((2, slots, group)),
        ],
    )
    tok = (jnp.arange(rows, dtype=jnp.int32) // (rows // t_new)).reshape(rows, 1)
    return pl.pallas_call(
        functools.partial(_paged_kernel, n_seq=nb, n_groups=n_groups, group=group, page=page, slots=slots,
                          t_new=t_new),
        grid_spec=grid_spec,
        out_shape=jax.ShapeDtypeStruct((nb, rows, kv), F32),
        compiler_params=_params(1),
        name="paged_attn",
    )(page_table, q_lat, q_rope, tok, new_lat, new_kr, cache_lat, cache_kr_t)


def _pad_rows(x, rows):
    if x.shape[0] == rows:
        return x
    return jnp.concatenate([x, jnp.zeros((rows - x.shape[0], x.shape[1]), x.dtype)], axis=0)


def _hgrn_kernel(hq_ref, hf_ref, hi_ref, hg_ref, lb_ref, gn_ref, buf_ref, o_ref, so_ref, st_ref,
                 *, heads, dk, chunk, n_chunks):
    del buf_ref
    @pl.when(pl.program_id(2) == 0)
    def _():
        st_ref[...] = jnp.zeros_like(st_ref)

    keys = max(chunk, LANES)
    row = lax.broadcasted_iota(jnp.int32, (chunk, keys), 0)
    col = lax.broadcasted_iota(jnp.int32, (chunk, keys), 1)
    causal = col <= row
    row_id = lax.broadcasted_iota(jnp.int32, (chunk, dk), 0)

    def body(c, carry):
        r0 = pl.multiple_of(c * chunk, chunk)
        for h in range(heads):
            cols = slice(h * dk, (h + 1) * dk)
            lb = lb_ref[:, cols]
            q = jax.nn.silu(hq_ref[pl.ds(r0, chunk), cols]) * dk ** -0.5
            f = lb + (1.0 - lb) * jax.nn.sigmoid(hf_ref[pl.ds(r0, chunk), cols])
            logf = jnp.log(f)
            k = 1.0 - f
            v = hi_ref[pl.ds(r0, chunk), cols]
            bsum = logf
            shift = 1
            while shift < chunk:
                bsum = bsum + jnp.where(row_id >= shift, pltpu.roll(bsum, shift, 0), 0.0)
                shift *= 2
            b_last = bsum[chunk - 1:chunk, :]
            q_dec = q * jnp.exp(bsum)
            k_inv = k * jnp.exp(-bsum)
            v_pad = _pad_rows(v, keys)
            att = jnp.where(causal, _dot_nt(q_dec, _pad_rows(k_inv, keys)), 0.0)
            st = st_ref[h]
            o = _dot(att, v_pad) + _dot_nt(q_dec, st)
            k_end = k * jnp.exp(b_last - bsum)
            st_ref[h] = st * jnp.exp(b_last) + _dot_tn(v_pad, _pad_rows(k_end, keys))
            o = _rms(o, gn_ref[...]) * jax.nn.silu(hg_ref[pl.ds(r0, chunk), cols])
            o_ref[pl.ds(r0, chunk), cols] = o.astype(o_ref.dtype)
        return carry

    lax.fori_loop(0, n_chunks, body, 0)

    @pl.when(pl.program_id(2) == pl.num_programs(2) - 1)
    def _():
        for h in range(heads):
            so_ref[0, h] = st_ref[h].T


def _hgrn(z, lb, hg_norm, buf, *, offs, n_seq, rows, row_block, heads_total, heads, dk, chunk):
    hw = heads * dk
    n_hb = heads_total // heads
    n_rb = rows // row_block
    n_chunks = row_block // chunk

    def zspec(off):
        return pl.BlockSpec((row_block, hw), lambda s, j, r: (s * n_rb + r, off // hw + j))

    return pl.pallas_call(
        functools.partial(_hgrn_kernel, heads=heads, dk=dk, chunk=chunk, n_chunks=n_chunks),
        grid=(n_seq, n_hb, n_rb),
        in_specs=[zspec(o) for o in offs] + [
            pl.BlockSpec((1, hw), lambda s, j, r: (0, j)),
            pl.BlockSpec((1, dk), lambda s, j, r: (0, 0)),
            pl.BlockSpec(memory_space=pl.ANY),
        ],
        out_specs=[
            pl.BlockSpec((row_block, hw), lambda s, j, r: (s * n_rb + r, j)),
            pl.BlockSpec((1, heads, dk, dk), lambda s, j, r: (s, j, 0, 0)),
        ],
        out_shape=[
            jax.ShapeDtypeStruct(buf.shape, buf.dtype),
            jax.ShapeDtypeStruct((n_seq, heads_total, dk, dk), F32),
        ],
        input_output_aliases={6: 0},
        scratch_shapes=[pltpu.VMEM((heads, dk, dk), F32)],
        compiler_params=_params(3),
        name="hgrn",
    )(z, z, z, z, lb.reshape(1, -1), hg_norm.reshape(1, dk), buf)


def _hgrn_step_kernel(hq_ref, hf_ref, hi_ref, hg_ref, lb_ref, gn_ref, s0_ref, buf_ref, o_ref, so_ref,
                      *, heads, dk, n_seq, t_new):
    del buf_ref
    rows = n_seq * t_new
    for h in range(heads):
        cols = slice(h * dk, (h + 1) * dk)
        lb = lb_ref[:, cols]
        q = jax.nn.silu(hq_ref[:, cols]) * dk ** -0.5
        f = lb + (1.0 - lb) * jax.nn.sigmoid(hf_ref[:, cols])
        v = hi_ref[:, cols]
        row_id = lax.broadcasted_iota(jnp.int32, (rows, dk), 0)
        o = jnp.zeros((rows, dk), F32)
        for s in range(n_seq):
            st = s0_ref[s, h]
            for t in range(t_new):
                r = s * t_new + t
                f_col = jnp.broadcast_to(f[r:r + 1, :], (dk, dk)).T
                q_col = jnp.broadcast_to(q[r:r + 1, :], (dk, dk)).T
                st = st * f_col + (1.0 - f_col) * v[r:r + 1, :]
                o = jnp.where(row_id == r, jnp.sum(st * q_col, axis=0, keepdims=True), o)
            so_ref[s, h] = st
        o_ref[:, cols] = (_rms(o, gn_ref[...]) * jax.nn.silu(hg_ref[:, cols])).astype(o_ref.dtype)


def _hgrn_step(z, lb, hg_norm, s0, buf, *, offs, row0, n_seq_total, t_new, heads, dk):
    n_seq = 2 * SUBLANES // t_new
    rows = n_seq * t_new
    hw = heads * dk
    assert rows == 2 * SUBLANES and row0 % rows == 0 and n_seq_total % n_seq == 0
    rb0 = row0 // rows

    def zspec(off):
        return pl.BlockSpec((rows, hw), lambda i: (rb0 + i, off // hw))

    return pl.pallas_call(
        functools.partial(_hgrn_step_kernel, heads=heads, dk=dk, n_seq=n_seq, t_new=t_new),
        grid=(n_seq_total // n_seq,),
        in_specs=[zspec(o) for o in offs] + [
            pl.BlockSpec((1, hw), lambda i: (0, 0)),
            pl.BlockSpec((1, dk), lambda i: (0, 0)),
            pl.BlockSpec((n_seq, heads, dk, dk), lambda i: (i, 0, 0, 0)),
            pl.BlockSpec(memory_space=pl.ANY),
        ],
        out_specs=[
            pl.BlockSpec((rows, hw), lambda i: (rb0 + i, 0)),
            pl.BlockSpec((n_seq, heads, dk, dk), lambda i: (i, 0, 0, 0)),
        ],
        out_shape=[
            jax.ShapeDtypeStruct(buf.shape, buf.dtype),
            jax.ShapeDtypeStruct(s0.shape, F32),
        ],
        input_output_aliases={7: 0},
        compiler_params=_params(1),
        name="hgrn_step",
    )(z, z, z, z, lb.reshape(1, -1), hg_norm.reshape(1, dk), s0, buf)


def _gated_kernel(a_ref, b_ref, ga_ref, gb_ref, wa_ref, wb_ref, o_ref):
    o_ref[...] = (jax.nn.sigmoid(ga_ref[...]) * _dot(a_ref[...], wa_ref[...])
                  + jax.nn.sigmoid(gb_ref[...]) * _dot(b_ref[...], wb_ref[...])).astype(o_ref.dtype)


def _gated_merge(a, b, z, w_a, w_b, *, off_ga, off_gb, tm=512, tn=512):
    m, ka = a.shape
    kb = b.shape[1]
    n = w_a.shape[1]
    tn = _tile(n, tn)
    return pl.pallas_call(
        _gated_kernel,
        grid=(m // tm, n // tn),
        in_specs=[
            pl.BlockSpec((tm, ka), lambda i, j: (i, 0)),
            pl.BlockSpec((tm, kb), lambda i, j: (i, 0)),
            pl.BlockSpec((tm, tn), lambda i, j: (i, off_ga // tn + j)),
            pl.BlockSpec((tm, tn), lambda i, j: (i, off_gb // tn + j)),
            pl.BlockSpec((ka, tn), lambda i, j: (0, j)),
            pl.BlockSpec((kb, tn), lambda i, j: (0, j)),
        ],
        out_specs=pl.BlockSpec((tm, tn), lambda i, j: (i, j)),
        out_shape=jax.ShapeDtypeStruct((m, n), a.dtype),
        compiler_params=_params(2),
        name="gated_merge",
    )(a, b, z, z, w_a, w_b)


def _cross_kernel(q_ref, k_ref, v_ref, o_ref, *, heads, dh):
    for h in range(heads):
        cols = slice(h * dh, (h + 1) * dh)
        s = _dot_nt(q_ref[0, :, cols], k_ref[0, :, cols]) * dh ** -0.5
        p = jnp.exp(s - jnp.max(s, axis=-1, keepdims=True))
        p = p / jnp.sum(p, axis=-1, keepdims=True)
        o_ref[0, :, cols] = _dot(p, v_ref[0, :, cols])


def _cross_attention(q, k, v, *, heads, groups_per_mem):
    n, r, w = q.shape
    mem = k.shape[1]
    return pl.pallas_call(
        functools.partial(_cross_kernel, heads=heads, dh=w // heads),
        grid=(n,),
        in_specs=[
            pl.BlockSpec((1, r, w), lambda i: (i, 0, 0)),
            pl.BlockSpec((1, mem, w), lambda i: (i // groups_per_mem, 0, 0)),
            pl.BlockSpec((1, mem, w), lambda i: (i // groups_per_mem, 0, 0)),
        ],
        out_specs=pl.BlockSpec((1, r, w), lambda i: (i, 0, 0)),
        out_shape=jax.ShapeDtypeStruct((n, r, w), F32),
        compiler_params=_params(1),
        name="cross_attn",
    )(q, k, v)


def _cross_step_kernel(q_ref, k_ref, v_ref, o_ref, *, n_seq, heads, dh):
    rows = q_ref.shape[1]
    keys = k_ref.shape[1]
    row = lax.broadcasted_iota(jnp.int32, (heads * rows, keys), 0)
    col_head = jnp.bitwise_and(lax.broadcasted_iota(jnp.int32, (heads * rows, keys), 1), heads - 1)
    own = row < 0
    for h in range(heads):
        own = own | ((row >= h * rows) & (row < (h + 1) * rows) & (col_head == h))
    for s in range(n_seq):
        q = jnp.concatenate([q_ref[s, :, h * dh:(h + 1) * dh] for h in range(heads)], axis=0)
        sc = jnp.where(own, _dot_nt(q, k_ref[s]) * dh ** -0.5, NEG)
        p = jnp.exp(sc - jnp.max(sc, axis=-1, keepdims=True))
        p = p / jnp.sum(p, axis=-1, keepdims=True)
        o = _dot(p, v_ref[s])
        for h in range(heads):
            o_ref[s, :, h * dh:(h + 1) * dh] = o[h * rows:(h + 1) * rows, :]


def _cross_attention_step(q, k, v, *, heads, n_seq=4):
    b, rows, w = q.shape
    keys, dh = k.shape[1], k.shape[2]
    assert heads & (heads - 1) == 0 and b % n_seq == 0
    return pl.pallas_call(
        functools.partial(_cross_step_kernel, n_seq=n_seq, heads=heads, dh=dh),
        grid=(b // n_seq,),
        in_specs=[
            pl.BlockSpec((n_seq, rows, w), lambda i: (i, 0, 0)),
            pl.BlockSpec((n_seq, keys, dh), lambda i: (i, 0, 0)),
            pl.BlockSpec((n_seq, keys, dh), lambda i: (i, 0, 0)),
        ],
        out_specs=pl.BlockSpec((n_seq, rows, w), lambda i: (i, 0, 0)),
        out_shape=jax.ShapeDtypeStruct((b, rows, w), F32),
        compiler_params=_params(1),
        name="cross_attn_step",
    )(q, k, v)


def _router_kernel(x_ref, w_ref, g_ref, lg_ref, h_ref):
    h = _rms(x_ref[...], g_ref[...])
    lg_ref[...] = _dot_nt(h, w_ref[...])
    h_ref[...] = h


def _router(x, w_nk, gain, *, tm=512):
    m, d = x.shape
    n = w_nk.shape[0]
    return pl.pallas_call(
        _router_kernel,
        grid=(m // tm,),
        in_specs=[
            pl.BlockSpec((tm, d), lambda i: (i, 0)),
            pl.BlockSpec((n, d), lambda i: (0, 0)),
            pl.BlockSpec((1, d), lambda i: (0, 0)),
        ],
        out_specs=[pl.BlockSpec((tm, n), lambda i: (i, 0)), pl.BlockSpec((tm, d), lambda i: (i, 0))],
        out_shape=[jax.ShapeDtypeStruct((m, n), F32), jax.ShapeDtypeStruct((m, d), F32)],
        compiler_params=_params(1),
        name="router",
    )(x, w_nk, gain.reshape(1, d))


def _moe_kernel(be_ref, na_ref, h_ref, gate_ref, wgu_ref, wdn_ref, o_ref, *, d_expert):
    @pl.when(pl.program_id(0) < na_ref[0])
    def _():
        gu = _dot(h_ref[...].astype(BF16), wgu_ref[0].astype(BF16))
        act = jax.nn.silu(gu[:, :d_expert]) * gu[:, d_expert:]
        o_ref[...] = _dot(act.astype(BF16), wdn_ref[0].astype(BF16)) * gate_ref[...]

    @pl.when(pl.program_id(0) >= na_ref[0])
    def _():
        o_ref[...] = jnp.zeros_like(o_ref)


def _moe_ffn(block_expert, n_active, h_sorted, row_gate, w_gu, w_dn, *, block):
    n_rows, d = h_sorted.shape
    n_blocks = n_rows // block
    d_expert = w_dn.shape[1]

    def row_block(i, be, na):
        return (jnp.minimum(i, na[0] - 1), 0)

    grid_spec = pltpu.PrefetchScalarGridSpec(
        num_scalar_prefetch=2,
        grid=(n_blocks,),
        in_specs=[
            pl.BlockSpec((block, d), row_block),
            pl.BlockSpec((block, 1), row_block),
            pl.BlockSpec((1, d, 2 * d_expert), lambda i, be, na: (be[i], 0, 0)),
            pl.BlockSpec((1, d_expert, d), lambda i, be, na: (be[i], 0, 0)),
        ],
        out_specs=pl.BlockSpec((block, d), lambda i, be, na: (i, 0)),
    )
    return pl.pallas_call(
        functools.partial(_moe_kernel, d_expert=d_expert),
        grid_spec=grid_spec,
        out_shape=jax.ShapeDtypeStruct((n_rows, d), F32),
        compiler_params=_params(1),
        name="moe_ffn",
    )(block_expert, n_active, h_sorted, row_gate, w_gu, w_dn)


def _final_kernel(x_ref, ya_ref, yb_ref, g_ref, op_ref, os_ref, *, sample_tile0):
    y = _rms(x_ref[...] + (ya_ref[...] + yb_ref[...]), g_ref[...])

    @pl.when(pl.program_id(0) < sample_tile0)
    def _():
        op_ref[...] = y

    @pl.when(pl.program_id(0) >= sample_tile0)
    def _():
        os_ref[...] = y


def _final_norm(x, ya, yb, g, *, sample_row0, tm=512):
    m, d = x.shape
    assert sample_row0 % tm == 0 and m % tm == 0
    tile0 = sample_row0 // tm
    spec = pl.BlockSpec((tm, d), lambda i: (i, 0))
    return pl.pallas_call(
        functools.partial(_final_kernel, sample_tile0=tile0),
        grid=(m // tm,),
        in_specs=[spec, spec, spec, pl.BlockSpec((1, d), lambda i: (0, 0))],
        out_specs=[
            pl.BlockSpec((tm, d), lambda i: (jnp.minimum(i, tile0 - 1), 0)),
            pl.BlockSpec((tm, d), lambda i: (jnp.maximum(i - tile0, 0), 0)),
        ],
        out_shape=[jax.ShapeDtypeStruct((sample_row0, d), F32), jax.ShapeDtypeStruct((m - sample_row0, d), F32)],
        compiler_params=_params(1),
        name="final_norm",
    )(x, ya, yb, g.reshape(1, d))


def _top_k(x, k):
    lane = jnp.arange(x.shape[-1], dtype=jnp.int32)
    vals, idxs = [], []
    for _ in range(k):
        i = jnp.argmax(x, axis=-1).astype(jnp.int32)
        vals.append(jnp.max(x, axis=-1))
        idxs.append(i)
        x = jnp.where(lane == i[..., None], -jnp.inf, x)
    return jnp.stack(vals, axis=-1), jnp.stack(idxs, axis=-1)


def _route(logits, b_rg, b_re, n_groups, n_experts, moe_block):
    n = logits.shape[0]
    epg = n_experts // n_groups
    p_group = jax.nn.softmax(logits[:, :n_groups] + b_rg, axis=-1)
    g_w, g_idx = _top_k(p_group, 1)
    e_logits = (logits[:, n_groups:n_groups + n_experts] + b_re).reshape(n, n_groups, epg)
    in_group = jnp.take_along_axis(e_logits, g_idx[:, :, None], axis=1)[:, 0]
    top_logit, top_local = _top_k(in_group, TOP_K)
    gate = g_w * jax.nn.softmax(top_logit, axis=-1)
    expert_idx = g_idx * epg + top_local

    a = n * TOP_K
    flat_e = expert_idx.reshape(a).astype(jnp.int32)
    onehot = (flat_e[:, None] == jnp.arange(n_experts, dtype=jnp.int32)[None, :]).astype(jnp.int32)
    rank = jnp.sum((jnp.cumsum(onehot, axis=0) - onehot) * onehot, axis=1)
    counts = jnp.sum(onehot, axis=0)
    padded = (counts + moe_block - 1) // moe_block * moe_block
    pad_end = jnp.cumsum(padded)
    pad_start = pad_end - padded
    dest = pad_start[flat_e] + rank
    n_blocks = (a + n_experts * (moe_block - 1) + moe_block - 1) // moe_block
    n_rows = n_blocks * moe_block
    fill = jnp.stack([jnp.arange(n_rows, dtype=jnp.int32) % n, jnp.zeros((n_rows,), jnp.int32)], axis=1)
    vals = jnp.stack([jnp.arange(a, dtype=jnp.int32) // TOP_K,
                      lax.bitcast_convert_type(gate.reshape(a).astype(F32), jnp.int32)], axis=1)
    packed = fill.at[dest].set(vals)
    row_token = packed[:, 0]
    row_gate = lax.bitcast_convert_type(packed[:, 1], F32)
    n_active = (pad_end[-1] // moe_block).astype(jnp.int32)
    blk = jnp.minimum(jnp.arange(n_blocks, dtype=jnp.int32), n_active - 1) * moe_block
    block_expert = jnp.minimum(jnp.searchsorted(pad_end, blk, side='right'), n_experts - 1).astype(jnp.int32)
    return row_token, row_gate, dest.reshape(n, TOP_K), block_expert, n_active.reshape(1)


def kernel(x_prompt, mem_prompt, x_sample, cache_kv_latent, cache_k_rope, page_table, state_hgrn, cache_mem_k, cache_mem_v, norm_mix, w_in, q_norm, w_uq, kv_norm, w_uk, w_uv, lb_logits, hg_norm, w_out_mla, w_out_hgrn, w_out, norm_cross, norm_mem, w_cq, w_ckv, w_co, norm_ffn, w_router_group, b_router_group, w_router_expert, b_router_expert, w_expert_gate_up, w_expert_down, norm_final):
    batch, seq, d = x_prompt.shape
    dec_batch, dec_seq, _ = x_sample.shape
    depth = w_in.shape[0]
    assert depth == 1
    page = cache_kv_latent.shape[2]
    kv_lora = cache_kv_latent.shape[3]
    rope = cache_k_rope.shape[3]
    half = rope // 2
    past_len = page_table.shape[1] * page
    q_lora = q_norm.shape[1]
    hg_heads, hg_dk = state_hgrn.shape[2], state_hgrn.shape[3]
    hgw = hg_heads * hg_dk
    mla_heads = (w_uq.shape[2] - w_uk.shape[2]) // rope
    nope = w_uk.shape[2] // mla_heads
    v_head = w_uv.shape[2] // mla_heads
    mem_tokens, x_heads, x_dh = cache_mem_k.shape[2], cache_mem_k.shape[3], cache_mem_k.shape[4]
    xw = x_heads * x_dh
    n_groups = w_router_group.shape[2]
    n_experts = w_router_expert.shape[2]
    mp = batch * seq
    ms = dec_batch * dec_seq
    m_all = mp + ms
    mla_scale = (nope + rope) ** -0.5
    assert nope == LANES and 2 * half <= LANES and hg_dk == LANES

    n_front = q_lora + kv_lora + rope
    off_cq, off_ckv, off_kr = 0, q_lora, q_lora + kv_lora
    off_hq, off_hf, off_hi, off_hg = 0, hgw, 2 * hgw, 3 * hgw
    off_ga, off_gb = 4 * hgw, 4 * hgw + d
    w_in_t = jnp.swapaxes(w_in[0], 0, 1)
    w_back = w_in_t[n_front:].astype(BF16)
    w_front = jnp.concatenate([w_in_t[:n_front], jnp.zeros((LANES - rope, d), F32)], axis=0).astype(BF16)

    x0 = jnp.concatenate([x_prompt.reshape(mp, d), x_sample.reshape(ms, d)], axis=0)
    pos = jnp.concatenate([jnp.tile(jnp.arange(seq), batch), jnp.tile(past_len + jnp.arange(dec_seq), dec_batch)])
    cos_t, sin_t = _rope_tables(pos, half)

    tm_in = _tile(m_all, 1152, unit=2 * SUBLANES)
    z = _matmul(x0, w_back, w_is_nk=True, gain=norm_mix[0], tm=tm_in, tn_cap=1024, name="mm_in")
    z_front = _matmul(x0, w_front, w_is_nk=True, gain=norm_mix[0], tm=tm_in, tn_cap=1024, name="mm_in_front")
    lat, kr_pad = _lat_kr(z_front, kv_norm[0], cos_t, sin_t, off_ckv=off_ckv, off_kr=off_kr, kv_lora=kv_lora,
                          half=half)
    w_uq_p = jnp.pad(w_uq[0].reshape(q_lora, mla_heads, nope + rope), ((0, 0), (0, 0), (0, LANES - rope)))
    w_uq_p = w_uq_p.reshape(q_lora, mla_heads * (nope + LANES)).astype(BF16)
    q_all, q_tok = _q_proj(z_front, w_uq_p, q_norm[0], cos_t, sin_t, off_cq=off_cq, q_lora=q_lora, heads=mla_heads,
                           nope=nope, half=half, scale=mla_scale, sample_row0=mp)
    w_uk_lo, w_uv_lo = w_uk[0].astype(BF16), w_uv[0].astype(BF16)

    k_all, v_all = _kv_proj(lat, w_uk_lo, w_uv_lo, kr_pad, rows=mp, heads=mla_heads, nope=nope)
    a_all = _flash(q_all, k_all, v_all, jnp.zeros((m_all, mla_heads * v_head), BF16), batch=batch, seq=seq,
                   heads=mla_heads, dv=v_head)

    rows_s = dec_seq * mla_heads
    q_lat = _q_lat(q_tok, w_uk_lo, heads=mla_heads, nope=nope).reshape(dec_batch, rows_s, kv_lora)
    q_rope_s = q_tok.reshape(ms, mla_heads, nope + LANES)[:, :, nope:nope + rope].astype(F32)
    q_rope_s = q_rope_s.reshape(dec_batch, rows_s, rope)
    lat_s = lat[mp:].reshape(dec_batch, dec_seq, kv_lora)
    kr_s = kr_pad[mp:, :rope].reshape(dec_batch, dec_seq, rope)
    o_lat = _paged_attention(page_table, q_lat, q_rope_s, lat_s, kr_s,
                             cache_kv_latent.reshape(-1, page, kv_lora),
                             jnp.swapaxes(cache_k_rope, 2, 3).reshape(-1, rope, page),
                             t_new=dec_seq)
    a_all = _o_uv(o_lat.reshape(ms, mla_heads * kv_lora), w_uv_lo, a_all, heads=mla_heads, dv=v_head, row0=mp)

    lower_bounds = jnp.cumsum(jax.nn.softmax(lb_logits.astype(F32), axis=0), axis=0)
    lb = lower_bounds[0]
    chunk_p = math.gcd(HG_CHUNK, seq)
    hg_offs = (off_hq, off_hf, off_hi, off_hg)
    b_all, s_p = _hgrn(z, lb, hg_norm[0], jnp.zeros((m_all, hgw), BF16), offs=hg_offs, n_seq=batch, rows=seq,
                       row_block=_tile(seq, 512, unit=chunk_p), heads_total=hg_heads, heads=math.gcd(8, hg_heads),
                       dk=hg_dk, chunk=chunk_p)
    b_all, s_s = _hgrn_step(z, lb, hg_norm[0], state_hgrn[0], b_all, offs=hg_offs, row0=mp, n_seq_total=dec_batch,
                            t_new=dec_seq, heads=hg_heads, dk=hg_dk)

    merged = _gated_merge(a_all, b_all, z, w_out_mla[0].astype(BF16), w_out_hgrn[0].astype(BF16), off_ga=off_ga,
                          off_gb=off_gb, tm=tm_in)
    x1 = _matmul(merged, w_out[0].astype(BF16), res=x0, tm=tm_in, name="mm_out")

    kv_mem = _matmul(mem_prompt.reshape(batch * mem_tokens, d), w_ckv[0], gain=norm_mem[0], tm=batch * mem_tokens,
                     name="mem_kv")
    mk_p, mv_p = kv_mem[:, :xw].reshape(batch, mem_tokens, xw), kv_mem[:, xw:].reshape(batch, mem_tokens, xw)
    qx = _matmul(x1, w_cq[0], gain=norm_cross[0], name="cross_q")
    tq_x = 512
    ox_p = _cross_attention(qx[:mp].reshape(mp // tq_x, tq_x, xw), mk_p, mv_p, heads=x_heads,
                            groups_per_mem=seq // tq_x)
    rows_pad = -(-dec_seq // SUBLANES) * SUBLANES
    qx_s = jnp.pad(qx[mp:].reshape(dec_batch, dec_seq, xw), ((0, 0), (0, rows_pad - dec_seq), (0, 0)))
    ox_s = _cross_attention_step(qx_s, cache_mem_k.reshape(dec_batch, mem_tokens * x_heads, x_dh),
                                 cache_mem_v.reshape(dec_batch, mem_tokens * x_heads, x_dh), heads=x_heads,
                                 n_seq=math.gcd(4, dec_batch))
    ox = jnp.concatenate([ox_p.reshape(mp, xw), ox_s[:, :dec_seq].reshape(ms, xw)], axis=0)
    x2 = _matmul(ox, w_co[0], res=x1, name="cross_out")

    n_r = n_groups + n_experts
    w_r = jnp.concatenate([jnp.swapaxes(w_router_group[0], 0, 1), jnp.swapaxes(w_router_expert[0], 0, 1),
                           jnp.zeros((-n_r % LANES, d), F32)], axis=0)
    logits, h_ffn = _router(x2, w_r, norm_ffn[0])
    moe_block = 256

    def route(lg):
        return _route(lg, b_router_group[0], b_router_expert[0], n_groups, n_experts, moe_block)

    row_token, row_gate, dest, block_expert, n_active = route(logits)
    y_sorted = _moe_ffn(block_expert, n_active, h_ffn[row_token], row_gate[:, None],
                        w_expert_gate_up[0], w_expert_down[0], block=moe_block)
    y_p, y_s = _final_norm(x2, y_sorted[dest[:, 0]], y_sorted[dest[:, 1]], norm_final, sample_row0=mp)

    y_prompt = y_p.reshape(batch, seq, d)
    y_sample = y_s.reshape(dec_batch, dec_seq, d)
    n_prompt_pages = mp // page
    kv_latent_prompt = lat[:mp].reshape(depth, n_prompt_pages, page, kv_lora)
    k_rope_prompt = kr_pad[:mp, :rope].reshape(depth, n_prompt_pages, page, rope)
    state_hgrn_prompt = s_p[None]
    mem_k_prompt = mk_p.reshape(depth, batch, mem_tokens, x_heads, x_dh)
    mem_v_prompt = mv_p.reshape(depth, batch, mem_tokens, x_heads, x_dh)
    kv_latent_sample = lat_s[None]
    k_rope_sample = kr_s[None]
    state_hgrn_sample = s_s[None]
    return (y_prompt, y_sample, kv_latent_prompt, k_rope_prompt, state_hgrn_prompt, mem_k_prompt, mem_v_prompt,
            kv_latent_sample, k_rope_sample, state_hgrn_sample)
```

```python
import functools
import math

import jax
import jax.numpy as jnp
from jax import lax
from jax.experimental import pallas as pl
from jax.experimental.pallas import tpu as pltpu

F32 = jnp.float32
BF16 = jnp.bfloat16

EPS = 1e-6
NEG = -1e30
ROPE_THETA = 10000.0
TOP_K = 2
HG_CHUNK = 64
LANES = 128
SUBLANES = 8
VMEM_LIMIT = 56 * 1024 * 1024
PAGE_GROUP = 16
PAGE_SLOTS = 3


def _params(n_axes):
    return pltpu.CompilerParams(dimension_semantics=("arbitrary",) * n_axes, vmem_limit_bytes=VMEM_LIMIT)


def _tile(n, cap, unit=LANES):
    if n <= cap:
        return n
    t = cap - cap % unit
    while t >= unit:
        if n % t == 0:
            return t
        t -= unit
    raise ValueError(f"no tile for {n} under {cap}")


def _rms(x, g):
    return x * lax.rsqrt(jnp.mean(x * x, axis=-1, keepdims=True) + EPS) * g


def _dot(a, b):
    return jnp.dot(a, b, preferred_element_type=F32)


def _dot_nt(a, b):
    return lax.dot_general(a, b, (((1,), (1,)), ((), ())), preferred_element_type=F32)


def _dot_tn(a, b):
    return lax.dot_general(a, b, (((0,), (0,)), ((), ())), preferred_element_type=F32)


def _mm_kernel(*refs, has_gain, has_res, w_is_nk):
    it = iter(refs)
    x_ref, w_ref = next(it), next(it)
    g_ref = next(it) if has_gain else None
    r_ref = next(it) if has_res else None
    o_ref = next(it)
    if has_gain:
        xs_ref = next(it)

        @pl.when(pl.program_id(1) == 0)
        def _():
            xs_ref[...] = _rms(x_ref[...].astype(F32), g_ref[...]).astype(xs_ref.dtype)

        x = xs_ref[...]
    else:
        x = x_ref[...].astype(w_ref.dtype)
    acc = _dot_nt(x, w_ref[...]) if w_is_nk else _dot(x, w_ref[...])
    if has_res:
        acc = r_ref[...] + acc
    o_ref[...] = acc.astype(o_ref.dtype)


def _matmul(x, w, *, w_is_nk=False, gain=None, res=None, x_col=0, tm=512, tn_cap=512, out_dtype=F32, name="mm"):
    m = x.shape[0]
    n, k = w.shape if w_is_nk else w.shape[::-1]
    assert x_col % k == 0 and m % tm == 0
    tn = _tile(n, tn_cap)
    xb = x_col // k
    w_spec = pl.BlockSpec((tn, k), lambda i, j: (j, 0)) if w_is_nk else pl.BlockSpec((k, tn), lambda i, j: (0, j))
    in_specs = [pl.BlockSpec((tm, k), lambda i, j: (i, xb)), w_spec]
    args = [x, w]
    scratch = []
    if gain is not None:
        in_specs.append(pl.BlockSpec((1, k), lambda i, j: (0, 0)))
        args.append(gain.reshape(1, k).astype(F32))
        scratch.append(pltpu.VMEM((tm, k), w.dtype))
    if res is not None:
        in_specs.append(pl.BlockSpec((tm, tn), lambda i, j: (i, j)))
        args.append(res)
    return pl.pallas_call(
        functools.partial(_mm_kernel, has_gain=gain is not None, has_res=res is not None, w_is_nk=w_is_nk),
        grid=(m // tm, n // tn),
        in_specs=in_specs,
        out_specs=pl.BlockSpec((tm, tn), lambda i, j: (i, j)),
        out_shape=jax.ShapeDtypeStruct((m, n), out_dtype),
        scratch_shapes=scratch,
        compiler_params=_params(2),
        name=name,
    )(*args)


def _rope_tile(x, cos_t, sin_t, half):
    lane = lax.broadcasted_iota(jnp.int32, x.shape, 1)
    swapped = jnp.where(lane < half, pltpu.roll(x, LANES - half, 1), pltpu.roll(x, half, 1))
    return x * cos_t + swapped * sin_t


def _rope_tables(pos, half):
    inv = ROPE_THETA ** (-jnp.arange(half, dtype=F32) / half)
    ang = pos.astype(F32)[:, None] * inv[None, :]
    cos, sin = jnp.cos(ang), jnp.sin(ang)
    zeros = jnp.zeros((pos.shape[0], LANES - 2 * half), F32)
    return jnp.concatenate([cos, cos, zeros], axis=1), jnp.concatenate([-sin, sin, zeros], axis=1)


def _latkr_kernel(ckv_ref, kr_ref, g_ref, cos_ref, sin_ref, lat_ref, kro_ref, *, half):
    lat_ref[...] = _rms(ckv_ref[...], g_ref[...])
    kro_ref[...] = _rope_tile(kr_ref[...], cos_ref[...], sin_ref[...], half)


def _lat_kr(z, kv_norm, cos_t, sin_t, *, off_ckv, off_kr, kv_lora, half, tm=512):
    m = z.shape[0]
    return pl.pallas_call(
        functools.partial(_latkr_kernel, half=half),
        grid=(m // tm,),
        in_specs=[
            pl.BlockSpec((tm, kv_lora), lambda i: (i, off_ckv // kv_lora)),
            pl.BlockSpec((tm, LANES), lambda i: (i, off_kr // LANES)),
            pl.BlockSpec((1, kv_lora), lambda i: (0, 0)),
            pl.BlockSpec((tm, LANES), lambda i: (i, 0)),
            pl.BlockSpec((tm, LANES), lambda i: (i, 0)),
        ],
        out_specs=[pl.BlockSpec((tm, kv_lora), lambda i: (i, 0)), pl.BlockSpec((tm, LANES), lambda i: (i, 0))],
        out_shape=[jax.ShapeDtypeStruct((m, kv_lora), F32), jax.ShapeDtypeStruct((m, LANES), F32)],
        compiler_params=_params(1),
        name="lat_kr",
    )(z, z, kv_norm.reshape(1, kv_lora), cos_t, sin_t)


def _qproj_kernel(cq_ref, w_ref, g_ref, cos_ref, sin_ref, o_ref, os_ref, *, heads, nope, half, scale, sample_tile0):
    width = nope + LANES
    q = _dot(_rms(cq_ref[...], g_ref[...]).astype(w_ref.dtype), w_ref[...]) * scale
    cos_t, sin_t = cos_ref[...], sin_ref[...]
    parts = []
    for h in range(heads):
        q_nope = q[:, h * width:h * width + nope].astype(o_ref.dtype)
        q_rope = _rope_tile(q[:, h * width + nope:(h + 1) * width], cos_t, sin_t, half).astype(o_ref.dtype)
        parts.append((q_nope, q_rope))
        o_ref[h, :, :nope] = q_nope
        o_ref[h, :, nope:] = q_rope

    @pl.when(pl.program_id(0) >= sample_tile0)
    def _():
        for h, (q_nope, q_rope) in enumerate(parts):
            os_ref[:, h * width:h * width + nope] = q_nope
            os_ref[:, h * width + nope:(h + 1) * width] = q_rope


def _q_proj(z, w_uq_p, q_norm, cos_t, sin_t, *, off_cq, q_lora, heads, nope, half, scale, sample_row0, tm=512):
    m = z.shape[0]
    width = nope + LANES
    assert sample_row0 % tm == 0
    tile0 = sample_row0 // tm
    return pl.pallas_call(
        functools.partial(_qproj_kernel, heads=heads, nope=nope, half=half, scale=scale, sample_tile0=tile0),
        grid=(m // tm,),
        in_specs=[
            pl.BlockSpec((tm, q_lora), lambda i: (i, off_cq // q_lora)),
            pl.BlockSpec((q_lora, heads * width), lambda i: (0, 0)),
            pl.BlockSpec((1, q_lora), lambda i: (0, 0)),
            pl.BlockSpec((tm, LANES), lambda i: (i, 0)),
            pl.BlockSpec((tm, LANES), lambda i: (i, 0)),
        ],
        out_specs=[
            pl.BlockSpec((heads, tm, width), lambda i: (0, i, 0)),
            pl.BlockSpec((tm, heads * width), lambda i: (jnp.maximum(i - tile0, 0), 0)),
        ],
        out_shape=[
            jax.ShapeDtypeStruct((heads, m, width), BF16),
            jax.ShapeDtypeStruct((m - sample_row0, heads * width), BF16),
        ],
        compiler_params=_params(1),
        name="q_proj",
    )(z, w_uq_p, q_norm.reshape(1, q_lora), cos_t, sin_t)


def _kvproj_kernel(lat_ref, wk_ref, wv_ref, kr_ref, k_ref, v_ref, *, heads, nope):
    lat = lat_ref[...].astype(wk_ref.dtype)
    k = _dot(lat, wk_ref[...])
    kr = kr_ref[...].astype(k_ref.dtype)
    for h in range(heads):
        k_ref[h, :, :nope] = k[:, h * nope:(h + 1) * nope].astype(k_ref.dtype)
        k_ref[h, :, nope:] = kr
    v_ref[...] = _dot(lat, wv_ref[...]).astype(v_ref.dtype)


def _kv_proj(lat, w_uk, w_uv, kr_pad, *, rows, heads, nope, tm=512):
    kv_lora = lat.shape[1]
    nv = w_uv.shape[1]
    return pl.pallas_call(
        functools.partial(_kvproj_kernel, heads=heads, nope=nope),
        grid=(rows // tm,),
        in_specs=[
            pl.BlockSpec((tm, kv_lora), lambda i: (i, 0)),
            pl.BlockSpec((kv_lora, heads * nope), lambda i: (0, 0)),
            pl.BlockSpec((kv_lora, nv), lambda i: (0, 0)),
            pl.BlockSpec((tm, LANES), lambda i: (i, 0)),
        ],
        out_specs=[
            pl.BlockSpec((heads, tm, nope + LANES), lambda i: (0, i, 0)),
            pl.BlockSpec((tm, nv), lambda i: (i, 0)),
        ],
        out_shape=[
            jax.ShapeDtypeStruct((heads, rows, nope + LANES), BF16),
            jax.ShapeDtypeStruct((rows, nv), BF16),
        ],
        compiler_params=_params(1),
        name="kv_proj",
    )(lat, w_uk, w_uv, kr_pad)


def _flash_kernel(q_ref, k_ref, v_ref, buf_ref, o_ref, *, tq, tk):
    del buf_ref
    qi = pl.program_id(2)
    dv = v_ref.shape[1]

    def step(kb, carry, row0, masked):
        m, l, acc = carry
        start = pl.multiple_of(kb * tk, tk)
        k = k_ref[0, pl.ds(start, tk), :]
        v = v_ref[pl.ds(start, tk), :]
        s = _dot_nt(q_ref[0, row0:, :], k)
        if masked:
            row = lax.broadcasted_iota(jnp.int32, s.shape, 0)
            col = lax.broadcasted_iota(jnp.int32, s.shape, 1)
            s = jnp.where(col <= row, s, NEG)
        m_new = jnp.maximum(m, jnp.max(s, axis=-1, keepdims=True))
        alpha = jnp.exp(m - m_new)
        p = jnp.exp(s - m_new)
        l = l * alpha + jnp.sum(p, axis=-1, keepdims=True)
        acc = acc * alpha + _dot(p.astype(v.dtype), v)
        return m_new, l, acc

    n_full = qi * (tq // tk)
    carry = (jnp.full((tq, 1), NEG, F32), jnp.zeros((tq, 1), F32), jnp.zeros((tq, dv), F32))
    carry = lax.fori_loop(0, n_full, functools.partial(step, row0=0, masked=False), carry)
    n_diag = tq // tk
    for j in range(n_diag):
        row0 = j * tk
        m, l, acc = step(n_full + j, carry, row0, True)
        o_ref[row0:row0 + tk, :] = (acc[:tk] / l[:tk]).astype(o_ref.dtype)
        if j + 1 < n_diag:
            carry = (m[tk:], l[tk:], acc[tk:])


def _flash(q_all, k_all, v_all, buf, *, batch, seq, heads, dv, tq=1024, tk=512):
    nq = seq // tq
    width = q_all.shape[2]
    assert tq % tk == 0
    return pl.pallas_call(
        functools.partial(_flash_kernel, tq=tq, tk=tk),
        grid=(batch, heads, nq),
        in_specs=[
            pl.BlockSpec((1, tq, width), lambda b, h, i: (h, b * nq + i, 0)),
            pl.BlockSpec((1, seq, width), lambda b, h, i: (h, b, 0)),
            pl.BlockSpec((seq, dv), lambda b, h, i: (b, h)),
            pl.BlockSpec(memory_space=pl.ANY),
        ],
        out_specs=pl.BlockSpec((tq, dv), lambda b, h, i: (b * nq + i, h)),
        out_shape=jax.ShapeDtypeStruct(buf.shape, buf.dtype),
        input_output_aliases={3: 0},
        compiler_params=_params(3),
        name="flash",
    )(q_all, k_all, v_all, buf)


def _qlat_kernel(q_ref, w_ref, o_ref, *, nope):
    o_ref[...] = _dot_nt(q_ref[:, :nope], w_ref[...])


def _q_lat(q_tok, w_uk, *, heads, nope):
    rows = q_tok.shape[0]
    width = q_tok.shape[1] // heads
    kv_lora = w_uk.shape[0]
    return pl.pallas_call(
        functools.partial(_qlat_kernel, nope=nope),
        grid=(heads,),
        in_specs=[
            pl.BlockSpec((rows, width), lambda h: (0, h)),
            pl.BlockSpec((kv_lora, nope), lambda h: (0, h)),
        ],
        out_specs=pl.BlockSpec((rows, kv_lora), lambda h: (0, h)),
        out_shape=jax.ShapeDtypeStruct((rows, heads * kv_lora), F32),
        compiler_params=_params(1),
        name="q_lat",
    )(q_tok, w_uk)


def _ouv_kernel(x_ref, w_ref, buf_ref, o_ref):
    del buf_ref
    o_ref[...] = _dot(x_ref[...].astype(w_ref.dtype), w_ref[...]).astype(o_ref.dtype)


def _o_uv(o_lat, w_uv, buf, *, heads, dv, row0):
    rows = o_lat.shape[0]
    kv_lora = o_lat.shape[1] // heads
    assert row0 % rows == 0
    return pl.pallas_call(
        _ouv_kernel,
        grid=(heads,),
        in_specs=[
            pl.BlockSpec((rows, kv_lora), lambda h: (0, h)),
            pl.BlockSpec((kv_lora, dv), lambda h: (0, h)),
            pl.BlockSpec(memory_space=pl.ANY),
        ],
        out_specs=pl.BlockSpec((rows, dv), lambda h: (row0 // rows, h)),
        out_shape=jax.ShapeDtypeStruct(buf.shape, buf.dtype),
        input_output_aliases={2: 0},
        compiler_params=_params(1),
        name="o_uv",
    )(o_lat, w_uv, buf)


def _paged_kernel(pt_ref, ql_ref, qr_ref, tok_ref, nl_ref, nr_ref, cl_hbm, cr_hbm, o_ref, bl_ref, br_ref, sem_ref,
                  *, n_seq, n_groups, group, page, slots, t_new):
    b = pl.program_id(0)
    total = n_seq * n_groups
    ahead = slots - 1

    def copies(t):
        seq = lax.div(t, n_groups)
        g = lax.rem(t, n_groups)
        slot = lax.rem(t, slots)
        out = []
        for i in range(group):
            pid = pt_ref[seq, g * group + i]
            keys = pl.ds(i * page, page)
            out.append(pltpu.make_async_copy(cl_hbm.at[pid], bl_ref.at[slot, keys], sem_ref.at[0, slot, i]))
            out.append(pltpu.make_async_copy(cr_hbm.at[pid], br_ref.at[slot, i], sem_ref.at[1, slot, i]))
        return out

    @pl.when(b == 0)
    def _():
        for t0 in range(min(ahead, total)):
            for c in copies(jnp.int32(t0)):
                c.start()

    ql = ql_ref[0]
    qr = qr_ref[0]
    ql_lo = ql.astype(BF16)
    qr_lo = qr.astype(BF16)
    rows = ql.shape[0]
    kv = ql.shape[1]

    def body(g, carry):
        m, l, acc = carry
        t = b * n_groups + g

        @pl.when(t + ahead < total)
        def _():
            for c in copies(t + ahead):
                c.start()

        for c in copies(t):
            c.wait()
        slot = lax.rem(t, slots)
        c_blk = bl_ref[slot].astype(BF16)
        s_rope = jnp.concatenate([_dot(qr_lo, br_ref[slot, i].astype(BF16)) for i in range(group)], axis=1)
        s = _dot_nt(ql_lo, c_blk) + s_rope
        m_new = jnp.maximum(m, jnp.max(s, axis=-1, keepdims=True))
        alpha = jnp.exp(m - m_new)
        p = jnp.exp(s - m_new)
        l = l * alpha + jnp.sum(p, axis=-1, keepdims=True)
        acc = acc * alpha + _dot(p.astype(BF16), c_blk)
        return m_new, l, acc

    init = (jnp.full((rows, 1), NEG, F32), jnp.zeros((rows, 1), F32), jnp.zeros((rows, kv), F32))
    m, l, acc = lax.fori_loop(0, n_groups, body, init)

    t_of_row = tok_ref[...]
    s_new = []
    for j in range(t_new):
        sj = (jnp.sum(ql * nl_ref[0, j:j + 1, :], axis=-1, keepdims=True)
              + jnp.sum(qr * nr_ref[0, j:j + 1, :], axis=-1, keepdims=True))
        s_new.append(jnp.where(t_of_row >= j, sj, NEG))
    m_new = m
    for sj in s_new:
        m_new = jnp.maximum(m_new, sj)
    alpha = jnp.exp(m - m_new)
    l = l * alpha
    acc = acc * alpha
    for j, sj in enumerate(s_new):
        pj = jnp.exp(sj - m_new)
        l = l + pj
        acc = acc + pj * nl_ref[0, j:j + 1, :]
    o_ref[0] = acc / l


def _paged_attention(page_table, q_lat, q_rope, new_lat, new_kr, cache_lat, cache_kr_t, *, t_new):
    nb, rows, kv = q_lat.shape
    rope = q_rope.shape[2]
    n_pages = page_table.shape[1]
    page = cache_lat.shape[1]
    group = math.gcd(PAGE_GROUP, n_pages)
    n_groups = n_pages // group
    slots = PAGE_SLOTS
    grid_spec = pltpu.PrefetchScalarGridSpec(
        num_scalar_prefetch=1,
        grid=(nb,),
        in_specs=[
            pl.BlockSpec((1, rows, kv), lambda b, pt: (b, 0, 0)),
            pl.BlockSpec((1, rows, rope), lambda b, pt: (b, 0, 0)),
            pl.BlockSpec((rows, 1), lambda b, pt: (0, 0)),
            pl.BlockSpec((1, t_new, kv), lambda b, pt: (b, 0, 0)),
            pl.BlockSpec((1, t_new, rope), lambda b, pt: (b, 0, 0)),
            pl.BlockSpec(memory_space=pl.ANY),
            pl.BlockSpec(memory_space=pl.ANY),
        ],
        out_specs=pl.BlockSpec((1, rows, kv), lambda b, pt: (b, 0, 0)),
        scratch_shapes=[
            pltpu.VMEM((slots, group * page, kv), F32),
            pltpu.VMEM((slots, group, rope, page), F32),
            pltpu.SemaphoreType.DMA((2, slots, group)),
        ],
    )
    tok = (jnp.arange(rows, dtype=jnp.int32) // (rows // t_new)).reshape(rows, 1)
    return pl.pallas_call(
        functools.partial(_paged_kernel, n_seq=nb, n_groups=n_groups, group=group, page=page, slots=slots,
                          t_new=t_new),
        grid_spec=grid_spec,
        out_shape=jax.ShapeDtypeStruct((nb, rows, kv), F32),
        compiler_params=_params(1),
        name="paged_attn",
    )(page_table, q_lat, q_rope, tok, new_lat, new_kr, cache_lat, cache_kr_t)


def _pad_rows(x, rows):
    if x.shape[0] == rows:
        return x
    return jnp.concatenate([x, jnp.zeros((rows - x.shape[0], x.shape[1]), x.dtype)], axis=0)


def _hgrn_kernel(hq_ref, hf_ref, hi_ref, hg_ref, lb_ref, gn_ref, buf_ref, o_ref, so_ref, st_ref,
                 *, heads, dk, chunk, n_chunks):
    del buf_ref
    @pl.when(pl.program_id(2) == 0)
    def _():
        st_ref[...] = jnp.zeros_like(st_ref)

    keys = max(chunk, LANES)
    row = lax.broadcasted_iota(jnp.int32, (chunk, keys), 0)
    col = lax.broadcasted_iota(jnp.int32, (chunk, keys), 1)
    causal = col <= row
    row_id = lax.broadcasted_iota(jnp.int32, (chunk, dk), 0)

    def body(c, carry):
        r0 = pl.multiple_of(c * chunk, chunk)
        for h in range(heads):
            cols = slice(h * dk, (h + 1) * dk)
            lb = lb_ref[:, cols]
            q = jax.nn.silu(hq_ref[pl.ds(r0, chunk), cols]) * dk ** -0.5
            f = lb + (1.0 - lb) * jax.nn.sigmoid(hf_ref[pl.ds(r0, chunk), cols])
            logf = jnp.log(f)
            k = 1.0 - f
            v = hi_ref[pl.ds(r0, chunk), cols]
            bsum = logf
            shift = 1
            while shift < chunk:
                bsum = bsum + jnp.where(row_id >= shift, pltpu.roll(bsum, shift, 0), 0.0)
                shift *= 2
            b_last = bsum[chunk - 1:chunk, :]
            q_dec = q * jnp.exp(bsum)
            k_inv = k * jnp.exp(-bsum)
            v_pad = _pad_rows(v, keys)
            att = jnp.where(causal, _dot_nt(q_dec, _pad_rows(k_inv, keys)), 0.0)
            st = st_ref[h]
            o = _dot(att, v_pad) + _dot_nt(q_dec, st)
            k_end = k * jnp.exp(b_last - bsum)
            st_ref[h] = st * jnp.exp(b_last) + _dot_tn(v_pad, _pad_rows(k_end, keys))
            o = _rms(o, gn_ref[...]) * jax.nn.silu(hg_ref[pl.ds(r0, chunk), cols])
            o_ref[pl.ds(r0, chunk), cols] = o.astype(o_ref.dtype)
        return carry

    lax.fori_loop(0, n_chunks, body, 0)

    @pl.when(pl.program_id(2) == pl.num_programs(2) - 1)
    def _():
        for h in range(heads):
            so_ref[0, h] = st_ref[h].T


def _hgrn(z, lb, hg_norm, buf, *, offs, n_seq, rows, row_block, heads_total, heads, dk, chunk):
    hw = heads * dk
    n_hb = heads_total // heads
    n_rb = rows // row_block
    n_chunks = row_block // chunk

    def zspec(off):
        return pl.BlockSpec((row_block, hw), lambda s, j, r: (s * n_rb + r, off // hw + j))

    return pl.pallas_call(
        functools.partial(_hgrn_kernel, heads=heads, dk=dk, chunk=chunk, n_chunks=n_chunks),
        grid=(n_seq, n_hb, n_rb),
        in_specs=[zspec(o) for o in offs] + [
            pl.BlockSpec((1, hw), lambda s, j, r: (0, j)),
            pl.BlockSpec((1, dk), lambda s, j, r: (0, 0)),
            pl.BlockSpec(memory_space=pl.ANY),
        ],
        out_specs=[
            pl.BlockSpec((row_block, hw), lambda s, j, r: (s * n_rb + r, j)),
            pl.BlockSpec((1, heads, dk, dk), lambda s, j, r: (s, j, 0, 0)),
        ],
        out_shape=[
            jax.ShapeDtypeStruct(buf.shape, buf.dtype),
            jax.ShapeDtypeStruct((n_seq, heads_total, dk, dk), F32),
        ],
        input_output_aliases={6: 0},
        scratch_shapes=[pltpu.VMEM((heads, dk, dk), F32)],
        compiler_params=_params(3),
        name="hgrn",
    )(z, z, z, z, lb.reshape(1, -1), hg_norm.reshape(1, dk), buf)


def _hgrn_step_kernel(hq_ref, hf_ref, hi_ref, hg_ref, lb_ref, gn_ref, s0_ref, buf_ref, o_ref, so_ref,
                      *, heads, dk, n_seq, t_new):
    del buf_ref
    rows = n_seq * t_new
    for h in range(heads):
        cols = slice(h * dk, (h + 1) * dk)
        lb = lb_ref[:, cols]
        q = jax.nn.silu(hq_ref[:, cols]) * dk ** -0.5
        f = lb + (1.0 - lb) * jax.nn.sigmoid(hf_ref[:, cols])
        v = hi_ref[:, cols]
        row_id = lax.broadcasted_iota(jnp.int32, (rows, dk), 0)
        o = jnp.zeros((rows, dk), F32)
        for s in range(n_seq):
            st = s0_ref[s, h]
            for t in range(t_new):
                r = s * t_new + t
                f_col = jnp.broadcast_to(f[r:r + 1, :], (dk, dk)).T
                q_col = jnp.broadcast_to(q[r:r + 1, :], (dk, dk)).T
                st = st * f_col + (1.0 - f_col) * v[r:r + 1, :]
                o = jnp.where(row_id == r, jnp.sum(st * q_col, axis=0, keepdims=True), o)
            so_ref[s, h] = st
        o_ref[:, cols] = (_rms(o, gn_ref[...]) * jax.nn.silu(hg_ref[:, cols])).astype(o_ref.dtype)


def _hgrn_step(z, lb, hg_norm, s0, buf, *, offs, row0, n_seq_total, t_new, heads, dk):
    n_seq = 2 * SUBLANES // t_new
    rows = n_seq * t_new
    hw = heads * dk
    assert rows == 2 * SUBLANES and row0 % rows == 0 and n_seq_total % n_seq == 0
    rb0 = row0 // rows

    def zspec(off):
        return pl.BlockSpec((rows, hw), lambda i: (rb0 + i, off // hw))

    return pl.pallas_call(
        functools.partial(_hgrn_step_kernel, heads=heads, dk=dk, n_seq=n_seq, t_new=t_new),
        grid=(n_seq_total // n_seq,),
        in_specs=[zspec(o) for o in offs] + [
            pl.BlockSpec((1, hw), lambda i: (0, 0)),
            pl.BlockSpec((1, dk), lambda i: (0, 0)),
            pl.BlockSpec((n_seq, heads, dk, dk), lambda i: (i, 0, 0, 0)),
            pl.BlockSpec(memory_space=pl.ANY),
        ],
        out_specs=[
            pl.BlockSpec((rows, hw), lambda i: (rb0 + i, 0)),
            pl.BlockSpec((n_seq, heads, dk, dk), lambda i: (i, 0, 0, 0)),
        ],
        out_shape=[
            jax.ShapeDtypeStruct(buf.shape, buf.dtype),
            jax.ShapeDtypeStruct(s0.shape, F32),
        ],
        input_output_aliases={7: 0},
        compiler_params=_params(1),
        name="hgrn_step",
    )(z, z, z, z, lb.reshape(1, -1), hg_norm.reshape(1, dk), s0, buf)


def _gated_kernel(a_ref, b_ref, ga_ref, gb_ref, wa_ref, wb_ref, o_ref):
    o_ref[...] = (jax.nn.sigmoid(ga_ref[...]) * _dot(a_ref[...], wa_ref[...])
                  + jax.nn.sigmoid(gb_ref[...]) * _dot(b_ref[...], wb_ref[...])).astype(o_ref.dtype)


def _gated_merge(a, b, z, w_a, w_b, *, off_ga, off_gb, tm=512, tn=512):
    m, ka = a.shape
    kb = b.shape[1]
    n = w_a.shape[1]
    tn = _tile(n, tn)
    return pl.pallas_call(
        _gated_kernel,
        grid=(m // tm, n // tn),
        in_specs=[
            pl.BlockSpec((tm, ka), lambda i, j: (i, 0)),
            pl.BlockSpec((tm, kb), lambda i, j: (i, 0)),
            pl.BlockSpec((tm, tn), lambda i, j: (i, off_ga // tn + j)),
            pl.BlockSpec((tm, tn), lambda i, j: (i, off_gb // tn + j)),
            pl.BlockSpec((ka, tn), lambda i, j: (0, j)),
            pl.BlockSpec((kb, tn), lambda i, j: (0, j)),
        ],
        out_specs=pl.BlockSpec((tm, tn), lambda i, j: (i, j)),
        out_shape=jax.ShapeDtypeStruct((m, n), a.dtype),
        compiler_params=_params(2),
        name="gated_merge",
    )(a, b, z, z, w_a, w_b)


def _cross_kernel(q_ref, k_ref, v_ref, o_ref, *, heads, dh):
    for h in range(heads):
        cols = slice(h * dh, (h + 1) * dh)
        s = _dot_nt(q_ref[0, :, cols], k_ref[0, :, cols]) * dh ** -0.5
        p = jnp.exp(s - jnp.max(s, axis=-1, keepdims=True))
        p = p / jnp.sum(p, axis=-1, keepdims=True)
        o_ref[0, :, cols] = _dot(p, v_ref[0, :, cols])


def _cross_attention(q, k, v, *, heads, groups_per_mem):
    n, r, w = q.shape
    mem = k.shape[1]
    return pl.pallas_call(
        functools.partial(_cross_kernel, heads=heads, dh=w // heads),
        grid=(n,),
        in_specs=[
            pl.BlockSpec((1, r, w), lambda i: (i, 0, 0)),
            pl.BlockSpec((1, mem, w), lambda i: (i // groups_per_mem, 0, 0)),
            pl.BlockSpec((1, mem, w), lambda i: (i // groups_per_mem, 0, 0)),
        ],
        out_specs=pl.BlockSpec((1, r, w), lambda i: (i, 0, 0)),
        out_shape=jax.ShapeDtypeStruct((n, r, w), F32),
        compiler_params=_params(1),
        name="cross_attn",
    )(q, k, v)


def _cross_step_kernel(q_ref, k_ref, v_ref, o_ref, *, n_seq, heads, dh):
    rows = q_ref.shape[1]
    keys = k_ref.shape[1]
    row = lax.broadcasted_iota(jnp.int32, (heads * rows, keys), 0)
    col_head = jnp.bitwise_and(lax.broadcasted_iota(jnp.int32, (heads * rows, keys), 1), heads - 1)
    own = row < 0
    for h in range(heads):
        own = own | ((row >= h * rows) & (row < (h + 1) * rows) & (col_head == h))
    for s in range(n_seq):
        q = jnp.concatenate([q_ref[s, :, h * dh:(h + 1) * dh] for h in range(heads)], axis=0)
        sc = jnp.where(own, _dot_nt(q, k_ref[s]) * dh ** -0.5, NEG)
        p = jnp.exp(sc - jnp.max(sc, axis=-1, keepdims=True))
        p = p / jnp.sum(p, axis=-1, keepdims=True)
        o = _dot(p, v_ref[s])
        for h in range(heads):
            o_ref[s, :, h * dh:(h + 1) * dh] = o[h * rows:(h + 1) * rows, :]


def _cross_attention_step(q, k, v, *, heads, n_seq=4):
    b, rows, w = q.shape
    keys, dh = k.shape[1], k.shape[2]
    assert heads & (heads - 1) == 0 and b % n_seq == 0
    return pl.pallas_call(
        functools.partial(_cross_step_kernel, n_seq=n_seq, heads=heads, dh=dh),
        grid=(b // n_seq,),
        in_specs=[
            pl.BlockSpec((n_seq, rows, w), lambda i: (i, 0, 0)),
            pl.BlockSpec((n_seq, keys, dh), lambda i: (i, 0, 0)),
            pl.BlockSpec((n_seq, keys, dh), lambda i: (i, 0, 0)),
        ],
        out_specs=pl.BlockSpec((n_seq, rows, w), lambda i: (i, 0, 0)),
        out_shape=jax.ShapeDtypeStruct((b, rows, w), F32),
        compiler_params=_params(1),
        name="cross_attn_step",
    )(q, k, v)


def _router_kernel(x_ref, w_ref, g_ref, lg_ref, h_ref):
    h = _rms(x_ref[...], g_ref[...])
    lg_ref[...] = _dot_nt(h, w_ref[...])
    h_ref[...] = h


def _router(x, w_nk, gain, *, tm=512):
    m, d = x.shape
    n = w_nk.shape[0]
    return pl.pallas_call(
        _router_kernel,
        grid=(m // tm,),
        in_specs=[
            pl.BlockSpec((tm, d), lambda i: (i, 0)),
            pl.BlockSpec((n, d), lambda i: (0, 0)),
            pl.BlockSpec((1, d), lambda i: (0, 0)),
        ],
        out_specs=[pl.BlockSpec((tm, n), lambda i: (i, 0)), pl.BlockSpec((tm, d), lambda i: (i, 0))],
        out_shape=[jax.ShapeDtypeStruct((m, n), F32), jax.ShapeDtypeStruct((m, d), F32)],
        compiler_params=_params(1),
        name="router",
    )(x, w_nk, gain.reshape(1, d))


def _moe_kernel(be_ref, na_ref, h_ref, gate_ref, wgu_ref, wdn_ref, o_ref, *, d_expert):
    @pl.when(pl.program_id(0) < na_ref[0])
    def _():
        gu = _dot(h_ref[...].astype(BF16), wgu_ref[0].astype(BF16))
        act = jax.nn.silu(gu[:, :d_expert]) * gu[:, d_expert:]
        o_ref[...] = _dot(act.astype(BF16), wdn_ref[0].astype(BF16)) * gate_ref[...]

    @pl.when(pl.program_id(0) >= na_ref[0])
    def _():
        o_ref[...] = jnp.zeros_like(o_ref)


def _moe_ffn(block_expert, n_active, h_sorted, row_gate, w_gu, w_dn, *, block):
    n_rows, d = h_sorted.shape
    n_blocks = n_rows // block
    d_expert = w_dn.shape[1]

    def row_block(i, be, na):
        return (jnp.minimum(i, na[0] - 1), 0)

    grid_spec = pltpu.PrefetchScalarGridSpec(
        num_scalar_prefetch=2,
        grid=(n_blocks,),
        in_specs=[
            pl.BlockSpec((block, d), row_block),
            pl.BlockSpec((block, 1), row_block),
            pl.BlockSpec((1, d, 2 * d_expert), lambda i, be, na: (be[i], 0, 0)),
            pl.BlockSpec((1, d_expert, d), lambda i, be, na: (be[i], 0, 0)),
        ],
        out_specs=pl.BlockSpec((block, d), lambda i, be, na: (i, 0)),
    )
    return pl.pallas_call(
        functools.partial(_moe_kernel, d_expert=d_expert),
        grid_spec=grid_spec,
        out_shape=jax.ShapeDtypeStruct((n_rows, d), F32),
        compiler_params=_params(1),
        name="moe_ffn",
    )(block_expert, n_active, h_sorted, row_gate, w_gu, w_dn)


def _final_kernel(x_ref, ya_ref, yb_ref, g_ref, op_ref, os_ref, *, sample_tile0):
    y = _rms(x_ref[...] + (ya_ref[...] + yb_ref[...]), g_ref[...])

    @pl.when(pl.program_id(0) < sample_tile0)
    def _():
        op_ref[...] = y

    @pl.when(pl.program_id(0) >= sample_tile0)
    def _():
        os_ref[...] = y


def _final_norm(x, ya, yb, g, *, sample_row0, tm=512):
    m, d = x.shape
    assert sample_row0 % tm == 0 and m % tm == 0
    tile0 = sample_row0 // tm
    spec = pl.BlockSpec((tm, d), lambda i: (i, 0))
    return pl.pallas_call(
        functools.partial(_final_kernel, sample_tile0=tile0),
        grid=(m // tm,),
        in_specs=[spec, spec, spec, pl.BlockSpec((1, d), lambda i: (0, 0))],
        out_specs=[
            pl.BlockSpec((tm, d), lambda i: (jnp.minimum(i, tile0 - 1), 0)),
            pl.BlockSpec((tm, d), lambda i: (jnp.maximum(i - tile0, 0), 0)),
        ],
        out_shape=[jax.ShapeDtypeStruct((sample_row0, d), F32), jax.ShapeDtypeStruct((m - sample_row0, d), F32)],
        compiler_params=_params(1),
        name="final_norm",
    )(x, ya, yb, g.reshape(1, d))


def _top_k(x, k):
    lane = jnp.arange(x.shape[-1], dtype=jnp.int32)
    vals, idxs = [], []
    for _ in range(k):
        i = jnp.argmax(x, axis=-1).astype(jnp.int32)
        vals.append(jnp.max(x, axis=-1))
        idxs.append(i)
        x = jnp.where(lane == i[..., None], -jnp.inf, x)
    return jnp.stack(vals, axis=-1), jnp.stack(idxs, axis=-1)


def _route(logits, b_rg, b_re, n_groups, n_experts, moe_block):
    n = logits.shape[0]
    epg = n_experts // n_groups
    p_group = jax.nn.softmax(logits[:, :n_groups] + b_rg, axis=-1)
    g_w, g_idx = _top_k(p_group, 1)
    e_logits = (logits[:, n_groups:n_groups + n_experts] + b_re).reshape(n, n_groups, epg)
    in_group = jnp.take_along_axis(e_logits, g_idx[:, :, None], axis=1)[:, 0]
    top_logit, top_local = _top_k(in_group, TOP_K)
    gate = g_w * jax.nn.softmax(top_logit, axis=-1)
    expert_idx = g_idx * epg + top_local

    a = n * TOP_K
    flat_e = expert_idx.reshape(a).astype(jnp.int32)
    onehot = (flat_e[:, None] == jnp.arange(n_experts, dtype=jnp.int32)[None, :]).astype(jnp.int32)
    rank = jnp.sum((jnp.cumsum(onehot, axis=0) - onehot) * onehot, axis=1)
    counts = jnp.sum(onehot, axis=0)
    padded = (counts + moe_block - 1) // moe_block * moe_block
    pad_end = jnp.cumsum(padded)
    pad_start = pad_end - padded
    dest = pad_start[flat_e] + rank
    n_blocks = (a + n_experts * (moe_block - 1) + moe_block - 1) // moe_block
    n_rows = n_blocks * moe_block
    fill = jnp.stack([jnp.arange(n_rows, dtype=jnp.int32) % n, jnp.zeros((n_rows,), jnp.int32)], axis=1)
    vals = jnp.stack([jnp.arange(a, dtype=jnp.int32) // TOP_K,
                      lax.bitcast_convert_type(gate.reshape(a).astype(F32), jnp.int32)], axis=1)
    packed = fill.at[dest].set(vals)
    row_token = packed[:, 0]
    row_gate = lax.bitcast_convert_type(packed[:, 1], F32)
    n_active = (pad_end[-1] // moe_block).astype(jnp.int32)
    blk = jnp.minimum(jnp.arange(n_blocks, dtype=jnp.int32), n_active - 1) * moe_block
    block_expert = jnp.minimum(jnp.searchsorted(pad_end, blk, side='right'), n_experts - 1).astype(jnp.int32)
    return row_token, row_gate, dest.reshape(n, TOP_K), block_expert, n_active.reshape(1)


def kernel(x_prompt, mem_prompt, x_sample, cache_kv_latent, cache_k_rope, page_table, state_hgrn, cache_mem_k, cache_mem_v, norm_mix, w_in, q_norm, w_uq, kv_norm, w_uk, w_uv, lb_logits, hg_norm, w_out_mla, w_out_hgrn, w_out, norm_cross, norm_mem, w_cq, w_ckv, w_co, norm_ffn, w_router_group, b_router_group, w_router_expert, b_router_expert, w_expert_gate_up, w_expert_down, norm_final):
    batch, seq, d = x_prompt.shape
    dec_batch, dec_seq, _ = x_sample.shape
    depth = w_in.shape[0]
    assert depth == 1
    page = cache_kv_latent.shape[2]
    kv_lora = cache_kv_latent.shape[3]
    rope = cache_k_rope.shape[3]
    half = rope // 2
    past_len = page_table.shape[1] * page
    q_lora = q_norm.shape[1]
    hg_heads, hg_dk = state_hgrn.shape[2], state_hgrn.shape[3]
    hgw = hg_heads * hg_dk
    mla_heads = (w_uq.shape[2] - w_uk.shape[2]) // rope
    nope = w_uk.shape[2] // mla_heads
    v_head = w_uv.shape[2] // mla_heads
    mem_tokens, x_heads, x_dh = cache_mem_k.shape[2], cache_mem_k.shape[3], cache_mem_k.shape[4]
    xw = x_heads * x_dh
    n_groups = w_router_group.shape[2]
    n_experts = w_router_expert.shape[2]
    mp = batch * seq
    ms = dec_batch * dec_seq
    m_all = mp + ms
    mla_scale = (nope + rope) ** -0.5
    assert nope == LANES and 2 * half <= LANES and hg_dk == LANES

    n_front = q_lora + kv_lora + rope
    off_cq, off_ckv, off_kr = 0, q_lora, q_lora + kv_lora
    off_hq, off_hf, off_hi, off_hg = 0, hgw, 2 * hgw, 3 * hgw
    off_ga, off_gb = 4 * hgw, 4 * hgw + d
    w_in_t = jnp.swapaxes(w_in[0], 0, 1)
    w_back = w_in_t[n_front:].astype(BF16)
    w_front = jnp.concatenate([w_in_t[:n_front], jnp.zeros((LANES - rope, d), F32)], axis=0).astype(BF16)

    x0 = jnp.concatenate([x_prompt.reshape(mp, d), x_sample.reshape(ms, d)], axis=0)
    pos = jnp.concatenate([jnp.tile(jnp.arange(seq), batch), jnp.tile(past_len + jnp.arange(dec_seq), dec_batch)])
    cos_t, sin_t = _rope_tables(pos, half)

    tm_in = _tile(m_all, 1152, unit=2 * SUBLANES)
    z = _matmul(x0, w_back, w_is_nk=True, gain=norm_mix[0], tm=tm_in, tn_cap=1024, name="mm_in")
    z_front = _matmul(x0, w_front, w_is_nk=True, gain=norm_mix[0], tm=tm_in, tn_cap=1024, name="mm_in_front")
    lat, kr_pad = _lat_kr(z_front, kv_norm[0], cos_t, sin_t, off_ckv=off_ckv, off_kr=off_kr, kv_lora=kv_lora,
                          half=half)
    w_uq_p = jnp.pad(w_uq[0].reshape(q_lora, mla_heads, nope + rope), ((0, 0), (0, 0), (0, LANES - rope)))
    w_uq_p = w_uq_p.reshape(q_lora, mla_heads * (nope + LANES)).astype(BF16)
    q_all, q_tok = _q_proj(z_front, w_uq_p, q_norm[0], cos_t, sin_t, off_cq=off_cq, q_lora=q_lora, heads=mla_heads,
                           nope=nope, half=half, scale=mla_scale, sample_row0=mp)
    w_uk_lo, w_uv_lo = w_uk[0].astype(BF16), w_uv[0].astype(BF16)

    k_all, v_all = _kv_proj(lat, w_uk_lo, w_uv_lo, kr_pad, rows=mp, heads=mla_heads, nope=nope)
    a_all = _flash(q_all, k_all, v_all, jnp.zeros((m_all, mla_heads * v_head), BF16), batch=batch, seq=seq,
                   heads=mla_heads, dv=v_head)

    rows_s = dec_seq * mla_heads
    q_lat = _q_lat(q_tok, w_uk_lo, heads=mla_heads, nope=nope).reshape(dec_batch, rows_s, kv_lora)
    q_rope_s = q_tok.reshape(ms, mla_heads, nope + LANES)[:, :, nope:nope + rope].astype(F32)
    q_rope_s = q_rope_s.reshape(dec_batch, rows_s, rope)
    lat_s = lat[mp:].reshape(dec_batch, dec_seq, kv_lora)
    kr_s = kr_pad[mp:, :rope].reshape(dec_batch, dec_seq, rope)
    o_lat = _paged_attention(page_table, q_lat, q_rope_s, lat_s, kr_s,
                             cache_kv_latent.reshape(-1, page, kv_lora),
                             jnp.swapaxes(cache_k_rope, 2, 3).reshape(-1, rope, page),
                             t_new=dec_seq)
    a_all = _o_uv(o_lat.reshape(ms, mla_heads * kv_lora), w_uv_lo, a_all, heads=mla_heads, dv=v_head, row0=mp)

    lower_bounds = jnp.cumsum(jax.nn.softmax(lb_logits.astype(F32), axis=0), axis=0)
    lb = lower_bounds[0]
    chunk_p = math.gcd(HG_CHUNK, seq)
    hg_offs = (off_hq, off_hf, off_hi, off_hg)
    b_all, s_p = _hgrn(z, lb, hg_norm[0], jnp.zeros((m_all, hgw), BF16), offs=hg_offs, n_seq=batch, rows=seq,
                       row_block=_tile(seq, 512, unit=chunk_p), heads_total=hg_heads, heads=math.gcd(8, hg_heads),
                       dk=hg_dk, chunk=chunk_p)
    b_all, s_s = _hgrn_step(z, lb, hg_norm[0], state_hgrn[0], b_all, offs=hg_offs, row0=mp, n_seq_total=dec_batch,
                            t_new=dec_seq, heads=hg_heads, dk=hg_dk)

    merged = _gated_merge(a_all, b_all, z, w_out_mla[0].astype(BF16), w_out_hgrn[0].astype(BF16), off_ga=off_ga,
                          off_gb=off_gb, tm=tm_in)
    x1 = _matmul(merged, w_out[0].astype(BF16), res=x0, tm=tm_in, name="mm_out")

    kv_mem = _matmul(mem_prompt.reshape(batch * mem_tokens, d), w_ckv[0], gain=norm_mem[0], tm=batch * mem_tokens,
                     name="mem_kv")
    mk_p, mv_p = kv_mem[:, :xw].reshape(batch, mem_tokens, xw), kv_mem[:, xw:].reshape(batch, mem_tokens, xw)
    qx = _matmul(x1, w_cq[0], gain=norm_cross[0], name="cross_q")
    tq_x = 512
    ox_p = _cross_attention(qx[:mp].reshape(mp // tq_x, tq_x, xw), mk_p, mv_p, heads=x_heads,
                            groups_per_mem=seq // tq_x)
    rows_pad = -(-dec_seq // SUBLANES) * SUBLANES
    qx_s = jnp.pad(qx[mp:].reshape(dec_batch, dec_seq, xw), ((0, 0), (0, rows_pad - dec_seq), (0, 0)))
    ox_s = _cross_attention_step(qx_s, cache_mem_k.reshape(dec_batch, mem_tokens * x_heads, x_dh),
                                 cache_mem_v.reshape(dec_batch, mem_tokens * x_heads, x_dh), heads=x_heads,
                                 n_seq=math.gcd(4, dec_batch))
    ox = jnp.concatenate([ox_p.reshape(mp, xw), ox_s[:, :dec_seq].reshape(ms, xw)], axis=0)
    x2 = _matmul(ox, w_co[0], res=x1, name="cross_out")

    n_r = n_groups + n_experts
    w_r = jnp.concatenate([jnp.swapaxes(w_router_group[0], 0, 1), jnp.swapaxes(w_router_expert[0], 0, 1),
                           jnp.zeros((-n_r % LANES, d), F32)], axis=0)
    logits, h_ffn = _router(x2, w_r, norm_ffn[0])
    moe_block = 256

    def route(lg):
        return _route(lg, b_router_group[0], b_router_expert[0], n_groups, n_experts, moe_block)

    row_token, row_gate, dest, block_expert, n_active = route(logits)
    y_sorted = _moe_ffn(block_expert, n_active, h_ffn[row_token], row_gate[:, None],
                        w_expert_gate_up[0], w_expert_down[0], block=moe_block)
    y_p, y_s = _final_norm(x2, y_sorted[dest[:, 0]], y_sorted[dest[:, 1]], norm_final, sample_row0=mp)

    y_prompt = y_p.reshape(batch, seq, d)
    y_sample = y_s.reshape(dec_batch, dec_seq, d)
    n_prompt_pages = mp // page
    kv_latent_prompt = lat[:mp].reshape(depth, n_prompt_pages, page, kv_lora)
    k_rope_prompt = kr_pad[:mp, :rope].reshape(depth, n_prompt_pages, page, rope)
    state_hgrn_prompt = s_p[None]
    mem_k_prompt = mk_p.reshape(depth, batch, mem_tokens, x_heads, x_dh)
    mem_v_prompt = mv_p.reshape(depth, batch, mem_tokens, x_heads, x_dh)
    kv_latent_sample = lat_s[None]
    k_rope_sample = kr_s[None]
    state_hgrn_sample = s_s[None]
    return (y_prompt, y_sample, kv_latent_prompt, k_rope_prompt, state_hgrn_prompt, mem_k_prompt, mem_v_prompt,
            kv_latent_sample, k_rope_sample, state_hgrn_sample)
```

```python
import functools
import math

import jax
import jax.numpy as jnp
from jax import lax
from jax.experimental import pallas as pl
from jax.experimental.pallas import tpu as pltpu

F32 = jnp.float32
BF16 = jnp.bfloat16

EPS = 1e-6
NEG = -1e30
ROPE_THETA = 10000.0
TOP_K = 2
HG_CHUNK = 64
LANES = 128
SUBLANES = 8
VMEM_LIMIT = 56 * 1024 * 1024
PAGE_GROUP = 32
PAGE_SLOTS = 3


def _params(n_axes):
    return pltpu.CompilerParams(dimension_semantics=("arbitrary",) * n_axes, vmem_limit_bytes=VMEM_LIMIT)


def _tile(n, cap, unit=LANES):
    if n <= cap:
        return n
    t = cap - cap % unit
    while t >= unit:
        if n % t == 0:
            return t
        t -= unit
    raise ValueError(f"no tile for {n} under {cap}")


def _rms(x, g):
    return x * lax.rsqrt(jnp.mean(x * x, axis=-1, keepdims=True) + EPS) * g


def _dot(a, b):
    return jnp.dot(a, b, preferred_element_type=F32)


def _dot_nt(a, b):
    return lax.dot_general(a, b, (((1,), (1,)), ((), ())), preferred_element_type=F32)


def _dot_tn(a, b):
    return lax.dot_general(a, b, (((0,), (0,)), ((), ())), preferred_element_type=F32)


def _mm_kernel(*refs, has_gain, has_res, w_is_nk):
    it = iter(refs)
    x_ref, w_ref = next(it), next(it)
    g_ref = next(it) if has_gain else None
    r_ref = next(it) if has_res else None
    o_ref = next(it)
    if has_gain:
        xs_ref = next(it)

        @pl.when(pl.program_id(1) == 0)
        def _():
            xs_ref[...] = _rms(x_ref[...].astype(F32), g_ref[...]).astype(xs_ref.dtype)

        x = xs_ref[...]
    else:
        x = x_ref[...].astype(w_ref.dtype)
    acc = _dot_nt(x, w_ref[...]) if w_is_nk else _dot(x, w_ref[...])
    if has_res:
        acc = r_ref[...] + acc
    o_ref[...] = acc.astype(o_ref.dtype)


def _matmul(x, w, *, w_is_nk=False, gain=None, res=None, x_col=0, tm=512, tn_cap=512, out_dtype=F32, name="mm"):
    m = x.shape[0]
    n, k = w.shape if w_is_nk else w.shape[::-1]
    assert x_col % k == 0 and m % tm == 0
    tn = _tile(n, tn_cap)
    xb = x_col // k
    w_spec = pl.BlockSpec((tn, k), lambda i, j: (j, 0)) if w_is_nk else pl.BlockSpec((k, tn), lambda i, j: (0, j))
    in_specs = [pl.BlockSpec((tm, k), lambda i, j: (i, xb)), w_spec]
    args = [x, w]
    scratch = []
    if gain is not None:
        in_specs.append(pl.BlockSpec((1, k), lambda i, j: (0, 0)))
        args.append(gain.reshape(1, k).astype(F32))
        scratch.append(pltpu.VMEM((tm, k), w.dtype))
    if res is not None:
        in_specs.append(pl.BlockSpec((tm, tn), lambda i, j: (i, j)))
        args.append(res)
    return pl.pallas_call(
        functools.partial(_mm_kernel, has_gain=gain is not None, has_res=res is not None, w_is_nk=w_is_nk),
        grid=(m // tm, n // tn),
        in_specs=in_specs,
        out_specs=pl.BlockSpec((tm, tn), lambda i, j: (i, j)),
        out_shape=jax.ShapeDtypeStruct((m, n), out_dtype),
        scratch_shapes=scratch,
        compiler_params=_params(2),
        name=name,
    )(*args)


def _rope_tile(x, cos_t, sin_t, half):
    lane = lax.broadcasted_iota(jnp.int32, x.shape, 1)
    swapped = jnp.where(lane < half, pltpu.roll(x, LANES - half, 1), pltpu.roll(x, half, 1))
    return x * cos_t + swapped * sin_t


def _rope_tables(pos, half):
    inv = ROPE_THETA ** (-jnp.arange(half, dtype=F32) / half)
    ang = pos.astype(F32)[:, None] * inv[None, :]
    cos, sin = jnp.cos(ang), jnp.sin(ang)
    zeros = jnp.zeros((pos.shape[0], LANES - 2 * half), F32)
    return jnp.concatenate([cos, cos, zeros], axis=1), jnp.concatenate([-sin, sin, zeros], axis=1)


def _latkr_kernel(ckv_ref, kr_ref, g_ref, cos_ref, sin_ref, lat_ref, kro_ref, *, half):
    lat_ref[...] = _rms(ckv_ref[...], g_ref[...])
    kro_ref[...] = _rope_tile(kr_ref[...], cos_ref[...], sin_ref[...], half)


def _lat_kr(z, kv_norm, cos_t, sin_t, *, off_ckv, off_kr, kv_lora, half, tm=512):
    m = z.shape[0]
    return pl.pallas_call(
        functools.partial(_latkr_kernel, half=half),
        grid=(m // tm,),
        in_specs=[
            pl.BlockSpec((tm, kv_lora), lambda i: (i, off_ckv // kv_lora)),
            pl.BlockSpec((tm, LANES), lambda i: (i, off_kr // LANES)),
            pl.BlockSpec((1, kv_lora), lambda i: (0, 0)),
            pl.BlockSpec((tm, LANES), lambda i: (i, 0)),
            pl.BlockSpec((tm, LANES), lambda i: (i, 0)),
        ],
        out_specs=[pl.BlockSpec((tm, kv_lora), lambda i: (i, 0)), pl.BlockSpec((tm, LANES), lambda i: (i, 0))],
        out_shape=[jax.ShapeDtypeStruct((m, kv_lora), F32), jax.ShapeDtypeStruct((m, LANES), F32)],
        compiler_params=_params(1),
        name="lat_kr",
    )(z, z, kv_norm.reshape(1, kv_lora), cos_t, sin_t)


def _qproj_kernel(cq_ref, w_ref, g_ref, cos_ref, sin_ref, o_ref, os_ref, *, heads, nope, half, scale, sample_tile0):
    width = nope + LANES
    q = _dot(_rms(cq_ref[...], g_ref[...]).astype(w_ref.dtype), w_ref[...]) * scale
    cos_t, sin_t = cos_ref[...], sin_ref[...]
    parts = []
    for h in range(heads):
        q_nope = q[:, h * width:h * width + nope].astype(o_ref.dtype)
        q_rope = _rope_tile(q[:, h * width + nope:(h + 1) * width], cos_t, sin_t, half).astype(o_ref.dtype)
        parts.append((q_nope, q_rope))
        o_ref[h, :, :nope] = q_nope
        o_ref[h, :, nope:] = q_rope

    @pl.when(pl.program_id(0) >= sample_tile0)
    def _():
        for h, (q_nope, q_rope) in enumerate(parts):
            os_ref[:, h * width:h * width + nope] = q_nope
            os_ref[:, h * width + nope:(h + 1) * width] = q_rope


def _q_proj(z, w_uq_p, q_norm, cos_t, sin_t, *, off_cq, q_lora, heads, nope, half, scale, sample_row0, tm=512):
    m = z.shape[0]
    width = nope + LANES
    assert sample_row0 % tm == 0
    tile0 = sample_row0 // tm
    return pl.pallas_call(
        functools.partial(_qproj_kernel, heads=heads, nope=nope, half=half, scale=scale, sample_tile0=tile0),
        grid=(m // tm,),
        in_specs=[
            pl.BlockSpec((tm, q_lora), lambda i: (i, off_cq // q_lora)),
            pl.BlockSpec((q_lora, heads * width), lambda i: (0, 0)),
            pl.BlockSpec((1, q_lora), lambda i: (0, 0)),
            pl.BlockSpec((tm, LANES), lambda i: (i, 0)),
            pl.BlockSpec((tm, LANES), lambda i: (i, 0)),
        ],
        out_specs=[
            pl.BlockSpec((heads, tm, width), lambda i: (0, i, 0)),
            pl.BlockSpec((tm, heads * width), lambda i: (jnp.maximum(i - tile0, 0), 0)),
        ],
        out_shape=[
            jax.ShapeDtypeStruct((heads, m, width), BF16),
            jax.ShapeDtypeStruct((m - sample_row0, heads * width), BF16),
        ],
        compiler_params=_params(1),
        name="q_proj",
    )(z, w_uq_p, q_norm.reshape(1, q_lora), cos_t, sin_t)


def _kvproj_kernel(lat_ref, wk_ref, wv_ref, kr_ref, k_ref, v_ref, *, heads, nope):
    lat = lat_ref[...].astype(wk_ref.dtype)
    k = _dot(lat, wk_ref[...])
    kr = kr_ref[...].astype(k_ref.dtype)
    for h in range(heads):
        k_ref[h, :, :nope] = k[:, h * nope:(h + 1) * nope].astype(k_ref.dtype)
        k_ref[h, :, nope:] = kr
    v_ref[...] = _dot(lat, wv_ref[...]).astype(v_ref.dtype)


def _kv_proj(lat, w_uk, w_uv, kr_pad, *, rows, heads, nope, tm=512):
    kv_lora = lat.shape[1]
    nv = w_uv.shape[1]
    return pl.pallas_call(
        functools.partial(_kvproj_kernel, heads=heads, nope=nope),
        grid=(rows // tm,),
        in_specs=[
            pl.BlockSpec((tm, kv_lora), lambda i: (i, 0)),
            pl.BlockSpec((kv_lora, heads * nope), lambda i: (0, 0)),
            pl.BlockSpec((kv_lora, nv), lambda i: (0, 0)),
            pl.BlockSpec((tm, LANES), lambda i: (i, 0)),
        ],
        out_specs=[
            pl.BlockSpec((heads, tm, nope + LANES), lambda i: (0, i, 0)),
            pl.BlockSpec((tm, nv), lambda i: (i, 0)),
        ],
        out_shape=[
            jax.ShapeDtypeStruct((heads, rows, nope + LANES), BF16),
            jax.ShapeDtypeStruct((rows, nv), BF16),
        ],
        compiler_params=_params(1),
        name="kv_proj",
    )(lat, w_uk, w_uv, kr_pad)


def _flash_kernel(q_ref, k_ref, v_ref, buf_ref, o_ref, *, tq, tk):
    del buf_ref
    qi = pl.program_id(2)
    dv = v_ref.shape[1]

    def step(kb, carry, row0, masked):
        m, l, acc = carry
        start = pl.multiple_of(kb * tk, tk)
        k = k_ref[0, pl.ds(start, tk), :]
        v = v_ref[pl.ds(start, tk), :]
        s = _dot_nt(q_ref[0, row0:, :], k)
        if masked:
            row = lax.broadcasted_iota(jnp.int32, s.shape, 0)
            col = lax.broadcasted_iota(jnp.int32, s.shape, 1)
            s = jnp.where(col <= row, s, NEG)
        m_new = jnp.maximum(m, jnp.max(s, axis=-1, keepdims=True))
        alpha = jnp.exp(m - m_new)
        p = jnp.exp(s - m_new)
        l = l * alpha + jnp.sum(p, axis=-1, keepdims=True)
        acc = acc * alpha + _dot(p.astype(v.dtype), v)
        return m_new, l, acc

    n_full = qi * (tq // tk)
    carry = (jnp.full((tq, 1), NEG, F32), jnp.zeros((tq, 1), F32), jnp.zeros((tq, dv), F32))
    carry = lax.fori_loop(0, n_full, functools.partial(step, row0=0, masked=False), carry)
    n_diag = tq // tk
    for j in range(n_diag):
        row0 = j * tk
        m, l, acc = step(n_full + j, carry, row0, True)
        o_ref[row0:row0 + tk, :] = (acc[:tk] / l[:tk]).astype(o_ref.dtype)
        if j + 1 < n_diag:
            carry = (m[tk:], l[tk:], acc[tk:])


def _flash(q_all, k_all, v_all, buf, *, batch, seq, heads, dv, tq=1024, tk=512):
    nq = seq // tq
    width = q_all.shape[2]
    assert tq % tk == 0
    return pl.pallas_call(
        functools.partial(_flash_kernel, tq=tq, tk=tk),
        grid=(batch, heads, nq),
        in_specs=[
            pl.BlockSpec((1, tq, width), lambda b, h, i: (h, b * nq + i, 0)),
            pl.BlockSpec((1, seq, width), lambda b, h, i: (h, b, 0)),
            pl.BlockSpec((seq, dv), lambda b, h, i: (b, h)),
            pl.BlockSpec(memory_space=pl.ANY),
        ],
        out_specs=pl.BlockSpec((tq, dv), lambda b, h, i: (b * nq + i, h)),
        out_shape=jax.ShapeDtypeStruct(buf.shape, buf.dtype),
        input_output_aliases={3: 0},
        compiler_params=_params(3),
        name="flash",
    )(q_all, k_all, v_all, buf)


def _qlat_kernel(q_ref, w_ref, o_ref, *, nope):
    o_ref[...] = _dot_nt(q_ref[:, :nope], w_ref[...])


def _q_lat(q_tok, w_uk, *, heads, nope):
    rows = q_tok.shape[0]
    width = q_tok.shape[1] // heads
    kv_lora = w_uk.shape[0]
    return pl.pallas_call(
        functools.partial(_qlat_kernel, nope=nope),
        grid=(heads,),
        in_specs=[
            pl.BlockSpec((rows, width), lambda h: (0, h)),
            pl.BlockSpec((kv_lora, nope), lambda h: (0, h)),
        ],
        out_specs=pl.BlockSpec((rows, kv_lora), lambda h: (0, h)),
        out_shape=jax.ShapeDtypeStruct((rows, heads * kv_lora), F32),
        compiler_params=_params(1),
        name="q_lat",
    )(q_tok, w_uk)


def _ouv_kernel(x_ref, w_ref, buf_ref, o_ref):
    del buf_ref
    o_ref[...] = _dot(x_ref[...].astype(w_ref.dtype), w_ref[...]).astype(o_ref.dtype)


def _o_uv(o_lat, w_uv, buf, *, heads, dv, row0):
    rows = o_lat.shape[0]
    kv_lora = o_lat.shape[1] // heads
    assert row0 % rows == 0
    return pl.pallas_call(
        _ouv_kernel,
        grid=(heads,),
        in_specs=[
            pl.BlockSpec((rows, kv_lora), lambda h: (0, h)),
            pl.BlockSpec((kv_lora, dv), lambda h: (0, h)),
            pl.BlockSpec(memory_space=pl.ANY),
        ],
        out_specs=pl.BlockSpec((rows, dv), lambda h: (row0 // rows, h)),
        out_shape=jax.ShapeDtypeStruct(buf.shape, buf.dtype),
        input_output_aliases={2: 0},
        compiler_params=_params(1),
        name="o_uv",
    )(o_lat, w_uv, buf)


def _paged_kernel(pt_ref, ql_ref, qr_ref, tok_ref, nl_ref, nr_ref, cl_hbm, cr_hbm, o_ref, bl_ref, br_ref, sem_ref,
                  *, n_seq, n_groups, group, page, slots, t_new):
    b = pl.program_id(0)
    total = n_seq * n_groups
    ahead = slots - 1

    def copies(t):
        seq = lax.div(t, n_groups)
        g = lax.rem(t, n_groups)
        slot = lax.rem(t, slots)
        out = []
        for i in range(group):
            pid = pt_ref[seq, g * group + i]
            keys = pl.ds(i * page, page)
            out.append(pltpu.make_async_copy(cl_hbm.at[pid], bl_ref.at[slot, keys], sem_ref.at[0, slot, i]))
            out.append(pltpu.make_async_copy(cr_hbm.at[pid], br_ref.at[slot, i], sem_ref.at[1, slot, i]))
        return out

    @pl.when(b == 0)
    def _():
        for t0 in range(min(ahead, total)):
            for c in copies(jnp.int32(t0)):
                c.start()

    ql = ql_ref[0]
    qr = qr_ref[0]
    ql_lo = ql.astype(BF16)
    qr_lo = qr.astype(BF16)
    rows = ql.shape[0]
    kv = ql.shape[1]

    def body(g, carry):
        m, l, acc = carry
        t = b * n_groups + g

        @pl.when(t + ahead < total)
        def _():
            for c in copies(t + ahead):
                c.start()

        for c in copies(t):
            c.wait()
        slot = lax.rem(t, slots)
        c_blk = bl_ref[slot].astype(BF16)
        s_rope = jnp.concatenate([_dot(qr_lo, br_ref[slot, i].astype(BF16)) for i in range(group)], axis=1)
        s = _dot_nt(ql_lo, c_blk) + s_rope
        m_new = jnp.maximum(m, jnp.max(s, axis=-1, keepdims=True))
        alpha = jnp.exp(m - m_new)
        p = jnp.exp(s - m_new)
        l = l * alpha + jnp.sum(p, axis=-1, keepdims=True)
        acc = acc * alpha + _dot(p.astype(BF16), c_blk)
        return m_new, l, acc

    init = (jnp.full((rows, 1), NEG, F32), jnp.zeros((rows, 1), F32), jnp.zeros((rows, kv), F32))
    m, l, acc = lax.fori_loop(0, n_groups, body, init)

    t_of_row = tok_ref[...]
    s_new = []
    for j in range(t_new):
        sj = (jnp.sum(ql * nl_ref[0, j:j + 1, :], axis=-1, keepdims=True)
              + jnp.sum(qr * nr_ref[0, j:j + 1, :], axis=-1, keepdims=True))
        s_new.append(jnp.where(t_of_row >= j, sj, NEG))
    m_new = m
    for sj in s_new:
        m_new = jnp.maximum(m_new, sj)
    alpha = jnp.exp(m - m_new)
    l = l * alpha
    acc = acc * alpha
    for j, sj in enumerate(s_new):
        pj = jnp.exp(sj - m_new)
        l = l + pj
        acc = acc + pj * nl_ref[0, j:j + 1, :]
    o_ref[0] = acc / l


def _paged_attention(page_table, q_lat, q_rope, new_lat, new_kr, cache_lat, cache_kr_t, *, t_new):
    nb, rows, kv = q_lat.shape
    rope = q_rope.shape[2]
    n_pages = page_table.shape[1]
    page = cache_lat.shape[1]
    group = math.gcd(PAGE_GROUP, n_pages)
    n_groups = n_pages // group
    slots = PAGE_SLOTS
    grid_spec = pltpu.PrefetchScalarGridSpec(
        num_scalar_prefetch=1,
        grid=(nb,),
        in_specs=[
            pl.BlockSpec((1, rows, kv), lambda b, pt: (b, 0, 0)),
            pl.BlockSpec((1, rows, rope), lambda b, pt: (b, 0, 0)),
            pl.BlockSpec((rows, 1), lambda b, pt: (0, 0)),
            pl.BlockSpec((1, t_new, kv), lambda b, pt: (b, 0, 0)),
            pl.BlockSpec((1, t_new, rope), lambda b, pt: (b, 0, 0)),
            pl.BlockSpec(memory_space=pl.ANY),
            pl.BlockSpec(memory_space=pl.ANY),
        ],
        out_specs=pl.BlockSpec((1, rows, kv), lambda b, pt: (b, 0, 0)),
        scratch_shapes=[
            pltpu.VMEM((slots, group * page, kv), F32),
            pltpu.VMEM((slots, group, rope, page), F32),
            pltpu.SemaphoreType.DMA((2, slots, group)),
        ],
    )
    tok = (jnp.arange(rows, dtype=jnp.int32) // (rows // t_new)).reshape(rows, 1)
    return pl.pallas_call(
        functools.partial(_paged_kernel, n_seq=nb, n_groups=n_groups, group=group, page=page, slots=slots,
                          t_new=t_new),
        grid_spec=grid_spec,
        out_shape=jax.ShapeDtypeStruct((nb, rows, kv), F32),
        compiler_params=_params(1),
        name="paged_attn",
    )(page_table, q_lat, q_rope, tok, new_lat, new_kr, cache_lat, cache_kr_t)


def _pad_rows(x, rows):
    if x.shape[0] == rows:
        return x
    return jnp.concatenate([x, jnp.zeros((rows - x.shape[0], x.shape[1]), x.dtype)], axis=0)


def _hgrn_kernel(hq_ref, hf_ref, hi_ref, hg_ref, lb_ref, gn_ref, buf_ref, o_ref, so_ref, st_ref,
                 *, heads, dk, chunk, n_chunks):
    del buf_ref
    @pl.when(pl.program_id(2) == 0)
    def _():
        st_ref[...] = jnp.zeros_like(st_ref)

    keys = max(chunk, LANES)
    row = lax.broadcasted_iota(jnp.int32, (chunk, keys), 0)
    col = lax.broadcasted_iota(jnp.int32, (chunk, keys), 1)
    causal = col <= row
    row_id = lax.broadcasted_iota(jnp.int32, (chunk, dk), 0)

    def body(c, carry):
        r0 = pl.multiple_of(c * chunk, chunk)
        for h in range(heads):
            cols = slice(h * dk, (h + 1) * dk)
            lb = lb_ref[:, cols]
            q = jax.nn.silu(hq_ref[pl.ds(r0, chunk), cols]) * dk ** -0.5
            f = lb + (1.0 - lb) * jax.nn.sigmoid(hf_ref[pl.ds(r0, chunk), cols])
            logf = jnp.log(f)
            k = 1.0 - f
            v = hi_ref[pl.ds(r0, chunk), cols]
            bsum = logf
            shift = 1
            while shift < chunk:
                bsum = bsum + jnp.where(row_id >= shift, pltpu.roll(bsum, shift, 0), 0.0)
                shift *= 2
            b_last = bsum[chunk - 1:chunk, :]
            q_dec = q * jnp.exp(bsum)
            k_inv = k * jnp.exp(-bsum)
            v_pad = _pad_rows(v, keys)
            att = jnp.where(causal, _dot_nt(q_dec, _pad_rows(k_inv, keys)), 0.0)
            st = st_ref[h]
            o = _dot(att, v_pad) + _dot_nt(q_dec, st)
            k_end = k * jnp.exp(b_last - bsum)
            st_ref[h] = st * jnp.exp(b_last) + _dot_tn(v_pad, _pad_rows(k_end, keys))
            o = _rms(o, gn_ref[...]) * jax.nn.silu(hg_ref[pl.ds(r0, chunk), cols])
            o_ref[pl.ds(r0, chunk), cols] = o.astype(o_ref.dtype)
        return carry

    lax.fori_loop(0, n_chunks, body, 0)

    @pl.when(pl.program_id(2) == pl.num_programs(2) - 1)
    def _():
        for h in range(heads):
            so_ref[0, h] = st_ref[h].T


def _hgrn(z, lb, hg_norm, buf, *, offs, n_seq, rows, row_block, heads_total, heads, dk, chunk):
    hw = heads * dk
    n_hb = heads_total // heads
    n_rb = rows // row_block
    n_chunks = row_block // chunk

    def zspec(off):
        return pl.BlockSpec((row_block, hw), lambda s, j, r: (s * n_rb + r, off // hw + j))

    return pl.pallas_call(
        functools.partial(_hgrn_kernel, heads=heads, dk=dk, chunk=chunk, n_chunks=n_chunks),
        grid=(n_seq, n_hb, n_rb),
        in_specs=[zspec(o) for o in offs] + [
            pl.BlockSpec((1, hw), lambda s, j, r: (0, j)),
            pl.BlockSpec((1, dk), lambda s, j, r: (0, 0)),
            pl.BlockSpec(memory_space=pl.ANY),
        ],
        out_specs=[
            pl.BlockSpec((row_block, hw), lambda s, j, r: (s * n_rb + r, j)),
            pl.BlockSpec((1, heads, dk, dk), lambda s, j, r: (s, j, 0, 0)),
        ],
        out_shape=[
            jax.ShapeDtypeStruct(buf.shape, buf.dtype),
            jax.ShapeDtypeStruct((n_seq, heads_total, dk, dk), F32),
        ],
        input_output_aliases={6: 0},
        scratch_shapes=[pltpu.VMEM((heads, dk, dk), F32)],
        compiler_params=_params(3),
        name="hgrn",
    )(z, z, z, z, lb.reshape(1, -1), hg_norm.reshape(1, dk), buf)


def _hgrn_step_kernel(hq_ref, hf_ref, hi_ref, hg_ref, lb_ref, gn_ref, s0_ref, buf_ref, o_ref, so_ref,
                      *, heads, dk, n_seq, t_new):
    del buf_ref
    rows = n_seq * t_new
    for h in range(heads):
        cols = slice(h * dk, (h + 1) * dk)
        lb = lb_ref[:, cols]
        q = jax.nn.silu(hq_ref[:, cols]) * dk ** -0.5
        f = lb + (1.0 - lb) * jax.nn.sigmoid(hf_ref[:, cols])
        v = hi_ref[:, cols]
        row_id = lax.broadcasted_iota(jnp.int32, (rows, dk), 0)
        o = jnp.zeros((rows, dk), F32)
        for s in range(n_seq):
            st = s0_ref[s, h]
            for t in range(t_new):
                r = s * t_new + t
                f_col = jnp.broadcast_to(f[r:r + 1, :], (dk, dk)).T
                q_col = jnp.broadcast_to(q[r:r + 1, :], (dk, dk)).T
                st = st * f_col + (1.0 - f_col) * v[r:r + 1, :]
                o = jnp.where(row_id == r, jnp.sum(st * q_col, axis=0, keepdims=True), o)
            so_ref[s, h] = st
        o_ref[:, cols] = (_rms(o, gn_ref[...]) * jax.nn.silu(hg_ref[:, cols])).astype(o_ref.dtype)


def _hgrn_step(z, lb, hg_norm, s0, buf, *, offs, row0, n_seq_total, t_new, heads, dk):
    n_seq = 2 * SUBLANES // t_new
    rows = n_seq * t_new
    hw = heads * dk
    assert rows == 2 * SUBLANES and row0 % rows == 0 and n_seq_total % n_seq == 0
    rb0 = row0 // rows

    def zspec(off):
        return pl.BlockSpec((rows, hw), lambda i: (rb0 + i, off // hw))

    return pl.pallas_call(
        functools.partial(_hgrn_step_kernel, heads=heads, dk=dk, n_seq=n_seq, t_new=t_new),
        grid=(n_seq_total // n_seq,),
        in_specs=[zspec(o) for o in offs] + [
            pl.BlockSpec((1, hw), lambda i: (0, 0)),
            pl.BlockSpec((1, dk), lambda i: (0, 0)),
            pl.BlockSpec((n_seq, heads, dk, dk), lambda i: (i, 0, 0, 0)),
            pl.BlockSpec(memory_space=pl.ANY),
        ],
        out_specs=[
            pl.BlockSpec((rows, hw), lambda i: (rb0 + i, 0)),
            pl.BlockSpec((n_seq, heads, dk, dk), lambda i: (i, 0, 0, 0)),
        ],
        out_shape=[
            jax.ShapeDtypeStruct(buf.shape, buf.dtype),
            jax.ShapeDtypeStruct(s0.shape, F32),
        ],
        input_output_aliases={7: 0},
        compiler_params=_params(1),
        name="hgrn_step",
    )(z, z, z, z, lb.reshape(1, -1), hg_norm.reshape(1, dk), s0, buf)


def _gated_kernel(a_ref, b_ref, ga_ref, gb_ref, wa_ref, wb_ref, o_ref):
    o_ref[...] = (jax.nn.sigmoid(ga_ref[...]) * _dot(a_ref[...], wa_ref[...])
                  + jax.nn.sigmoid(gb_ref[...]) * _dot(b_ref[...], wb_ref[...])).astype(o_ref.dtype)


def _gated_merge(a, b, z, w_a, w_b, *, off_ga, off_gb, tm=512, tn=512):
    m, ka = a.shape
    kb = b.shape[1]
    n = w_a.shape[1]
    tn = _tile(n, tn)
    return pl.pallas_call(
        _gated_kernel,
        grid=(m // tm, n // tn),
        in_specs=[
            pl.BlockSpec((tm, ka), lambda i, j: (i, 0)),
            pl.BlockSpec((tm, kb), lambda i, j: (i, 0)),
            pl.BlockSpec((tm, tn), lambda i, j: (i, off_ga // tn + j)),
            pl.BlockSpec((tm, tn), lambda i, j: (i, off_gb // tn + j)),
            pl.BlockSpec((ka, tn), lambda i, j: (0, j)),
            pl.BlockSpec((kb, tn), lambda i, j: (0, j)),
        ],
        out_specs=pl.BlockSpec((tm, tn), lambda i, j: (i, j)),
        out_shape=jax.ShapeDtypeStruct((m, n), a.dtype),
        compiler_params=_params(2),
        name="gated_merge",
    )(a, b, z, z, w_a, w_b)


def _cross_kernel(q_ref, k_ref, v_ref, o_ref, *, heads, dh):
    for h in range(heads):
        cols = slice(h * dh, (h + 1) * dh)
        s = _dot_nt(q_ref[0, :, cols], k_ref[0, :, cols]) * dh ** -0.5
        p = jnp.exp(s - jnp.max(s, axis=-1, keepdims=True))
        p = p / jnp.sum(p, axis=-1, keepdims=True)
        o_ref[0, :, cols] = _dot(p, v_ref[0, :, cols])


def _cross_attention(q, k, v, *, heads, groups_per_mem):
    n, r, w = q.shape
    mem = k.shape[1]
    return pl.pallas_call(
        functools.partial(_cross_kernel, heads=heads, dh=w // heads),
        grid=(n,),
        in_specs=[
            pl.BlockSpec((1, r, w), lambda i: (i, 0, 0)),
            pl.BlockSpec((1, mem, w), lambda i: (i // groups_per_mem, 0, 0)),
            pl.BlockSpec((1, mem, w), lambda i: (i // groups_per_mem, 0, 0)),
        ],
        out_specs=pl.BlockSpec((1, r, w), lambda i: (i, 0, 0)),
        out_shape=jax.ShapeDtypeStruct((n, r, w), F32),
        compiler_params=_params(1),
        name="cross_attn",
    )(q, k, v)


def _cross_step_kernel(q_ref, k_ref, v_ref, o_ref, *, n_seq, heads, dh):
    rows = q_ref.shape[1]
    keys = k_ref.shape[1]
    row = lax.broadcasted_iota(jnp.int32, (heads * rows, keys), 0)
    col_head = jnp.bitwise_and(lax.broadcasted_iota(jnp.int32, (heads * rows, keys), 1), heads - 1)
    own = row < 0
    for h in range(heads):
        own = own | ((row >= h * rows) & (row < (h + 1) * rows) & (col_head == h))
    for s in range(n_seq):
        q = jnp.concatenate([q_ref[s, :, h * dh:(h + 1) * dh] for h in range(heads)], axis=0)
        sc = jnp.where(own, _dot_nt(q, k_ref[s]) * dh ** -0.5, NEG)
        p = jnp.exp(sc - jnp.max(sc, axis=-1, keepdims=True))
        p = p / jnp.sum(p, axis=-1, keepdims=True)
        o = _dot(p, v_ref[s])
        for h in range(heads):
            o_ref[s, :, h * dh:(h + 1) * dh] = o[h * rows:(h + 1) * rows, :]


def _cross_attention_step(q, k, v, *, heads, n_seq=4):
    b, rows, w = q.shape
    keys, dh = k.shape[1], k.shape[2]
    assert heads & (heads - 1) == 0 and b % n_seq == 0
    return pl.pallas_call(
        functools.partial(_cross_step_kernel, n_seq=n_seq, heads=heads, dh=dh),
        grid=(b // n_seq,),
        in_specs=[
            pl.BlockSpec((n_seq, rows, w), lambda i: (i, 0, 0)),
            pl.BlockSpec((n_seq, keys, dh), lambda i: (i, 0, 0)),
            pl.BlockSpec((n_seq, keys, dh), lambda i: (i, 0, 0)),
        ],
        out_specs=pl.BlockSpec((n_seq, rows, w), lambda i: (i, 0, 0)),
        out_shape=jax.ShapeDtypeStruct((b, rows, w), F32),
        compiler_params=_params(1),
        name="cross_attn_step",
    )(q, k, v)


def _router_kernel(x_ref, w_ref, g_ref, lg_ref, h_ref):
    h = _rms(x_ref[...], g_ref[...])
    lg_ref[...] = _dot_nt(h, w_ref[...])
    h_ref[...] = h


def _router(x, w_nk, gain, *, tm=512):
    m, d = x.shape
    n = w_nk.shape[0]
    return pl.pallas_call(
        _router_kernel,
        grid=(m // tm,),
        in_specs=[
            pl.BlockSpec((tm, d), lambda i: (i, 0)),
            pl.BlockSpec((n, d), lambda i: (0, 0)),
            pl.BlockSpec((1, d), lambda i: (0, 0)),
        ],
        out_specs=[pl.BlockSpec((tm, n), lambda i: (i, 0)), pl.BlockSpec((tm, d), lambda i: (i, 0))],
        out_shape=[jax.ShapeDtypeStruct((m, n), F32), jax.ShapeDtypeStruct((m, d), F32)],
        compiler_params=_params(1),
        name="router",
    )(x, w_nk, gain.reshape(1, d))


def _moe_kernel(be_ref, na_ref, h_ref, gate_ref, wgu_ref, wdn_ref, o_ref, *, d_expert):
    @pl.when(pl.program_id(0) < na_ref[0])
    def _():
        gu = _dot(h_ref[...].astype(BF16), wgu_ref[0].astype(BF16))
        act = jax.nn.silu(gu[:, :d_expert]) * gu[:, d_expert:]
        o_ref[...] = _dot(act.astype(BF16), wdn_ref[0].astype(BF16)) * gate_ref[...]

    @pl.when(pl.program_id(0) >= na_ref[0])
    def _():
        o_ref[...] = jnp.zeros_like(o_ref)


def _moe_ffn(block_expert, n_active, h_sorted, row_gate, w_gu, w_dn, *, block):
    n_rows, d = h_sorted.shape
    n_blocks = n_rows // block
    d_expert = w_dn.shape[1]

    def row_block(i, be, na):
        return (jnp.minimum(i, na[0] - 1), 0)

    grid_spec = pltpu.PrefetchScalarGridSpec(
        num_scalar_prefetch=2,
        grid=(n_blocks,),
        in_specs=[
            pl.BlockSpec((block, d), row_block),
            pl.BlockSpec((block, 1), row_block),
            pl.BlockSpec((1, d, 2 * d_expert), lambda i, be, na: (be[i], 0, 0)),
            pl.BlockSpec((1, d_expert, d), lambda i, be, na: (be[i], 0, 0)),
        ],
        out_specs=pl.BlockSpec((block, d), lambda i, be, na: (i, 0)),
    )
    return pl.pallas_call(
        functools.partial(_moe_kernel, d_expert=d_expert),
        grid_spec=grid_spec,
        out_shape=jax.ShapeDtypeStruct((n_rows, d), F32),
        compiler_params=_params(1),
        name="moe_ffn",
    )(block_expert, n_active, h_sorted, row_gate, w_gu, w_dn)


def _final_kernel(x_ref, ya_ref, yb_ref, g_ref, op_ref, os_ref, *, sample_tile0):
    y = _rms(x_ref[...] + (ya_ref[...] + yb_ref[...]), g_ref[...])

    @pl.when(pl.program_id(0) < sample_tile0)
    def _():
        op_ref[...] = y

    @pl.when(pl.program_id(0) >= sample_tile0)
    def _():
        os_ref[...] = y


def _final_norm(x, ya, yb, g, *, sample_row0, tm=512):
    m, d = x.shape
    assert sample_row0 % tm == 0 and m % tm == 0
    tile0 = sample_row0 // tm
    spec = pl.BlockSpec((tm, d), lambda i: (i, 0))
    return pl.pallas_call(
        functools.partial(_final_kernel, sample_tile0=tile0),
        grid=(m // tm,),
        in_specs=[spec, spec, spec, pl.BlockSpec((1, d), lambda i: (0, 0))],
        out_specs=[
            pl.BlockSpec((tm, d), lambda i: (jnp.minimum(i, tile0 - 1), 0)),
            pl.BlockSpec((tm, d), lambda i: (jnp.maximum(i - tile0, 0), 0)),
        ],
        out_shape=[jax.ShapeDtypeStruct((sample_row0, d), F32), jax.ShapeDtypeStruct((m - sample_row0, d), F32)],
        compiler_params=_params(1),
        name="final_norm",
    )(x, ya, yb, g.reshape(1, d))


def _top_k(x, k):
    lane = jnp.arange(x.shape[-1], dtype=jnp.int32)
    vals, idxs = [], []
    for _ in range(k):
        i = jnp.argmax(x, axis=-1).astype(jnp.int32)
        vals.append(jnp.max(x, axis=-1))
        idxs.append(i)
        x = jnp.where(lane == i[..., None], -jnp.inf, x)
    return jnp.stack(vals, axis=-1), jnp.stack(idxs, axis=-1)


def _route(logits, b_rg, b_re, n_groups, n_experts, moe_block):
    n = logits.shape[0]
    epg = n_experts // n_groups
    p_group = jax.nn.softmax(logits[:, :n_groups] + b_rg, axis=-1)
    g_w, g_idx = _top_k(p_group, 1)
    e_logits = (logits[:, n_groups:n_groups + n_experts] + b_re).reshape(n, n_groups, epg)
    in_group = jnp.take_along_axis(e_logits, g_idx[:, :, None], axis=1)[:, 0]
    top_logit, top_local = _top_k(in_group, TOP_K)
    gate = g_w * jax.nn.softmax(top_logit, axis=-1)
    expert_idx = g_idx * epg + top_local

    a = n * TOP_K
    flat_e = expert_idx.reshape(a).astype(jnp.int32)
    onehot = (flat_e[:, None] == jnp.arange(n_experts, dtype=jnp.int32)[None, :]).astype(jnp.int32)
    rank = jnp.sum((jnp.cumsum(onehot, axis=0) - onehot) * onehot, axis=1)
    counts = jnp.sum(onehot, axis=0)
    padded = (counts + moe_block - 1) // moe_block * moe_block
    pad_end = jnp.cumsum(padded)
    pad_start = pad_end - padded
    dest = pad_start[flat_e] + rank
    n_blocks = (a + n_experts * (moe_block - 1) + moe_block - 1) // moe_block
    n_rows = n_blocks * moe_block
    fill = jnp.stack([jnp.arange(n_rows, dtype=jnp.int32) % n, jnp.zeros((n_rows,), jnp.int32)], axis=1)
    vals = jnp.stack([jnp.arange(a, dtype=jnp.int32) // TOP_K,
                      lax.bitcast_convert_type(gate.reshape(a).astype(F32), jnp.int32)], axis=1)
    packed = fill.at[dest].set(vals)
    row_token = packed[:, 0]
    row_gate = lax.bitcast_convert_type(packed[:, 1], F32)
    n_active = (pad_end[-1] // moe_block).astype(jnp.int32)
    blk = jnp.minimum(jnp.arange(n_blocks, dtype=jnp.int32), n_active - 1) * moe_block
    block_expert = jnp.minimum(jnp.searchsorted(pad_end, blk, side='right'), n_experts - 1).astype(jnp.int32)
    return row_token, row_gate, dest.reshape(n, TOP_K), block_expert, n_active.reshape(1)


def kernel(x_prompt, mem_prompt, x_sample, cache_kv_latent, cache_k_rope, page_table, state_hgrn, cache_mem_k, cache_mem_v, norm_mix, w_in, q_norm, w_uq, kv_norm, w_uk, w_uv, lb_logits, hg_norm, w_out_mla, w_out_hgrn, w_out, norm_cross, norm_mem, w_cq, w_ckv, w_co, norm_ffn, w_router_group, b_router_group, w_router_expert, b_router_expert, w_expert_gate_up, w_expert_down, norm_final):
    batch, seq, d = x_prompt.shape
    dec_batch, dec_seq, _ = x_sample.shape
    depth = w_in.shape[0]
    assert depth == 1
    page = cache_kv_latent.shape[2]
    kv_lora = cache_kv_latent.shape[3]
    rope = cache_k_rope.shape[3]
    half = rope // 2
    past_len = page_table.shape[1] * page
    q_lora = q_norm.shape[1]
    hg_heads, hg_dk = state_hgrn.shape[2], state_hgrn.shape[3]
    hgw = hg_heads * hg_dk
    mla_heads = (w_uq.shape[2] - w_uk.shape[2]) // rope
    nope = w_uk.shape[2] // mla_heads
    v_head = w_uv.shape[2] // mla_heads
    mem_tokens, x_heads, x_dh = cache_mem_k.shape[2], cache_mem_k.shape[3], cache_mem_k.shape[4]
    xw = x_heads * x_dh
    n_groups = w_router_group.shape[2]
    n_experts = w_router_expert.shape[2]
    mp = batch * seq
    ms = dec_batch * dec_seq
    m_all = mp + ms
    mla_scale = (nope + rope) ** -0.5
    assert nope == LANES and 2 * half <= LANES and hg_dk == LANES

    n_front = q_lora + kv_lora + rope
    off_cq, off_ckv, off_kr = 0, q_lora, q_lora + kv_lora
    off_hq, off_hf, off_hi, off_hg = 0, hgw, 2 * hgw, 3 * hgw
    off_ga, off_gb = 4 * hgw, 4 * hgw + d
    w_in_t = jnp.swapaxes(w_in[0], 0, 1)
    w_back = w_in_t[n_front:].astype(BF16)
    w_front = jnp.concatenate([w_in_t[:n_front], jnp.zeros((LANES - rope, d), F32)], axis=0).astype(BF16)

    x0 = jnp.concatenate([x_prompt.reshape(mp, d), x_sample.reshape(ms, d)], axis=0)
    pos = jnp.concatenate([jnp.tile(jnp.arange(seq), batch), jnp.tile(past_len + jnp.arange(dec_seq), dec_batch)])
    cos_t, sin_t = _rope_tables(pos, half)

    tm_in = _tile(m_all, 1152, unit=2 * SUBLANES)
    z = _matmul(x0, w_back, w_is_nk=True, gain=norm_mix[0], tm=tm_in, tn_cap=1024, name="mm_in")
    z_front = _matmul(x0, w_front, w_is_nk=True, gain=norm_mix[0], tm=tm_in, tn_cap=1024, name="mm_in_front")
    lat, kr_pad = _lat_kr(z_front, kv_norm[0], cos_t, sin_t, off_ckv=off_ckv, off_kr=off_kr, kv_lora=kv_lora,
                          half=half)
    w_uq_p = jnp.pad(w_uq[0].reshape(q_lora, mla_heads, nope + rope), ((0, 0), (0, 0), (0, LANES - rope)))
    w_uq_p = w_uq_p.reshape(q_lora, mla_heads * (nope + LANES)).astype(BF16)
    q_all, q_tok = _q_proj(z_front, w_uq_p, q_norm[0], cos_t, sin_t, off_cq=off_cq, q_lora=q_lora, heads=mla_heads,
                           nope=nope, half=half, scale=mla_scale, sample_row0=mp)
    w_uk_lo, w_uv_lo = w_uk[0].astype(BF16), w_uv[0].astype(BF16)

    k_all, v_all = _kv_proj(lat, w_uk_lo, w_uv_lo, kr_pad, rows=mp, heads=mla_heads, nope=nope)
    a_all = _flash(q_all, k_all, v_all, jnp.zeros((m_all, mla_heads * v_head), BF16), batch=batch, seq=seq,
                   heads=mla_heads, dv=v_head)

    rows_s = dec_seq * mla_heads
    q_lat = _q_lat(q_tok, w_uk_lo, heads=mla_heads, nope=nope).reshape(dec_batch, rows_s, kv_lora)
    q_rope_s = q_tok.reshape(ms, mla_heads, nope + LANES)[:, :, nope:nope + rope].astype(F32)
    q_rope_s = q_rope_s.reshape(dec_batch, rows_s, rope)
    lat_s = lat[mp:].reshape(dec_batch, dec_seq, kv_lora)
    kr_s = kr_pad[mp:, :rope].reshape(dec_batch, dec_seq, rope)
    o_lat = _paged_attention(page_table, q_lat, q_rope_s, lat_s, kr_s,
                             cache_kv_latent.reshape(-1, page, kv_lora),
                             jnp.swapaxes(cache_k_rope, 2, 3).reshape(-1, rope, page),
                             t_new=dec_seq)
    a_all = _o_uv(o_lat.reshape(ms, mla_heads * kv_lora), w_uv_lo, a_all, heads=mla_heads, dv=v_head, row0=mp)

    lower_bounds = jnp.cumsum(jax.nn.softmax(lb_logits.astype(F32), axis=0), axis=0)
    lb = lower_bounds[0]
    chunk_p = math.gcd(HG_CHUNK, seq)
    hg_offs = (off_hq, off_hf, off_hi, off_hg)
    b_all, s_p = _hgrn(z, lb, hg_norm[0], jnp.zeros((m_all, hgw), BF16), offs=hg_offs, n_seq=batch, rows=seq,
                       row_block=_tile(seq, 256, unit=chunk_p), heads_total=hg_heads, heads=math.gcd(16, hg_heads),
                       dk=hg_dk, chunk=chunk_p)
    b_all, s_s = _hgrn_step(z, lb, hg_norm[0], state_hgrn[0], b_all, offs=hg_offs, row0=mp, n_seq_total=dec_batch,
                            t_new=dec_seq, heads=hg_heads, dk=hg_dk)

    merged = _gated_merge(a_all, b_all, z, w_out_mla[0].astype(BF16), w_out_hgrn[0].astype(BF16), off_ga=off_ga,
                          off_gb=off_gb, tm=tm_in)
    x1 = _matmul(merged, w_out[0].astype(BF16), res=x0, tm=tm_in, name="mm_out")

    kv_mem = _matmul(mem_prompt.reshape(batch * mem_tokens, d), w_ckv[0], gain=norm_mem[0], tm=batch * mem_tokens,
                     name="mem_kv")
    mk_p, mv_p = kv_mem[:, :xw].reshape(batch, mem_tokens, xw), kv_mem[:, xw:].reshape(batch, mem_tokens, xw)
    qx = _matmul(x1, w_cq[0], gain=norm_cross[0], name="cross_q")
    tq_x = 512
    ox_p = _cross_attention(qx[:mp].reshape(mp // tq_x, tq_x, xw), mk_p, mv_p, heads=x_heads,
                            groups_per_mem=seq // tq_x)
    rows_pad = -(-dec_seq // SUBLANES) * SUBLANES
    qx_s = jnp.pad(qx[mp:].reshape(dec_batch, dec_seq, xw), ((0, 0), (0, rows_pad - dec_seq), (0, 0)))
    ox_s = _cross_attention_step(qx_s, cache_mem_k.reshape(dec_batch, mem_tokens * x_heads, x_dh),
                                 cache_mem_v.reshape(dec_batch, mem_tokens * x_heads, x_dh), heads=x_heads,
                                 n_seq=math.gcd(4, dec_batch))
    ox = jnp.concatenate([ox_p.reshape(mp, xw), ox_s[:, :dec_seq].reshape(ms, xw)], axis=0)
    x2 = _matmul(ox, w_co[0], res=x1, name="cross_out")

    n_r = n_groups + n_experts
    w_r = jnp.concatenate([jnp.swapaxes(w_router_group[0], 0, 1), jnp.swapaxes(w_router_expert[0], 0, 1),
                           jnp.zeros((-n_r % LANES, d), F32)], axis=0)
    logits, h_ffn = _router(x2, w_r, norm_ffn[0])
    moe_block = 256

    def route(lg):
        return _route(lg, b_router_group[0], b_router_expert[0], n_groups, n_experts, moe_block)

    row_token, row_gate, dest, block_expert, n_active = route(logits)
    y_sorted = _moe_ffn(block_expert, n_active, h_ffn[row_token], row_gate[:, None],
                        w_expert_gate_up[0], w_expert_down[0], block=moe_block)
    y_p, y_s = _final_norm(x2, y_sorted[dest[:, 0]], y_sorted[dest[:, 1]], norm_final, sample_row0=mp)

    y_prompt = y_p.reshape(batch, seq, d)
    y_sample = y_s.reshape(dec_batch, dec_seq, d)
    n_prompt_pages = mp // page
    kv_latent_prompt = lat[:mp].reshape(depth, n_prompt_pages, page, kv_lora)
    k_rope_prompt = kr_pad[:mp, :rope].reshape(depth, n_prompt_pages, page, rope)
    state_hgrn_prompt = s_p[None]
    mem_k_prompt = mk_p.reshape(depth, batch, mem_tokens, x_heads, x_dh)
    mem_v_prompt = mv_p.reshape(depth, batch, mem_tokens, x_heads, x_dh)
    kv_latent_sample = lat_s[None]
    k_rope_sample = kr_s[None]
    state_hgrn_sample = s_s[None]
    return (y_prompt, y_sample, kv_latent_prompt, k_rope_prompt, state_hgrn_prompt, mem_k_prompt, mem_v_prompt,
            kv_latent_sample, k_rope_sample, state_hgrn_sample)
```

```python
import functools
import math

import jax
import jax.numpy as jnp
from jax import lax
from jax.experimental import pallas as pl
from jax.experimental.pallas import tpu as pltpu

F32 = jnp.float32
BF16 = jnp.bfloat16

EPS = 1e-6
NEG = -1e30
ROPE_THETA = 10000.0
TOP_K = 2
HG_CHUNK = 64
LANES = 128
SUBLANES = 8
VMEM_LIMIT = 56 * 1024 * 1024
PAGE_GROUP = 32
PAGE_SLOTS = 4


def _params(n_axes):
    return pltpu.CompilerParams(dimension_semantics=("arbitrary",) * n_axes, vmem_limit_bytes=VMEM_LIMIT)


def _tile(n, cap, unit=LANES):
    if n <= cap:
        return n
    t = cap - cap % unit
    while t >= unit:
        if n % t == 0:
            return t
        t -= unit
    raise ValueError(f"no tile for {n} under {cap}")


def _rms(x, g):
    return x * lax.rsqrt(jnp.mean(x * x, axis=-1, keepdims=True) + EPS) * g


def _dot(a, b):
    return jnp.dot(a, b, preferred_element_type=F32)


def _dot_nt(a, b):
    return lax.dot_general(a, b, (((1,), (1,)), ((), ())), preferred_element_type=F32)


def _dot_tn(a, b):
    return lax.dot_general(a, b, (((0,), (0,)), ((), ())), preferred_element_type=F32)


def _mm_kernel(*refs, has_gain, has_res, w_is_nk):
    it = iter(refs)
    x_ref, w_ref = next(it), next(it)
    g_ref = next(it) if has_gain else None
    r_ref = next(it) if has_res else None
    o_ref = next(it)
    if has_gain:
        xs_ref = next(it)

        @pl.when(pl.program_id(1) == 0)
        def _():
            xs_ref[...] = _rms(x_ref[...].astype(F32), g_ref[...]).astype(xs_ref.dtype)

        x = xs_ref[...]
    else:
        x = x_ref[...].astype(w_ref.dtype)
    acc = _dot_nt(x, w_ref[...]) if w_is_nk else _dot(x, w_ref[...])
    if has_res:
        acc = r_ref[...] + acc
    o_ref[...] = acc.astype(o_ref.dtype)


def _matmul(x, w, *, w_is_nk=False, gain=None, res=None, x_col=0, tm=512, tn_cap=512, out_dtype=F32, name="mm"):
    m = x.shape[0]
    n, k = w.shape if w_is_nk else w.shape[::-1]
    assert x_col % k == 0 and m % tm == 0
    tn = _tile(n, tn_cap)
    xb = x_col // k
    w_spec = pl.BlockSpec((tn, k), lambda i, j: (j, 0)) if w_is_nk else pl.BlockSpec((k, tn), lambda i, j: (0, j))
    in_specs = [pl.BlockSpec((tm, k), lambda i, j: (i, xb)), w_spec]
    args = [x, w]
    scratch = []
    if gain is not None:
        in_specs.append(pl.BlockSpec((1, k), lambda i, j: (0, 0)))
        args.append(gain.reshape(1, k).astype(F32))
        scratch.append(pltpu.VMEM((tm, k), w.dtype))
    if res is not None:
        in_specs.append(pl.BlockSpec((tm, tn), lambda i, j: (i, j)))
        args.append(res)
    return pl.pallas_call(
        functools.partial(_mm_kernel, has_gain=gain is not None, has_res=res is not None, w_is_nk=w_is_nk),
        grid=(m // tm, n // tn),
        in_specs=in_specs,
        out_specs=pl.BlockSpec((tm, tn), lambda i, j: (i, j)),
        out_shape=jax.ShapeDtypeStruct((m, n), out_dtype),
        scratch_shapes=scratch,
        compiler_params=_params(2),
        name=name,
    )(*args)


def _rope_tile(x, cos_t, sin_t, half):
    lane = lax.broadcasted_iota(jnp.int32, x.shape, 1)
    swapped = jnp.where(lane < half, pltpu.roll(x, LANES - half, 1), pltpu.roll(x, half, 1))
    return x * cos_t + swapped * sin_t


def _rope_tables(pos, half):
    inv = ROPE_THETA ** (-jnp.arange(half, dtype=F32) / half)
    ang = pos.astype(F32)[:, None] * inv[None, :]
    cos, sin = jnp.cos(ang), jnp.sin(ang)
    zeros = jnp.zeros((pos.shape[0], LANES - 2 * half), F32)
    return jnp.concatenate([cos, cos, zeros], axis=1), jnp.concatenate([-sin, sin, zeros], axis=1)


def _latkr_kernel(ckv_ref, kr_ref, g_ref, cos_ref, sin_ref, lat_ref, kro_ref, *, half):
    lat_ref[...] = _rms(ckv_ref[...], g_ref[...])
    kro_ref[...] = _rope_tile(kr_ref[...], cos_ref[...], sin_ref[...], half)


def _lat_kr(z, kv_norm, cos_t, sin_t, *, off_ckv, off_kr, kv_lora, half, tm=512):
    m = z.shape[0]
    return pl.pallas_call(
        functools.partial(_latkr_kernel, half=half),
        grid=(m // tm,),
        in_specs=[
            pl.BlockSpec((tm, kv_lora), lambda i: (i, off_ckv // kv_lora)),
            pl.BlockSpec((tm, LANES), lambda i: (i, off_kr // LANES)),
            pl.BlockSpec((1, kv_lora), lambda i: (0, 0)),
            pl.BlockSpec((tm, LANES), lambda i: (i, 0)),
            pl.BlockSpec((tm, LANES), lambda i: (i, 0)),
        ],
        out_specs=[pl.BlockSpec((tm, kv_lora), lambda i: (i, 0)), pl.BlockSpec((tm, LANES), lambda i: (i, 0))],
        out_shape=[jax.ShapeDtypeStruct((m, kv_lora), F32), jax.ShapeDtypeStruct((m, LANES), F32)],
        compiler_params=_params(1),
        name="lat_kr",
    )(z, z, kv_norm.reshape(1, kv_lora), cos_t, sin_t)


def _qproj_kernel(cq_ref, w_ref, g_ref, cos_ref, sin_ref, o_ref, os_ref, *, heads, nope, half, scale, sample_tile0):
    width = nope + LANES
    q = _dot(_rms(cq_ref[...], g_ref[...]).astype(w_ref.dtype), w_ref[...]) * scale
    cos_t, sin_t = cos_ref[...], sin_ref[...]
    parts = []
    for h in range(heads):
        q_nope = q[:, h * width:h * width + nope].astype(o_ref.dtype)
        q_rope = _rope_tile(q[:, h * width + nope:(h + 1) * width], cos_t, sin_t, half).astype(o_ref.dtype)
        parts.append((q_nope, q_rope))
        o_ref[h, :, :nope] = q_nope
        o_ref[h, :, nope:] = q_rope

    @pl.when(pl.program_id(0) >= sample_tile0)
    def _():
        for h, (q_nope, q_rope) in enumerate(parts):
            os_ref[:, h * width:h * width + nope] = q_nope
            os_ref[:, h * width + nope:(h + 1) * width] = q_rope


def _q_proj(z, w_uq_p, q_norm, cos_t, sin_t, *, off_cq, q_lora, heads, nope, half, scale, sample_row0, tm=512):
    m = z.shape[0]
    width = nope + LANES
    assert sample_row0 % tm == 0
    tile0 = sample_row0 // tm
    return pl.pallas_call(
        functools.partial(_qproj_kernel, heads=heads, nope=nope, half=half, scale=scale, sample_tile0=tile0),
        grid=(m // tm,),
        in_specs=[
            pl.BlockSpec((tm, q_lora), lambda i: (i, off_cq // q_lora)),
            pl.BlockSpec((q_lora, heads * width), lambda i: (0, 0)),
            pl.BlockSpec((1, q_lora), lambda i: (0, 0)),
            pl.BlockSpec((tm, LANES), lambda i: (i, 0)),
            pl.BlockSpec((tm, LANES), lambda i: (i, 0)),
        ],
        out_specs=[
            pl.BlockSpec((heads, tm, width), lambda i: (0, i, 0)),
            pl.BlockSpec((tm, heads * width), lambda i: (jnp.maximum(i - tile0, 0), 0)),
        ],
        out_shape=[
            jax.ShapeDtypeStruct((heads, m, width), BF16),
            jax.ShapeDtypeStruct((m - sample_row0, heads * width), BF16),
        ],
        compiler_params=_params(1),
        name="q_proj",
    )(z, w_uq_p, q_norm.reshape(1, q_lora), cos_t, sin_t)


def _kvproj_kernel(lat_ref, wk_ref, wv_ref, kr_ref, k_ref, v_ref, *, heads, nope):
    lat = lat_ref[...].astype(wk_ref.dtype)
    k = _dot(lat, wk_ref[...])
    kr = kr_ref[...].astype(k_ref.dtype)
    for h in range(heads):
        k_ref[h, :, :nope] = k[:, h * nope:(h + 1) * nope].astype(k_ref.dtype)
        k_ref[h, :, nope:] = kr
    v_ref[...] = _dot(lat, wv_ref[...]).astype(v_ref.dtype)


def _kv_proj(lat, w_uk, w_uv, kr_pad, *, rows, heads, nope, tm=512):
    kv_lora = lat.shape[1]
    nv = w_uv.shape[1]
    return pl.pallas_call(
        functools.partial(_kvproj_kernel, heads=heads, nope=nope),
        grid=(rows // tm,),
        in_specs=[
            pl.BlockSpec((tm, kv_lora), lambda i: (i, 0)),
            pl.BlockSpec((kv_lora, heads * nope), lambda i: (0, 0)),
            pl.BlockSpec((kv_lora, nv), lambda i: (0, 0)),
            pl.BlockSpec((tm, LANES), lambda i: (i, 0)),
        ],
        out_specs=[
            pl.BlockSpec((heads, tm, nope + LANES), lambda i: (0, i, 0)),
            pl.BlockSpec((tm, nv), lambda i: (i, 0)),
        ],
        out_shape=[
            jax.ShapeDtypeStruct((heads, rows, nope + LANES), BF16),
            jax.ShapeDtypeStruct((rows, nv), BF16),
        ],
        compiler_params=_params(1),
        name="kv_proj",
    )(lat, w_uk, w_uv, kr_pad)


def _flash_kernel(q_ref, k_ref, v_ref, buf_ref, o_ref, *, tq, tk):
    del buf_ref
    qi = pl.program_id(2)
    dv = v_ref.shape[1]

    def step(kb, carry, row0, masked):
        m, l, acc = carry
        start = pl.multiple_of(kb * tk, tk)
        k = k_ref[0, pl.ds(start, tk), :]
        v = v_ref[pl.ds(start, tk), :]
        s = _dot_nt(q_ref[0, row0:, :], k)
        if masked:
            row = lax.broadcasted_iota(jnp.int32, s.shape, 0)
            col = lax.broadcasted_iota(jnp.int32, s.shape, 1)
            s = jnp.where(col <= row, s, NEG)
        m_new = jnp.maximum(m, jnp.max(s, axis=-1, keepdims=True))
        alpha = jnp.exp(m - m_new)
        p = jnp.exp(s - m_new)
        l = l * alpha + jnp.sum(p, axis=-1, keepdims=True)
        acc = acc * alpha + _dot(p.astype(v.dtype), v)
        return m_new, l, acc

    n_full = qi * (tq // tk)
    carry = (jnp.full((tq, 1), NEG, F32), jnp.zeros((tq, 1), F32), jnp.zeros((tq, dv), F32))
    carry = lax.fori_loop(0, n_full, functools.partial(step, row0=0, masked=False), carry)
    n_diag = tq // tk
    for j in range(n_diag):
        row0 = j * tk
        m, l, acc = step(n_full + j, carry, row0, True)
        o_ref[row0:row0 + tk, :] = (acc[:tk] / l[:tk]).astype(o_ref.dtype)
        if j + 1 < n_diag:
            carry = (m[tk:], l[tk:], acc[tk:])


def _flash(q_all, k_all, v_all, buf, *, batch, seq, heads, dv, tq=1024, tk=512):
    nq = seq // tq
    width = q_all.shape[2]
    assert tq % tk == 0
    return pl.pallas_call(
        functools.partial(_flash_kernel, tq=tq, tk=tk),
        grid=(batch, heads, nq),
        in_specs=[
            pl.BlockSpec((1, tq, width), lambda b, h, i: (h, b * nq + i, 0)),
            pl.BlockSpec((1, seq, width), lambda b, h, i: (h, b, 0)),
            pl.BlockSpec((seq, dv), lambda b, h, i: (b, h)),
            pl.BlockSpec(memory_space=pl.ANY),
        ],
        out_specs=pl.BlockSpec((tq, dv), lambda b, h, i: (b * nq + i, h)),
        out_shape=jax.ShapeDtypeStruct(buf.shape, buf.dtype),
        input_output_aliases={3: 0},
        compiler_params=_params(3),
        name="flash",
    )(q_all, k_all, v_all, buf)


def _qlat_kernel(q_ref, w_ref, o_ref, *, nope):
    o_ref[...] = _dot_nt(q_ref[:, :nope], w_ref[...])


def _q_lat(q_tok, w_uk, *, heads, nope):
    rows = q_tok.shape[0]
    width = q_tok.shape[1] // heads
    kv_lora = w_uk.shape[0]
    return pl.pallas_call(
        functools.partial(_qlat_kernel, nope=nope),
        grid=(heads,),
        in_specs=[
            pl.BlockSpec((rows, width), lambda h: (0, h)),
            pl.BlockSpec((kv_lora, nope), lambda h: (0, h)),
        ],
        out_specs=pl.BlockSpec((rows, kv_lora), lambda h: (0, h)),
        out_shape=jax.ShapeDtypeStruct((rows, heads * kv_lora), F32),
        compiler_params=_params(1),
        name="q_lat",
    )(q_tok, w_uk)


def _ouv_kernel(x_ref, w_ref, buf_ref, o_ref):
    del buf_ref
    o_ref[...] = _dot(x_ref[...].astype(w_ref.dtype), w_ref[...]).astype(o_ref.dtype)


def _o_uv(o_lat, w_uv, buf, *, heads, dv, row0):
    rows = o_lat.shape[0]
    kv_lora = o_lat.shape[1] // heads
    assert row0 % rows == 0
    return pl.pallas_call(
        _ouv_kernel,
        grid=(heads,),
        in_specs=[
            pl.BlockSpec((rows, kv_lora), lambda h: (0, h)),
            pl.BlockSpec((kv_lora, dv), lambda h: (0, h)),
            pl.BlockSpec(memory_space=pl.ANY),
        ],
        out_specs=pl.BlockSpec((rows, dv), lambda h: (row0 // rows, h)),
        out_shape=jax.ShapeDtypeStruct(buf.shape, buf.dtype),
        input_output_aliases={2: 0},
        compiler_params=_params(1),
        name="o_uv",
    )(o_lat, w_uv, buf)


def _paged_kernel(pt_ref, ql_ref, qr_ref, tok_ref, nl_ref, nr_ref, cl_hbm, cr_hbm, o_ref, bl_ref, br_ref, sem_ref,
                  *, n_seq, n_groups, group, page, slots, t_new):
    b = pl.program_id(0)
    total = n_seq * n_groups
    ahead = slots - 1

    def copies(t):
        seq = lax.div(t, n_groups)
        g = lax.rem(t, n_groups)
        slot = lax.rem(t, slots)
        out = []
        for i in range(group):
            pid = pt_ref[seq, g * group + i]
            keys = pl.ds(i * page, page)
            out.append(pltpu.make_async_copy(cl_hbm.at[pid], bl_ref.at[slot, keys], sem_ref.at[0, slot, i]))
            out.append(pltpu.make_async_copy(cr_hbm.at[pid], br_ref.at[slot, i], sem_ref.at[1, slot, i]))
        return out

    @pl.when(b == 0)
    def _():
        for t0 in range(min(ahead, total)):
            for c in copies(jnp.int32(t0)):
                c.start()

    ql = ql_ref[0]
    qr = qr_ref[0]
    ql_lo = ql.astype(BF16)
    qr_lo = qr.astype(BF16)
    rows = ql.shape[0]
    kv = ql.shape[1]

    def body(g, carry):
        m, l, acc = carry
        t = b * n_groups + g

        @pl.when(t + ahead < total)
        def _():
            for c in copies(t + ahead):
                c.start()

        for c in copies(t):
            c.wait()
        slot = lax.rem(t, slots)
        c_blk = bl_ref[slot].astype(BF16)
        s_rope = jnp.concatenate([_dot(qr_lo, br_ref[slot, i].astype(BF16)) for i in range(group)], axis=1)
        s = _dot_nt(ql_lo, c_blk) + s_rope
        m_new = jnp.maximum(m, jnp.max(s, axis=-1, keepdims=True))
        alpha = jnp.exp(m - m_new)
        p = jnp.exp(s - m_new)
        l = l * alpha + jnp.sum(p, axis=-1, keepdims=True)
        acc = acc * alpha + _dot(p.astype(BF16), c_blk)
        return m_new, l, acc

    init = (jnp.full((rows, 1), NEG, F32), jnp.zeros((rows, 1), F32), jnp.zeros((rows, kv), F32))
    m, l, acc = lax.fori_loop(0, n_groups, body, init)

    t_of_row = tok_ref[...]
    s_new = []
    for j in range(t_new):
        sj = (jnp.sum(ql * nl_ref[0, j:j + 1, :], axis=-1, keepdims=True)
              + jnp.sum(qr * nr_ref[0, j:j + 1, :], axis=-1, keepdims=True))
        s_new.append(jnp.where(t_of_row >= j, sj, NEG))
    m_new = m
    for sj in s_new:
        m_new = jnp.maximum(m_new, sj)
    alpha = jnp.exp(m - m_new)
    l = l * alpha
    acc = acc * alpha
    for j, sj in enumerate(s_new):
        pj = jnp.exp(sj - m_new)
        l = l + pj
        acc = acc + pj * nl_ref[0, j:j + 1, :]
    o_ref[0] = acc / l


def _paged_attention(page_table, q_lat, q_rope, new_lat, new_kr, cache_lat, cache_kr_t, *, t_new):
    nb, rows, kv = q_lat.shape
    rope = q_rope.shape[2]
    n_pages = page_table.shape[1]
    page = cache_lat.shape[1]
    group = math.gcd(PAGE_GROUP, n_pages)
    n_groups = n_pages // group
    slots = PAGE_SLOTS
    grid_spec = pltpu.PrefetchScalarGridSpec(
        num_scalar_prefetch=1,
        grid=(nb,),
        in_specs=[
            pl.BlockSpec((1, rows, kv), lambda b, pt: (b, 0, 0)),
            pl.BlockSpec((1, rows, rope), lambda b, pt: (b, 0, 0)),
            pl.BlockSpec((rows, 1), lambda b, pt: (0, 0)),
            pl.BlockSpec((1, t_new, kv), lambda b, pt: (b, 0, 0)),
            pl.BlockSpec((1, t_new, rope), lambda b, pt: (b, 0, 0)),
            pl.BlockSpec(memory_space=pl.ANY),
            pl.BlockSpec(memory_space=pl.ANY),
        ],
        out_specs=pl.BlockSpec((1, rows, kv), lambda b, pt: (b, 0, 0)),
        scratch_shapes=[
            pltpu.VMEM((slots, group * page, kv), F32),
            pltpu.VMEM((slots, group, rope, page), F32),
            pltpu.SemaphoreType.DMA((2, slots, group)),
        ],
    )
    tok = (jnp.arange(rows, dtype=jnp.int32) // (rows // t_new)).reshape(rows, 1)
    return pl.pallas_call(
        functools.partial(_paged_kernel, n_seq=nb, n_groups=n_groups, group=group, page=page, slots=slots,
                          t_new=t_new),
        grid_spec=grid_spec,
        out_shape=jax.ShapeDtypeStruct((nb, rows, kv), F32),
        compiler_params=_params(1),
        name="paged_attn",
    )(page_table, q_lat, q_rope, tok, new_lat, new_kr, cache_lat, cache_kr_t)


def _pad_rows(x, rows):
    if x.shape[0] == rows:
        return x
    return jnp.concatenate([x, jnp.zeros((rows - x.shape[0], x.shape[1]), x.dtype)], axis=0)


def _hgrn_kernel(hq_ref, hf_ref, hi_ref, hg_ref, lb_ref, gn_ref, buf_ref, o_ref, so_ref, st_ref,
                 *, heads, dk, chunk, n_chunks):
    del buf_ref
    @pl.when(pl.program_id(2) == 0)
    def _():
        st_ref[...] = jnp.zeros_like(st_ref)

    keys = max(chunk, LANES)
    row = lax.broadcasted_iota(jnp.int32, (chunk, keys), 0)
    col = lax.broadcasted_iota(jnp.int32, (chunk, keys), 1)
    causal = col <= row
    row_id = lax.broadcasted_iota(jnp.int32, (chunk, dk), 0)

    def body(c, carry):
        r0 = pl.multiple_of(c * chunk, chunk)
        for h in range(heads):
            cols = slice(h * dk, (h + 1) * dk)
            lb = lb_ref[:, cols]
            q = jax.nn.silu(hq_ref[pl.ds(r0, chunk), cols]) * dk ** -0.5
            f = lb + (1.0 - lb) * jax.nn.sigmoid(hf_ref[pl.ds(r0, chunk), cols])
            logf = jnp.log(f)
            k = 1.0 - f
            v = hi_ref[pl.ds(r0, chunk), cols]
            bsum = logf
            shift = 1
            while shift < chunk:
                bsum = bsum + jnp.where(row_id >= shift, pltpu.roll(bsum, shift, 0), 0.0)
                shift *= 2
            b_last = bsum[chunk - 1:chunk, :]
            q_dec = q * jnp.exp(bsum)
            k_inv = k * jnp.exp(-bsum)
            v_pad = _pad_rows(v, keys)
            att = jnp.where(causal, _dot_nt(q_dec, _pad_rows(k_inv, keys)), 0.0)
            st = st_ref[h]
            o = _dot(att, v_pad) + _dot_nt(q_dec, st)
            k_end = k * jnp.exp(b_last - bsum)
            st_ref[h] = st * jnp.exp(b_last) + _dot_tn(v_pad, _pad_rows(k_end, keys))
            o = _rms(o, gn_ref[...]) * jax.nn.silu(hg_ref[pl.ds(r0, chunk), cols])
            o_ref[pl.ds(r0, chunk), cols] = o.astype(o_ref.dtype)
        return carry

    lax.fori_loop(0, n_chunks, body, 0)

    @pl.when(pl.program_id(2) == pl.num_programs(2) - 1)
    def _():
        for h in range(heads):
            so_ref[0, h] = st_ref[h].T


def _hgrn(z, lb, hg_norm, buf, *, offs, n_seq, rows, row_block, heads_total, heads, dk, chunk):
    hw = heads * dk
    n_hb = heads_total // heads
    n_rb = rows // row_block
    n_chunks = row_block // chunk

    def zspec(off):
        return pl.BlockSpec((row_block, hw), lambda s, j, r: (s * n_rb + r, off // hw + j))

    return pl.pallas_call(
        functools.partial(_hgrn_kernel, heads=heads, dk=dk, chunk=chunk, n_chunks=n_chunks),
        grid=(n_seq, n_hb, n_rb),
        in_specs=[zspec(o) for o in offs] + [
            pl.BlockSpec((1, hw), lambda s, j, r: (0, j)),
            pl.BlockSpec((1, dk), lambda s, j, r: (0, 0)),
            pl.BlockSpec(memory_space=pl.ANY),
        ],
        out_specs=[
            pl.BlockSpec((row_block, hw), lambda s, j, r: (s * n_rb + r, j)),
            pl.BlockSpec((1, heads, dk, dk), lambda s, j, r: (s, j, 0, 0)),
        ],
        out_shape=[
            jax.ShapeDtypeStruct(buf.shape, buf.dtype),
            jax.ShapeDtypeStruct((n_seq, heads_total, dk, dk), F32),
        ],
        input_output_aliases={6: 0},
        scratch_shapes=[pltpu.VMEM((heads, dk, dk), F32)],
        compiler_params=_params(3),
        name="hgrn",
    )(z, z, z, z, lb.reshape(1, -1), hg_norm.reshape(1, dk), buf)


def _hgrn_step_kernel(hq_ref, hf_ref, hi_ref, hg_ref, lb_ref, gn_ref, s0_ref, buf_ref, o_ref, so_ref,
                      *, heads, dk, n_seq, t_new):
    del buf_ref
    rows = n_seq * t_new
    for h in range(heads):
        cols = slice(h * dk, (h + 1) * dk)
        lb = lb_ref[:, cols]
        q = jax.nn.silu(hq_ref[:, cols]) * dk ** -0.5
        f = lb + (1.0 - lb) * jax.nn.sigmoid(hf_ref[:, cols])
        v = hi_ref[:, cols]
        row_id = lax.broadcasted_iota(jnp.int32, (rows, dk), 0)
        o = jnp.zeros((rows, dk), F32)
        for s in range(n_seq):
            st = s0_ref[s, h]
            for t in range(t_new):
                r = s * t_new + t
                f_col = jnp.broadcast_to(f[r:r + 1, :], (dk, dk)).T
                q_col = jnp.broadcast_to(q[r:r + 1, :], (dk, dk)).T
                st = st * f_col + (1.0 - f_col) * v[r:r + 1, :]
                o = jnp.where(row_id == r, jnp.sum(st * q_col, axis=0, keepdims=True), o)
            so_ref[s, h] = st
        o_ref[:, cols] = (_rms(o, gn_ref[...]) * jax.nn.silu(hg_ref[:, cols])).astype(o_ref.dtype)


def _hgrn_step(z, lb, hg_norm, s0, buf, *, offs, row0, n_seq_total, t_new, heads, dk):
    n_seq = 2 * SUBLANES // t_new
    rows = n_seq * t_new
    hw = heads * dk
    assert rows == 2 * SUBLANES and row0 % rows == 0 and n_seq_total % n_seq == 0
    rb0 = row0 // rows

    def zspec(off):
        return pl.BlockSpec((rows, hw), lambda i: (rb0 + i, off // hw))

    return pl.pallas_call(
        functools.partial(_hgrn_step_kernel, heads=heads, dk=dk, n_seq=n_seq, t_new=t_new),
        grid=(n_seq_total // n_seq,),
        in_specs=[zspec(o) for o in offs] + [
            pl.BlockSpec((1, hw), lambda i: (0, 0)),
            pl.BlockSpec((1, dk), lambda i: (0, 0)),
            pl.BlockSpec((n_seq, heads, dk, dk), lambda i: (i, 0, 0, 0)),
            pl.BlockSpec(memory_space=pl.ANY),
        ],
        out_specs=[
            pl.BlockSpec((rows, hw), lambda i: (rb0 + i, 0)),
            pl.BlockSpec((n_seq, heads, dk, dk), lambda i: (i, 0, 0, 0)),
        ],
        out_shape=[
            jax.ShapeDtypeStruct(buf.shape, buf.dtype),
            jax.ShapeDtypeStruct(s0.shape, F32),
        ],
        input_output_aliases={7: 0},
        compiler_params=_params(1),
        name="hgrn_step",
    )(z, z, z, z, lb.reshape(1, -1), hg_norm.reshape(1, dk), s0, buf)


def _gated_kernel(a_ref, b_ref, ga_ref, gb_ref, wa_ref, wb_ref, o_ref):
    o_ref[...] = (jax.nn.sigmoid(ga_ref[...]) * _dot(a_ref[...], wa_ref[...])
                  + jax.nn.sigmoid(gb_ref[...]) * _dot(b_ref[...], wb_ref[...])).astype(o_ref.dtype)


def _gated_merge(a, b, z, w_a, w_b, *, off_ga, off_gb, tm=512, tn=512):
    m, ka = a.shape
    kb = b.shape[1]
    n = w_a.shape[1]
    tn = _tile(n, tn)
    return pl.pallas_call(
        _gated_kernel,
        grid=(m // tm, n // tn),
        in_specs=[
            pl.BlockSpec((tm, ka), lambda i, j: (i, 0)),
            pl.BlockSpec((tm, kb), lambda i, j: (i, 0)),
            pl.BlockSpec((tm, tn), lambda i, j: (i, off_ga // tn + j)),
            pl.BlockSpec((tm, tn), lambda i, j: (i, off_gb // tn + j)),
            pl.BlockSpec((ka, tn), lambda i, j: (0, j)),
            pl.BlockSpec((kb, tn), lambda i, j: (0, j)),
        ],
        out_specs=pl.BlockSpec((tm, tn), lambda i, j: (i, j)),
        out_shape=jax.ShapeDtypeStruct((m, n), a.dtype),
        compiler_params=_params(2),
        name="gated_merge",
    )(a, b, z, z, w_a, w_b)


def _cross_kernel(q_ref, k_ref, v_ref, o_ref, *, heads, dh):
    for h in range(heads):
        cols = slice(h * dh, (h + 1) * dh)
        s = _dot_nt(q_ref[0, :, cols], k_ref[0, :, cols]) * dh ** -0.5
        p = jnp.exp(s - jnp.max(s, axis=-1, keepdims=True))
        p = p / jnp.sum(p, axis=-1, keepdims=True)
        o_ref[0, :, cols] = _dot(p, v_ref[0, :, cols])


def _cross_attention(q, k, v, *, heads, groups_per_mem):
    n, r, w = q.shape
    mem = k.shape[1]
    return pl.pallas_call(
        functools.partial(_cross_kernel, heads=heads, dh=w // heads),
        grid=(n,),
        in_specs=[
            pl.BlockSpec((1, r, w), lambda i: (i, 0, 0)),
            pl.BlockSpec((1, mem, w), lambda i: (i // groups_per_mem, 0, 0)),
            pl.BlockSpec((1, mem, w), lambda i: (i // groups_per_mem, 0, 0)),
        ],
        out_specs=pl.BlockSpec((1, r, w), lambda i: (i, 0, 0)),
        out_shape=jax.ShapeDtypeStruct((n, r, w), F32),
        compiler_params=_params(1),
        name="cross_attn",
    )(q, k, v)


def _cross_step_kernel(q_ref, k_ref, v_ref, o_ref, *, n_seq, heads, dh):
    rows = q_ref.shape[1]
    keys = k_ref.shape[1]
    row = lax.broadcasted_iota(jnp.int32, (heads * rows, keys), 0)
    col_head = jnp.bitwise_and(lax.broadcasted_iota(jnp.int32, (heads * rows, keys), 1), heads - 1)
    own = row < 0
    for h in range(heads):
        own = own | ((row >= h * rows) & (row < (h + 1) * rows) & (col_head == h))
    for s in range(n_seq):
        q = jnp.concatenate([q_ref[s, :, h * dh:(h + 1) * dh] for h in range(heads)], axis=0)
        sc = jnp.where(own, _dot_nt(q, k_ref[s]) * dh ** -0.5, NEG)
        p = jnp.exp(sc - jnp.max(sc, axis=-1, keepdims=True))
        p = p / jnp.sum(p, axis=-1, keepdims=True)
        o = _dot(p, v_ref[s])
        for h in range(heads):
            o_ref[s, :, h * dh:(h + 1) * dh] = o[h * rows:(h + 1) * rows, :]


def _cross_attention_step(q, k, v, *, heads, n_seq=4):
    b, rows, w = q.shape
    keys, dh = k.shape[1], k.shape[2]
    assert heads & (heads - 1) == 0 and b % n_seq == 0
    return pl.pallas_call(
        functools.partial(_cross_step_kernel, n_seq=n_seq, heads=heads, dh=dh),
        grid=(b // n_seq,),
        in_specs=[
            pl.BlockSpec((n_seq, rows, w), lambda i: (i, 0, 0)),
            pl.BlockSpec((n_seq, keys, dh), lambda i: (i, 0, 0)),
            pl.BlockSpec((n_seq, keys, dh), lambda i: (i, 0, 0)),
        ],
        out_specs=pl.BlockSpec((n_seq, rows, w), lambda i: (i, 0, 0)),
        out_shape=jax.ShapeDtypeStruct((b, rows, w), F32),
        compiler_params=_params(1),
        name="cross_attn_step",
    )(q, k, v)


def _router_kernel(x_ref, w_ref, g_ref, lg_ref, h_ref):
    h = _rms(x_ref[...], g_ref[...])
    lg_ref[...] = _dot_nt(h, w_ref[...])
    h_ref[...] = h


def _router(x, w_nk, gain, *, tm=512):
    m, d = x.shape
    n = w_nk.shape[0]
    return pl.pallas_call(
        _router_kernel,
        grid=(m // tm,),
        in_specs=[
            pl.BlockSpec((tm, d), lambda i: (i, 0)),
            pl.BlockSpec((n, d), lambda i: (0, 0)),
            pl.BlockSpec((1, d), lambda i: (0, 0)),
        ],
        out_specs=[pl.BlockSpec((tm, n), lambda i: (i, 0)), pl.BlockSpec((tm, d), lambda i: (i, 0))],
        out_shape=[jax.ShapeDtypeStruct((m, n), F32), jax.ShapeDtypeStruct((m, d), F32)],
        compiler_params=_params(1),
        name="router",
    )(x, w_nk, gain.reshape(1, d))


def _moe_kernel(be_ref, na_ref, h_ref, gate_ref, wgu_ref, wdn_ref, o_ref, *, d_expert):
    @pl.when(pl.program_id(0) < na_ref[0])
    def _():
        gu = _dot(h_ref[...].astype(BF16), wgu_ref[0].astype(BF16))
        act = jax.nn.silu(gu[:, :d_expert]) * gu[:, d_expert:]
        o_ref[...] = _dot(act.astype(BF16), wdn_ref[0].astype(BF16)) * gate_ref[...]

    @pl.when(pl.program_id(0) >= na_ref[0])
    def _():
        o_ref[...] = jnp.zeros_like(o_ref)


def _moe_ffn(block_expert, n_active, h_sorted, row_gate, w_gu, w_dn, *, block):
    n_rows, d = h_sorted.shape
    n_blocks = n_rows // block
    d_expert = w_dn.shape[1]

    def row_block(i, be, na):
        return (jnp.minimum(i, na[0] - 1), 0)

    grid_spec = pltpu.PrefetchScalarGridSpec(
        num_scalar_prefetch=2,
        grid=(n_blocks,),
        in_specs=[
            pl.BlockSpec((block, d), row_block),
            pl.BlockSpec((block, 1), row_block),
            pl.BlockSpec((1, d, 2 * d_expert), lambda i, be, na: (be[i], 0, 0)),
            pl.BlockSpec((1, d_expert, d), lambda i, be, na: (be[i], 0, 0)),
        ],
        out_specs=pl.BlockSpec((block, d), lambda i, be, na: (i, 0)),
    )
    return pl.pallas_call(
        functools.partial(_moe_kernel, d_expert=d_expert),
        grid_spec=grid_spec,
        out_shape=jax.ShapeDtypeStruct((n_rows, d), F32),
        compiler_params=_params(1),
        name="moe_ffn",
    )(block_expert, n_active, h_sorted, row_gate, w_gu, w_dn)


def _final_kernel(x_ref, ya_ref, yb_ref, g_ref, op_ref, os_ref, *, sample_tile0):
    y = _rms(x_ref[...] + (ya_ref[...] + yb_ref[...]), g_ref[...])

    @pl.when(pl.program_id(0) < sample_tile0)
    def _():
        op_ref[...] = y

    @pl.when(pl.program_id(0) >= sample_tile0)
    def _():
        os_ref[...] = y


def _final_norm(x, ya, yb, g, *, sample_row0, tm=512):
    m, d = x.shape
    assert sample_row0 % tm == 0 and m % tm == 0
    tile0 = sample_row0 // tm
    spec = pl.BlockSpec((tm, d), lambda i: (i, 0))
    return pl.pallas_call(
        functools.partial(_final_kernel, sample_tile0=tile0),
        grid=(m // tm,),
        in_specs=[spec, spec, spec, pl.BlockSpec((1, d), lambda i: (0, 0))],
        out_specs=[
            pl.BlockSpec((tm, d), lambda i: (jnp.minimum(i, tile0 - 1), 0)),
            pl.BlockSpec((tm, d), lambda i: (jnp.maximum(i - tile0, 0), 0)),
        ],
        out_shape=[jax.ShapeDtypeStruct((sample_row0, d), F32), jax.ShapeDtypeStruct((m - sample_row0, d), F32)],
        compiler_params=_params(1),
        name="final_norm",
    )(x, ya, yb, g.reshape(1, d))


def _top_k(x, k):
    lane = jnp.arange(x.shape[-1], dtype=jnp.int32)
    vals, idxs = [], []
    for _ in range(k):
        i = jnp.argmax(x, axis=-1).astype(jnp.int32)
        vals.append(jnp.max(x, axis=-1))
        idxs.append(i)
        x = jnp.where(lane == i[..., None], -jnp.inf, x)
    return jnp.stack(vals, axis=-1), jnp.stack(idxs, axis=-1)


def _route(logits, b_rg, b_re, n_groups, n_experts, moe_block):
    n = logits.shape[0]
    epg = n_experts // n_groups
    p_group = jax.nn.softmax(logits[:, :n_groups] + b_rg, axis=-1)
    g_w, g_idx = _top_k(p_group, 1)
    e_logits = (logits[:, n_groups:n_groups + n_experts] + b_re).reshape(n, n_groups, epg)
    in_group = jnp.take_along_axis(e_logits, g_idx[:, :, None], axis=1)[:, 0]
    top_logit, top_local = _top_k(in_group, TOP_K)
    gate = g_w * jax.nn.softmax(top_logit, axis=-1)
    expert_idx = g_idx * epg + top_local

    a = n * TOP_K
    flat_e = expert_idx.reshape(a).astype(jnp.int32)
    onehot = (flat_e[:, None] == jnp.arange(n_experts, dtype=jnp.int32)[None, :]).astype(jnp.int32)
    rank = jnp.sum((jnp.cumsum(onehot, axis=0) - onehot) * onehot, axis=1)
    counts = jnp.sum(onehot, axis=0)
    padded = (counts + moe_block - 1) // moe_block * moe_block
    pad_end = jnp.cumsum(padded)
    pad_start = pad_end - padded
    dest = pad_start[flat_e] + rank
    n_blocks = (a + n_experts * (moe_block - 1) + moe_block - 1) // moe_block
    n_rows = n_blocks * moe_block
    fill = jnp.stack([jnp.arange(n_rows, dtype=jnp.int32) % n, jnp.zeros((n_rows,), jnp.int32)], axis=1)
    vals = jnp.stack([jnp.arange(a, dtype=jnp.int32) // TOP_K,
                      lax.bitcast_convert_type(gate.reshape(a).astype(F32), jnp.int32)], axis=1)
    packed = fill.at[dest].set(vals)
    row_token = packed[:, 0]
    row_gate = lax.bitcast_convert_type(packed[:, 1], F32)
    n_active = (pad_end[-1] // moe_block).astype(jnp.int32)
    blk = jnp.minimum(jnp.arange(n_blocks, dtype=jnp.int32), n_active - 1) * moe_block
    block_expert = jnp.minimum(jnp.searchsorted(pad_end, blk, side='right'), n_experts - 1).astype(jnp.int32)
    return row_token, row_gate, dest.reshape(n, TOP_K), block_expert, n_active.reshape(1)


def kernel(x_prompt, mem_prompt, x_sample, cache_kv_latent, cache_k_rope, page_table, state_hgrn, cache_mem_k, cache_mem_v, norm_mix, w_in, q_norm, w_uq, kv_norm, w_uk, w_uv, lb_logits, hg_norm, w_out_mla, w_out_hgrn, w_out, norm_cross, norm_mem, w_cq, w_ckv, w_co, norm_ffn, w_router_group, b_router_group, w_router_expert, b_router_expert, w_expert_gate_up, w_expert_down, norm_final):
    batch, seq, d = x_prompt.shape
    dec_batch, dec_seq, _ = x_sample.shape
    depth = w_in.shape[0]
    assert depth == 1
    page = cache_kv_latent.shape[2]
    kv_lora = cache_kv_latent.shape[3]
    rope = cache_k_rope.shape[3]
    half = rope // 2
    past_len = page_table.shape[1] * page
    q_lora = q_norm.shape[1]
    hg_heads, hg_dk = state_hgrn.shape[2], state_hgrn.shape[3]
    hgw = hg_heads * hg_dk
    mla_heads = (w_uq.shape[2] - w_uk.shape[2]) // rope
    nope = w_uk.shape[2] // mla_heads
    v_head = w_uv.shape[2] // mla_heads
    mem_tokens, x_heads, x_dh = cache_mem_k.shape[2], cache_mem_k.shape[3], cache_mem_k.shape[4]
    xw = x_heads * x_dh
    n_groups = w_router_group.shape[2]
    n_experts = w_router_expert.shape[2]
    mp = batch * seq
    ms = dec_batch * dec_seq
    m_all = mp + ms
    mla_scale = (nope + rope) ** -0.5
    assert nope == LANES and 2 * half <= LANES and hg_dk == LANES

    n_front = q_lora + kv_lora + rope
    off_cq, off_ckv, off_kr = 0, q_lora, q_lora + kv_lora
    off_hq, off_hf, off_hi, off_hg = 0, hgw, 2 * hgw, 3 * hgw
    off_ga, off_gb = 4 * hgw, 4 * hgw + d
    w_in_t = jnp.swapaxes(w_in[0], 0, 1)
    w_back = w_in_t[n_front:].astype(BF16)
    w_front = jnp.concatenate([w_in_t[:n_front], jnp.zeros((LANES - rope, d), F32)], axis=0).astype(BF16)

    x0 = jnp.concatenate([x_prompt.reshape(mp, d), x_sample.reshape(ms, d)], axis=0)
    pos = jnp.concatenate([jnp.tile(jnp.arange(seq), batch), jnp.tile(past_len + jnp.arange(dec_seq), dec_batch)])
    cos_t, sin_t = _rope_tables(pos, half)

    tm_in = _tile(m_all, 1152, unit=2 * SUBLANES)
    z = _matmul(x0, w_back, w_is_nk=True, gain=norm_mix[0], tm=tm_in, tn_cap=1024, name="mm_in")
    z_front = _matmul(x0, w_front, w_is_nk=True, gain=norm_mix[0], tm=tm_in, tn_cap=1024, name="mm_in_front")
    lat, kr_pad = _lat_kr(z_front, kv_norm[0], cos_t, sin_t, off_ckv=off_ckv, off_kr=off_kr, kv_lora=kv_lora,
                          half=half)
    w_uq_p = jnp.pad(w_uq[0].reshape(q_lora, mla_heads, nope + rope), ((0, 0), (0, 0), (0, LANES - rope)))
    w_uq_p = w_uq_p.reshape(q_lora, mla_heads * (nope + LANES)).astype(BF16)
    q_all, q_tok = _q_proj(z_front, w_uq_p, q_norm[0], cos_t, sin_t, off_cq=off_cq, q_lora=q_lora, heads=mla_heads,
                           nope=nope, half=half, scale=mla_scale, sample_row0=mp)
    w_uk_lo, w_uv_lo = w_uk[0].astype(BF16), w_uv[0].astype(BF16)

    k_all, v_all = _kv_proj(lat, w_uk_lo, w_uv_lo, kr_pad, rows=mp, heads=mla_heads, nope=nope)
    a_all = _flash(q_all, k_all, v_all, jnp.zeros((m_all, mla_heads * v_head), BF16), batch=batch, seq=seq,
                   heads=mla_heads, dv=v_head)

    rows_s = dec_seq * mla_heads
    q_lat = _q_lat(q_tok, w_uk_lo, heads=mla_heads, nope=nope).reshape(dec_batch, rows_s, kv_lora)
    q_rope_s = q_tok.reshape(ms, mla_heads, nope + LANES)[:, :, nope:nope + rope].astype(F32)
    q_rope_s = q_rope_s.reshape(dec_batch, rows_s, rope)
    lat_s = lat[mp:].reshape(dec_batch, dec_seq, kv_lora)
    kr_s = kr_pad[mp:, :rope].reshape(dec_batch, dec_seq, rope)
    o_lat = _paged_attention(page_table, q_lat, q_rope_s, lat_s, kr_s,
                             cache_kv_latent.reshape(-1, page, kv_lora),
                             jnp.swapaxes(cache_k_rope, 2, 3).reshape(-1, rope, page),
                             t_new=dec_seq)
    a_all = _o_uv(o_lat.reshape(ms, mla_heads * kv_lora), w_uv_lo, a_all, heads=mla_heads, dv=v_head, row0=mp)

    lower_bounds = jnp.cumsum(jax.nn.softmax(lb_logits.astype(F32), axis=0), axis=0)
    lb = lower_bounds[0]
    chunk_p = math.gcd(HG_CHUNK, seq)
    hg_offs = (off_hq, off_hf, off_hi, off_hg)
    b_all, s_p = _hgrn(z, lb, hg_norm[0], jnp.zeros((m_all, hgw), BF16), offs=hg_offs, n_seq=batch, rows=seq,
                       row_block=_tile(seq, 256, unit=chunk_p), heads_total=hg_heads, heads=math.gcd(16, hg_heads),
                       dk=hg_dk, chunk=chunk_p)
    b_all, s_s = _hgrn_step(z, lb, hg_norm[0], state_hgrn[0], b_all, offs=hg_offs, row0=mp, n_seq_total=dec_batch,
                            t_new=dec_seq, heads=hg_heads, dk=hg_dk)

    merged = _gated_merge(a_all, b_all, z, w_out_mla[0].astype(BF16), w_out_hgrn[0].astype(BF16), off_ga=off_ga,
                          off_gb=off_gb, tm=tm_in)
    x1 = _matmul(merged, w_out[0].astype(BF16), res=x0, tm=tm_in, name="mm_out")

    kv_mem = _matmul(mem_prompt.reshape(batch * mem_tokens, d), w_ckv[0], gain=norm_mem[0], tm=batch * mem_tokens,
                     name="mem_kv")
    mk_p, mv_p = kv_mem[:, :xw].reshape(batch, mem_tokens, xw), kv_mem[:, xw:].reshape(batch, mem_tokens, xw)
    qx = _matmul(x1, w_cq[0], gain=norm_cross[0], name="cross_q")
    tq_x = 512
    ox_p = _cross_attention(qx[:mp].reshape(mp // tq_x, tq_x, xw), mk_p, mv_p, heads=x_heads,
                            groups_per_mem=seq // tq_x)
    rows_pad = -(-dec_seq // SUBLANES) * SUBLANES
    qx_s = jnp.pad(qx[mp:].reshape(dec_batch, dec_seq, xw), ((0, 0), (0, rows_pad - dec_seq), (0, 0)))
    ox_s = _cross_attention_step(qx_s, cache_mem_k.reshape(dec_batch, mem_tokens * x_heads, x_dh),
                                 cache_mem_v.reshape(dec_batch, mem_tokens * x_heads, x_dh), heads=x_heads,
                                 n_seq=math.gcd(4, dec_batch))
    ox = jnp.concatenate([ox_p.reshape(mp, xw), ox_s[:, :dec_seq].reshape(ms, xw)], axis=0)
    x2 = _matmul(ox, w_co[0].astype(BF16), res=x1, tm=tm_in, name="cross_out")

    n_r = n_groups + n_experts
    w_r = jnp.concatenate([jnp.swapaxes(w_router_group[0], 0, 1), jnp.swapaxes(w_router_expert[0], 0, 1),
                           jnp.zeros((-n_r % LANES, d), F32)], axis=0)
    logits, h_ffn = _router(x2, w_r, norm_ffn[0])
    moe_block = 256

    def route(lg):
        return _route(lg, b_router_group[0], b_router_expert[0], n_groups, n_experts, moe_block)

    row_token, row_gate, dest, block_expert, n_active = route(logits)
    y_sorted = _moe_ffn(block_expert, n_active, h_ffn[row_token], row_gate[:, None],
                        w_expert_gate_up[0], w_expert_down[0], block=moe_block)
    y_p, y_s = _final_norm(x2, y_sorted[dest[:, 0]], y_sorted[dest[:, 1]], norm_final, sample_row0=mp)

    y_prompt = y_p.reshape(batch, seq, d)
    y_sample = y_s.reshape(dec_batch, dec_seq, d)
    n_prompt_pages = mp // page
    kv_latent_prompt = lat[:mp].reshape(depth, n_prompt_pages, page, kv_lora)
    k_rope_prompt = kr_pad[:mp, :rope].reshape(depth, n_prompt_pages, page, rope)
    state_hgrn_prompt = s_p[None]
    mem_k_prompt = mk_p.reshape(depth, batch, mem_tokens, x_heads, x_dh)
    mem_v_prompt = mv_p.reshape(depth, batch, mem_tokens, x_heads, x_dh)
    kv_latent_sample = lat_s[None]
    k_rope_sample = kr_s[None]
    state_hgrn_sample = s_s[None]
    return (y_prompt, y_sample, kv_latent_prompt, k_rope_prompt, state_hgrn_prompt, mem_k_prompt, mem_v_prompt,
            kv_latent_sample, k_rope_sample, state_hgrn_sample)
```
